```python
import math
import jax, jax.numpy as jnp
from jax import lax
import numpy as np

D_MODEL = 2048
BATCH = 4
SEQ = 2048
DEPTH = 4

GRID_W = 64
CTX_LEN = 256
N_MIXERS = 3
D_FF = 4 * D_MODEL
EPS = 1e-6

N_A = len(range(0, DEPTH, N_MIXERS))
N_B = len(range(1, DEPTH, N_MIXERS))
N_C = len(range(2, DEPTH, N_MIXERS))

NA_HEADS = 16
NA_HEAD_DIM = D_MODEL // NA_HEADS
NA_WIN_ROWS = 8
NA_WIN_COLS = 16

SSM_D_INNER = 2 * D_MODEL
SSM_HEAD_DIM = 64
SSM_HEADS = SSM_D_INNER // SSM_HEAD_DIM
SSM_GROUPS = 8
SSM_HEADS_PER_GROUP = SSM_HEADS // SSM_GROUPS
SSM_STATE = 128
SSM_CONV = 5
SSM_CHUNK = 128
SSM_CONV_DIM = SSM_D_INNER + 2 * SSM_GROUPS * SSM_STATE
SSM_IN_DIM = SSM_D_INNER + SSM_CONV_DIM + 2 * SSM_HEADS

DA_HEADS = 8
DA_HEAD_DIM = D_MODEL // DA_HEADS // 2
DA_Q_BLOCK = 128
DA_SUBLN_EPS = 1e-5
ROPE_BASE = 10000.0

kernel_name = 'hybrid_natten_ssd_diffattn_dit'


def rmsnorm(x, w, eps=EPS):
    xf = x.astype(jnp.float32)
    y = xf * lax.rsqrt(jnp.mean(xf * xf, axis=-1, keepdims=True) + eps)
    return (y * w.astype(jnp.float32)).astype(x.dtype)


def adaln(cond, w, b):
    m = jax.nn.silu(cond) @ w + b
    return jnp.split(m, 6, axis=-1)


def modulate(x, shift, scale):
    return x * (1 + scale) + shift


def sq_relu_mlp(u, w1, w2):
    return jnp.square(jax.nn.relu(u @ w1)) @ w2


def rope_2d_tables(n_tokens, head_dim, dtype):
    t = jnp.arange(n_tokens)
    pos = jnp.stack([t // GRID_W, t % GRID_W], axis=-1).astype(jnp.float32)
    n_freq = head_dim // 4
    inv_freq = ROPE_BASE ** (-jnp.arange(n_freq, dtype=jnp.float32) / n_freq)
    ang = pos[:, :, None] * inv_freq
    return jnp.cos(ang).astype(dtype), jnp.sin(ang).astype(dtype)


def apply_rope_2d(x, cos, sin):
    Bsz, S, H, dh = x.shape
    xa = x.reshape(Bsz, S, H, 2, 2, dh // 4)
    x1, x2 = xa[..., 0, :], xa[..., 1, :]
    cs, sn = cos[None, :, None], sin[None, :, None]
    return jnp.stack([x1 * cs - x2 * sn, x2 * cs + x1 * sn], axis=-2).reshape(Bsz, S, H, dh)


def neighbourhood_attention(u, uc, w_qkv, w_o, rpb, with_ctx_out):
    Bsz, S, _ = u.shape
    rows = S // GRID_W
    kr = min(NA_WIN_ROWS, rows)
    H, dh = NA_HEADS, NA_HEAD_DIM
    hd = H * dh
    scale = dh ** -0.5
    f32 = jnp.float32
    q, k, v = jnp.split(u @ w_qkv, 3, axis=-1)
    q, k, v = (t.reshape(Bsz, rows, GRID_W, H, dh) for t in (q, k, v))
    kc, vc = jnp.split(uc @ w_qkv[:, hd:], 2, axis=-1)
    Lc = uc.shape[1]
    kc, vc = kc.reshape(Bsz, Lc, H, dh), vc.reshape(Bsz, Lc, H, dh)

    col = jnp.arange(GRID_W)
    col_start = jnp.clip(col - NA_WIN_COLS // 2, 0, GRID_W - NA_WIN_COLS)
    col_mask = (col[None, :] >= col_start[:, None]) & (col[None, :] < col_start[:, None] + NA_WIN_COLS)
    dc_idx = jnp.clip(col[None, :] - col[:, None], -(NA_WIN_COLS - 1), NA_WIN_COLS - 1) + NA_WIN_COLS - 1
    rpb_cols = rpb[:, :, dc_idx]

    def row_block(r):
        rs = jnp.clip(r - kr // 2, 0, rows - kr)
        q_r = lax.dynamic_index_in_dim(q, r, axis=1, keepdims=False)
        k_r = lax.dynamic_slice_in_dim(k, rs, kr, axis=1)
        v_r = lax.dynamic_slice_in_dim(v, rs, kr, axis=1)
        dr_idx = rs + jnp.arange(kr) - r + NA_WIN_ROWS - 1
        bias = jnp.take(rpb_cols, dr_idx, axis=1).transpose(0, 2, 1, 3)
        s_loc = jnp.einsum('bqhd,bjkhd->bhqjk', q_r, k_r).astype(f32) * scale + bias[None].astype(f32)
        s_loc = jnp.where(col_mask[:, None, :], s_loc, -jnp.inf).reshape(Bsz, H, GRID_W, kr * GRID_W)
        s_ctx = jnp.einsum('bqhd,bchd->bhqc', q_r, kc).astype(f32) * scale
        p = jax.nn.softmax(jnp.concatenate([s_loc, s_ctx], axis=-1), axis=-1).astype(v.dtype)
        p_loc = p[..., :kr * GRID_W].reshape(Bsz, H, GRID_W, kr, GRID_W)
        p_ctx = p[..., kr * GRID_W:]
        return jnp.einsum('bhqjk,bjkhd->bqhd', p_loc, v_r) + jnp.einsum('bhqc,bchd->bqhd', p_ctx, vc)

    o = lax.map(row_block, jnp.arange(rows))
    y = jnp.moveaxis(o, 0, 1).reshape(Bsz, S, hd) @ w_o
    yc = None
    if with_ctx_out:
        qc = (uc @ w_qkv[:, :hd]).reshape(Bsz, Lc, H, dh)
        s = jnp.einsum('bqhd,bkhd->bhqk', qc, kc).astype(f32) * scale
        p = jax.nn.softmax(s, axis=-1).astype(vc.dtype)
        yc = jnp.einsum('bhqk,bkhd->bqhd', p, vc).reshape(Bsz, Lc, hd) @ w_o
    return y, yc


def centred_depthwise_conv(x, w, b):
    pad = w.shape[0] // 2
    xp = jnp.pad(x, ((0, 0), (pad, pad), (0, 0)))
    y = lax.conv_general_dilated(xp, w[:, None, :], window_strides=(1,), padding='VALID',
                                 dimension_numbers=('NWC', 'WIO', 'NWC'), feature_group_count=x.shape[-1])
    return y + b


def ssd_chunked(x, dt, A, Bm, Cm, h0, with_output=True):
    f32 = jnp.float32
    Bsz, L, G, R, P = x.shape
    N = Bm.shape[-1]
    Q = SSM_CHUNK
    nc = L // Q
    xdt = (x.astype(f32) * dt[..., None]).reshape(Bsz, nc, Q, G, R, P)
    Bc = Bm.astype(f32).reshape(Bsz, nc, Q, G, N)
    Cc = Cm.astype(f32).reshape(Bsz, nc, Q, G, N)
    a_cum = jnp.cumsum((dt * A).reshape(Bsz, nc, Q, G, R), axis=2)
    a_last = a_cum[:, :, -1]
    decay_to_end = jnp.exp(a_last[:, :, None] - a_cum)
    states = jnp.einsum('bcjgn,bcjgr,bcjgrp->bcgrpn', Bc, decay_to_end, xdt)

    def step(h, inp):
        dec, st = inp
        return dec[..., None, None] * h + st, h

    h_final, h_starts = lax.scan(step, h0, (jnp.moveaxis(jnp.exp(a_last), 1, 0), jnp.moveaxis(states, 1, 0)))
    if not with_output:
        return None, h_final
    h_starts = jnp.moveaxis(h_starts, 0, 1)
    seg = a_cum[:, :, :, None] - a_cum[:, :, None]
    lower = jnp.tril(jnp.ones((Q, Q), dtype=bool))[:, :, None, None]
    lmat = jnp.exp(jnp.where(lower, seg, -jnp.inf))
    cb = jnp.einsum('bcign,bcjgn->bcijg', Cc, Bc)
    y_diag = jnp.einsum('bcijg,bcijgr,bcjgrp->bcigrp', cb, lmat, xdt)
    y_off = jnp.einsum('bcign,bcgrpn,bcigr->bcigrp', Cc, h_starts, jnp.exp(a_cum))
    y = (y_diag + y_off).reshape(Bsz, L, G, R, P).astype(x.dtype)
    return y, h_final


def mamba2_bidirectional(u, uc, w_in, conv_w, conv_b, dt_bias, a_log, d_skip, norm_w, w_out, with_ctx_out):
    f32 = jnp.float32
    G, R, P, N, DI = SSM_GROUPS, SSM_HEADS_PER_GROUP, SSM_HEAD_DIM, SSM_STATE, SSM_D_INNER
    A = -jnp.exp(a_log.astype(f32)).reshape(2, G, R)
    dtb = dt_bias.astype(f32).reshape(2, G, R)

    def project(h, need_gate):
        Bsz, L, _ = h.shape
        proj = h @ (w_in if need_gate else w_in[:, DI:])
        z = proj[..., :DI] if need_gate else None
        proj = proj[..., DI:] if need_gate else proj
        xbc = jax.nn.silu(centred_depthwise_conv(proj[..., :SSM_CONV_DIM], conv_w, conv_b))
        xs = xbc[..., :DI].reshape(Bsz, L, G, R, P)
        Bm = xbc[..., DI:DI + G * N].reshape(Bsz, L, G, N)
        Cm = xbc[..., DI + G * N:].reshape(Bsz, L, G, N)
        dt = jax.nn.softplus(proj[..., SSM_CONV_DIM:].astype(f32).reshape(Bsz, L, 2, G, R) + dtb)
        return z, xs, Bm, Cm, dt

    zl, xl, Bl, Cl, dtl = project(u, True)
    zc, xc, Bc, Cc, dtc = project(uc, with_ctx_out)
    h0 = jnp.zeros((u.shape[0], G, R, P, N), f32)
    flip = lambda t: jnp.flip(t, axis=1)
    yc_f, hc_f = ssd_chunked(xc, dtc[:, :, 0], A[0], Bc, Cc, h0, with_ctx_out)
    yc_b, hc_b = ssd_chunked(flip(xc), flip(dtc[:, :, 1]), A[1], flip(Bc), flip(Cc), h0, with_ctx_out)
    yl_f, _ = ssd_chunked(xl, dtl[:, :, 0], A[0], Bl, Cl, hc_f)
    yl_b, _ = ssd_chunked(flip(xl), flip(dtl[:, :, 1]), A[1], flip(Bl), flip(Cl), hc_b)

    def finish(y, xs, z):
        Bsz, L = y.shape[:2]
        y = (y + xs * d_skip.reshape(G, R, 1)).reshape(Bsz, L, DI) * jax.nn.silu(z)
        y = rmsnorm(y.reshape(Bsz, L, G, DI // G), norm_w.reshape(G, DI // G)).reshape(Bsz, L, DI)
        return y @ w_out

    y = finish(yl_f + flip(yl_b), xl, zl)
    yc = finish(yc_f + flip(yc_b), xc, zc) if with_ctx_out else None
    return y, yc


def diff_attention(u, uc, w_qkv, w_o, lam, subln_w, lambda_init, cos, sin, with_ctx_out):
    f32 = jnp.float32
    Bsz, S, _ = u.shape
    Lc = uc.shape[1]
    H, dh = DA_HEADS, DA_HEAD_DIM
    qk_w = 2 * H * dh
    q, k, v = jnp.split(u @ w_qkv, [qk_w, 2 * qk_w], axis=-1)
    q = apply_rope_2d(q.reshape(Bsz, S, 2 * H, dh), cos, sin)
    k = apply_rope_2d(k.reshape(Bsz, S, 2 * H, dh), cos, sin)
    v = v.reshape(Bsz, S, H, 2 * dh)
    kc, vc = jnp.split(uc @ w_qkv[:, qk_w:], [qk_w], axis=-1)
    kc, vc = kc.reshape(Bsz, Lc, 2 * H, dh), vc.reshape(Bsz, Lc, H, 2 * dh)
    lam = lam.astype(f32)
    lam_full = jnp.exp(jnp.sum(lam[0] * lam[1])) - jnp.exp(jnp.sum(lam[2] * lam[3])) + lambda_init
    scale = dh ** -0.5

    def attend(qb, keys, vals):
        Lq = qb.shape[1]
        s = jnp.einsum('bqhd,bkhd->bhqk', qb, keys).astype(f32) * scale
        p = jax.nn.softmax(s, axis=-1).reshape(Bsz, H, 2, Lq, keys.shape[1])
        a = (p[:, :, 0] - lam_full * p[:, :, 1]).astype(vals.dtype)
        return jnp.einsum('bhqk,bkhe->bqhe', a, vals)

    def finish(o, L):
        o = rmsnorm(o, subln_w, DA_SUBLN_EPS) * (1 - lambda_init)
        return o.reshape(Bsz, L, H * 2 * dh) @ w_o

    keys = jnp.concatenate([k, kc], axis=1)
    vals = jnp.concatenate([v, vc], axis=1)
    nb = S // DA_Q_BLOCK
    qb = jnp.moveaxis(q.reshape(Bsz, nb, DA_Q_BLOCK, 2 * H, dh), 1, 0)
    o = lax.map(lambda t: attend(t, keys, vals), qb)
    y = finish(jnp.moveaxis(o, 0, 1).reshape(Bsz, S, H, 2 * dh), S)
    yc = None
    if with_ctx_out:
        qc = (uc @ w_qkv[:, :qk_w]).reshape(Bsz, Lc, 2 * H, dh)
        yc = finish(attend(qc, kc, vc), Lc)
    return y, yc


def setup_inputs(seed: int = 0) -> dict:
    key = jax.random.key(seed)
    ks = iter(jax.random.split(key, 32))
    nrm = lambda shape, s: jax.random.normal(next(ks), shape, jnp.float32) * s
    D = D_MODEL
    x = nrm((BATCH, SEQ, D), 1.0)
    c = nrm((BATCH, D), 1.0)
    ctx = nrm((BATCH, CTX_LEN, D), 1.0)
    c_ctx = nrm((D,), 1.0)
    ada_w = nrm((DEPTH, D, 6 * D), 0.5 * D ** -0.5)
    ada_b = nrm((DEPTH, 6 * D), 0.01)
    norm_mix_w = 1.0 + nrm((DEPTH, D), 0.01)
    norm_mlp_w = 1.0 + nrm((DEPTH, D), 0.01)
    mlp_w1 = nrm((DEPTH, D, D_FF), D ** -0.5)
    mlp_w2 = nrm((DEPTH, D_FF, D), D_FF ** -0.5)
    na_w_qkv = nrm((N_A, D, 3 * NA_HEADS * NA_HEAD_DIM), D ** -0.5)
    na_w_o = nrm((N_A, NA_HEADS * NA_HEAD_DIM, D), (NA_HEADS * NA_HEAD_DIM) ** -0.5)
    na_rpb = nrm((N_A, NA_HEADS, 2 * NA_WIN_ROWS - 1, 2 * NA_WIN_COLS - 1), 0.1)
    ssm_w_in = nrm((N_B, D, SSM_IN_DIM), D ** -0.5)
    ssm_conv_w = nrm((N_B, SSM_CONV, SSM_CONV_DIM), SSM_CONV ** -0.5)
    ssm_conv_b = nrm((N_B, SSM_CONV_DIM), 0.01)
    dt0 = jnp.exp(jax.random.uniform(next(ks), (N_B, 2, SSM_HEADS), jnp.float32, math.log(1e-3), math.log(1e-1)))
    ssm_dt_bias = dt0 + jnp.log(-jnp.expm1(-dt0))
    ssm_a_log = jnp.log(jax.random.uniform(next(ks), (N_B, 2, SSM_HEADS), jnp.float32, 1.0, 16.0))
    ssm_d = 1.0 + nrm((N_B, SSM_HEADS), 0.01)
    ssm_norm_w = 1.0 + nrm((N_B, SSM_D_INNER), 0.01)
    ssm_w_out = nrm((N_B, SSM_D_INNER, D), SSM_D_INNER ** -0.5)
    da_w_qkv = nrm((N_C, D, 3 * 2 * DA_HEADS * DA_HEAD_DIM), D ** -0.5)
    da_w_o = nrm((N_C, 2 * DA_HEADS * DA_HEAD_DIM, D), (2 * DA_HEADS * DA_HEAD_DIM) ** -0.5)
    da_lambda = nrm((N_C, 4, DA_HEAD_DIM), 0.1)
    da_subln_w = 1.0 + nrm((N_C, 2 * DA_HEAD_DIM), 0.01)
    final_norm_w = 1.0 + nrm((D,), 0.01)
    return {'x': x, 'c': c, 'ctx': ctx, 'c_ctx': c_ctx, 'ada_w': ada_w, 'ada_b': ada_b,
            'norm_mix_w': norm_mix_w, 'norm_mlp_w': norm_mlp_w, 'mlp_w1': mlp_w1, 'mlp_w2': mlp_w2,
            'na_w_qkv': na_w_qkv, 'na_w_o': na_w_o, 'na_rpb': na_rpb,
            'ssm_w_in': ssm_w_in, 'ssm_conv_w': ssm_conv_w, 'ssm_conv_b': ssm_conv_b,
            'ssm_dt_bias': ssm_dt_bias, 'ssm_a_log': ssm_a_log, 'ssm_d': ssm_d,
            'ssm_norm_w': ssm_norm_w, 'ssm_w_out': ssm_w_out,
            'da_w_qkv': da_w_qkv, 'da_w_o': da_w_o, 'da_lambda': da_lambda, 'da_subln_w': da_subln_w,
            'final_norm_w': final_norm_w}


def reference(x, c, ctx, c_ctx, ada_w, ada_b, norm_mix_w, norm_mlp_w, mlp_w1, mlp_w2,
              na_w_qkv, na_w_o, na_rpb, ssm_w_in, ssm_conv_w, ssm_conv_b, ssm_dt_bias, ssm_a_log,
              ssm_d, ssm_norm_w, ssm_w_out, da_w_qkv, da_w_o, da_lambda, da_subln_w, final_norm_w):
    h, hc = x, ctx
    cos, sin = rope_2d_tables(x.shape[1], DA_HEAD_DIM, x.dtype)
    for i in range(DEPTH):
        last = i == DEPTH - 1
        mixer, j = i % N_MIXERS, i // N_MIXERS
        sh_a, sc_a, g_a, sh_m, sc_m, g_m = adaln(c[:, None, :], ada_w[i], ada_b[i])
        csh_a, csc_a, cg_a, csh_m, csc_m, cg_m = adaln(c_ctx, ada_w[i], ada_b[i])
        u = modulate(rmsnorm(h, norm_mix_w[i]), sh_a, sc_a)
        uc = modulate(rmsnorm(hc, norm_mix_w[i]), csh_a, csc_a)
        if mixer == 0:
            y, yc = neighbourhood_attention(u, uc, na_w_qkv[j], na_w_o[j], na_rpb[j], not last)
        elif mixer == 1:
            y, yc = mamba2_bidirectional(u, uc, ssm_w_in[j], ssm_conv_w[j], ssm_conv_b[j], ssm_dt_bias[j],
                                         ssm_a_log[j], ssm_d[j], ssm_norm_w[j], ssm_w_out[j], not last)
        else:
            lambda_init = 0.8 - 0.6 * math.exp(-0.3 * i)
            y, yc = diff_attention(u, uc, da_w_qkv[j], da_w_o[j], da_lambda[j], da_subln_w[j],
                                   lambda_init, cos, sin, not last)
        h = h + g_a * y
        h = h + g_m * sq_relu_mlp(modulate(rmsnorm(h, norm_mlp_w[i]), sh_m, sc_m), mlp_w1[i], mlp_w2[i])
        if not last:
            hc = hc + cg_a * yc
            hc = hc + cg_m * sq_relu_mlp(modulate(rmsnorm(hc, norm_mlp_w[i]), csh_m, csc_m), mlp_w1[i], mlp_w2[i])
    return rmsnorm(h, final_norm_w)
```

```python
import functools
import math

import jax
import jax.numpy as jnp
import numpy as np
from jax import lax
from jax.experimental import pallas as pl
from jax.experimental.pallas import tpu as pltpu

F32 = jnp.float32
BF16 = jnp.bfloat16

D_MODEL = 2048
DEPTH = 4
GRID_W = 64
N_MIXERS = 3
EPS = 1e-6

NA_HEADS = 16
NA_HEAD_DIM = 128
NA_WIN_ROWS = 8
NA_WIN_COLS = 16
NA_GROUP_ROWS = 4
NA_KEY_ROWS = 3 * NA_GROUP_ROWS
NA_HEADS_PER_STEP = 4

SSM_D_INNER = 4096
SSM_HEAD_DIM = 64
SSM_HEADS = 64
SSM_GROUPS = 8
SSM_HEADS_PER_GROUP = 8
SSM_STATE = 128
SSM_CONV = 5
SSM_CHUNK = 128
SSM_CONV_DIM = SSM_D_INNER + 2 * SSM_GROUPS * SSM_STATE
SSM_GROUP_WIDTH = SSM_HEADS_PER_GROUP * SSM_HEAD_DIM

DA_HEADS = 8
DA_HEAD_DIM = 128
DA_SUBLN_EPS = 1e-5
ROPE_BASE = 10000.0

MASK_NEG = -1e30

V7X_VMEM_BYTES = 64 * 1024 * 1024
VMEM_LIMIT = (V7X_VMEM_BYTES * 3) // 4
SUBLANES_BF16 = 16


def _params(n_grid_dims):
    return pltpu.CompilerParams(
        dimension_semantics=("arbitrary",) * n_grid_dims, vmem_limit_bytes=VMEM_LIMIT)


def _sigmoid(x):
    return 1.0 / (1.0 + jnp.exp(-x))


def _dot(a, b):
    return jnp.dot(a, b, preferred_element_type=F32)


def _dot_nt(a, b):
    return lax.dot_general(a, b, (((1,), (1,)), ((), ())), preferred_element_type=F32)


def _split3(v):
    hi = v.astype(BF16)
    r1 = v - hi.astype(F32)
    mid = r1.astype(BF16)
    lo = (r1 - mid.astype(F32)).astype(BF16)
    return hi, mid, lo


def _mod_row(row0, t_lat, seq, n_batch):
    return jnp.where(row0 < t_lat, row0 // seq, n_batch)


def _adaln_kernel(c_ref, w_ref, b_ref, o_ref):
    x = c_ref[...]
    x = (x * _sigmoid(x)).astype(BF16)
    o_ref[0] = _dot(x, w_ref[0].astype(BF16)) + b_ref[0]


def _adaln_all(cond, ada_w, ada_b):
    depth, d, n = ada_w.shape
    bn = 1024
    return pl.pallas_call(
        _adaln_kernel,
        grid=(depth, n // bn),
        in_specs=[
            pl.BlockSpec((8, d), lambda l, j: (0, 0)),
            pl.BlockSpec((1, d, bn), lambda l, j: (l, 0, j)),
            pl.BlockSpec((1, 1, bn), lambda l, j: (l, 0, j)),
        ],
        out_specs=pl.BlockSpec((1, 8, bn), lambda l, j: (l, 0, j)),
        out_shape=jax.ShapeDtypeStruct((depth, 8, n), F32),
        compiler_params=_params(2),
        name="adaln",
    )(cond, ada_w, ada_b.reshape(depth, 1, n))


def _norm_mod_kernel(h_ref, nw_ref, mod_ref, o_ref, *, bm, shift_k, t_lat, seq, n_batch):
    d = h_ref.shape[1]
    midx = _mod_row(pl.program_id(0) * bm, t_lat, seq, n_batch)
    x = h_ref[...]
    y = x * lax.rsqrt(jnp.mean(x * x, axis=-1, keepdims=True) + EPS) * nw_ref[...]
    shift = mod_ref[pl.ds(midx, 1), shift_k * d:(shift_k + 1) * d]
    scale = mod_ref[pl.ds(midx, 1), (shift_k + 1) * d:(shift_k + 2) * d]
    o_ref[...] = (y * (1.0 + scale) + shift).astype(o_ref.dtype)


def _norm_mod(h, norm_w, mod_l, shift_k, m_rows, geom):
    t, d = h.shape
    bm = 256
    kern = functools.partial(_norm_mod_kernel, bm=bm, shift_k=shift_k, **geom)
    return pl.pallas_call(
        kern,
        grid=(m_rows // bm,),
        in_specs=[
            pl.BlockSpec((bm, d), lambda i: (i, 0)),
            pl.BlockSpec((1, d), lambda i: (0, 0)),
            pl.BlockSpec(mod_l.shape, lambda i: (0, 0)),
        ],
        out_specs=pl.BlockSpec((bm, d), lambda i: (i, 0)),
        out_shape=jax.ShapeDtypeStruct((m_rows, d), BF16),
        compiler_params=_params(1),
        name="norm_mod",
    )(h, norm_w.reshape(1, d), mod_l)


def _final_norm_kernel(h_ref, nw_ref, o_ref):
    x = h_ref[...]
    o_ref[...] = x * lax.rsqrt(jnp.mean(x * x, axis=-1, keepdims=True) + EPS) * nw_ref[...]


def _final_norm(h, norm_w, m_rows):
    d = h.shape[1]
    bm = 256
    return pl.pallas_call(
        _final_norm_kernel,
        grid=(m_rows // bm,),
        in_specs=[pl.BlockSpec((bm, d), lambda i: (i, 0)), pl.BlockSpec((1, d), lambda i: (0, 0))],
        out_specs=pl.BlockSpec((bm, d), lambda i: (i, 0)),
        out_shape=jax.ShapeDtypeStruct((m_rows, d), F32),
        compiler_params=_params(1),
        name="final_norm",
    )(h, norm_w.reshape(1, d))


def _mm_blocks(m_rows, k, n):
    bm = 1024 if m_rows % 1024 == 0 else 256
    bn = 1024 if n % 1024 == 0 else (512 if n % 512 == 0 else 128)
    if k >= 8192:
        bm, bn = min(bm, 512), min(bn, 512)
    elif k >= 4096:
        bn = min(bn, 512)
    return bm, bn


def _mm_kernel(x_ref, w_ref, o_ref, *, act):
    acc = _dot(x_ref[...], w_ref[...])
    if act == "relu2":
        acc = jnp.maximum(acc, 0.0)
        acc = acc * acc
    o_ref[...] = acc.astype(o_ref.dtype)


def _mm(x, w, m_rows, out_dtype=BF16, act=None):
    k, n = w.shape
    bm, bn = _mm_blocks(m_rows, k, n)
    return pl.pallas_call(
        functools.partial(_mm_kernel, act=act),
        grid=(m_rows // bm, n // bn),
        in_specs=[pl.BlockSpec((bm, k), lambda i, j: (i, 0)), pl.BlockSpec((k, bn), lambda i, j: (0, j))],
        out_specs=pl.BlockSpec((bm, bn), lambda i, j: (i, j)),
        out_shape=jax.ShapeDtypeStruct((m_rows, n), out_dtype),
        compiler_params=_params(2),
        name="mm_" + (act or "plain"),
    )(x, w)


def _mm_gated_kernel(x_ref, w_ref, res_ref, gate_ref, o_ref, *, bm, t_lat, seq, n_batch):
    midx = _mod_row(pl.program_id(0) * bm, t_lat, seq, n_batch)
    acc = _dot(x_ref[...], w_ref[...])
    o_ref[...] = res_ref[...] + gate_ref[pl.ds(midx, 1), :] * acc


def _mm_gated(x, w, res, mod_l, gate_k, m_rows, geom):
    k, n = w.shape
    bm, bn = _mm_blocks(m_rows, k, n)
    gate_blk0 = gate_k * (n // bn)
    return pl.pallas_call(
        functools.partial(_mm_gated_kernel, bm=bm, **geom),
        grid=(m_rows // bm, n // bn),
        in_specs=[
            pl.BlockSpec((bm, k), lambda i, j: (i, 0)),
            pl.BlockSpec((k, bn), lambda i, j: (0, j)),
            pl.BlockSpec((bm, bn), lambda i, j: (i, j)),
            pl.BlockSpec((8, bn), lambda i, j: (0, gate_blk0 + j)),
        ],
        out_specs=pl.BlockSpec((bm, bn), lambda i, j: (i, j)),
        out_shape=jax.ShapeDtypeStruct(res.shape, F32),
        input_output_aliases={2: 0},
        compiler_params=_params(2),
        name="mm_gated",
    )(x, w, res, mod_l)


def _mm_rope_kernel(x_ref, w_ref, cos_ref, sin_ref, o_ref, *, rope_row_blocks, rope_col_blocks):
    acc = _dot(x_ref[...], w_ref[...])
    do_rope = (pl.program_id(0) < rope_row_blocks) & (pl.program_id(1) < rope_col_blocks)

    @pl.when(do_rope)
    def _():
        n = acc.shape[1]
        lane = lax.broadcasted_iota(jnp.int32, acc.shape, 1)
        partner = jnp.where((lane % 64) < 32, pltpu.roll(acc, n - 32, 1), pltpu.roll(acc, 32, 1))
        o_ref[...] = (acc * cos_ref[...] + partner * sin_ref[...]).astype(o_ref.dtype)

    @pl.when(jnp.logical_not(do_rope))
    def _():
        o_ref[...] = acc.astype(o_ref.dtype)


def _mm_rope(x, w, cos_t, sin_t, t_lat, rope_cols):
    m_rows = x.shape[0]
    k, n = w.shape
    bm, bn = _mm_blocks(m_rows, k, n)
    seq_blocks = cos_t.shape[0] // bm
    kern = functools.partial(_mm_rope_kernel, rope_row_blocks=t_lat // bm, rope_col_blocks=rope_cols // bn)
    return pl.pallas_call(
        kern,
        grid=(m_rows // bm, n // bn),
        in_specs=[
            pl.BlockSpec((bm, k), lambda i, j: (i, 0)),
            pl.BlockSpec((k, bn), lambda i, j: (0, j)),
            pl.BlockSpec((bm, bn), lambda i, j: (i % seq_blocks, 0)),
            pl.BlockSpec((bm, bn), lambda i, j: (i % seq_blocks, 0)),
        ],
        out_specs=pl.BlockSpec((bm, bn), lambda i, j: (i, j)),
        out_shape=jax.ShapeDtypeStruct((m_rows, n), BF16),
        compiler_params=_params(2),
        name="mm_rope",
    )(x, w, cos_t, sin_t)


def _na_bias_table(rpb, rows):
    heads = rpb.shape[0]
    gq, gk = NA_GROUP_ROWS, NA_KEY_ROWS
    col = np.arange(GRID_W)
    col_start = np.clip(col - NA_WIN_COLS // 2, 0, GRID_W - NA_WIN_COLS)
    col_mask = (col[None, :] >= col_start[:, None]) & (col[None, :] < col_start[:, None] + NA_WIN_COLS)
    dc_idx = np.clip(col[None, :] - col[:, None], -(NA_WIN_COLS - 1), NA_WIN_COLS - 1) + NA_WIN_COLS - 1
    tabs = []
    for r0, ks in ((0, 0), (gq, 0), (rows - gq, rows - gk)):
        r = r0 + np.arange(gq)
        kr = ks + np.arange(gk)
        rs = np.clip(r - NA_WIN_ROWS // 2, 0, rows - NA_WIN_ROWS)
        valid = (kr[None, :] >= rs[:, None]) & (kr[None, :] < rs[:, None] + NA_WIN_ROWS)
        dr_idx = np.clip(kr[None, :] - r[:, None] + NA_WIN_ROWS - 1, 0, 2 * NA_WIN_ROWS - 2)
        b = rpb[:, dr_idx[:, None, :, None], dc_idx[None, :, None, :]]
        m = valid[:, None, :, None] & col_mask[None, :, None, :]
        tabs.append(jnp.where(m[None], b.astype(F32), MASK_NEG).reshape(heads, gq * GRID_W, gk * GRID_W))
    tabs.append(jnp.full_like(tabs[0], MASK_NEG))
    return jnp.stack(tabs)


def _na_kernel(q_ref, k0_ref, k1_ref, k2_ref, v0_ref, v1_ref, v2_ref, kc_ref, vc_ref, bias_ref, o_ref, *, scale):
    dh = NA_HEAD_DIM
    kb = q_ref.shape[0]
    for hh in range(NA_HEADS_PER_STEP):
        sl = slice(hh * dh, (hh + 1) * dh)
        q = q_ref[:, sl]
        s_loc = jnp.concatenate([_dot_nt(q, kr[:, sl]) for kr in (k0_ref, k1_ref, k2_ref)], axis=1)
        s_loc = s_loc * scale + bias_ref[0, hh]
        s_ctx = _dot_nt(q, kc_ref[:, sl]) * scale
        m = jnp.maximum(jnp.max(s_loc, axis=1, keepdims=True), jnp.max(s_ctx, axis=1, keepdims=True))
        p_loc = jnp.exp(s_loc - m)
        p_ctx = jnp.exp(s_ctx - m)
        denom = jnp.sum(p_loc, axis=1, keepdims=True) + jnp.sum(p_ctx, axis=1, keepdims=True)
        p_loc = p_loc.astype(BF16)
        o = _dot(p_ctx.astype(BF16), vc_ref[:, sl])
        for j, vr in enumerate((v0_ref, v1_ref, v2_ref)):
            o = o + _dot(p_loc[:, j * kb:(j + 1) * kb], vr[:, sl])
        o_ref[:, sl] = (o * (1.0 / denom)).astype(o_ref.dtype)


def _na_attention(qkv, bias_tab, n_batch, seq, lc, ctx_out):
    hd = NA_HEADS * NA_HEAD_DIM
    gtok = NA_GROUP_ROWS * GRID_W
    ng = seq // gtok
    assert lc == gtok and ng >= 4
    hw = NA_HEADS_PER_STEP * NA_HEAD_DIM
    nhq = NA_HEADS // NA_HEADS_PER_STEP
    t_lat = n_batch * seq
    out_rows = t_lat + (n_batch * lc if ctx_out else 0)

    def q_row(g, b):
        return jnp.where(g < ng, b * ng + g, n_batch * ng + b)

    def k_row(g, b, j):
        return b * ng + jnp.clip(g - 1, 0, ng - 3) + j

    def bias_type(g):
        return jnp.where(g == 0, 0, jnp.where(g < ng - 1, 1, jnp.where(g == ng - 1, 2, 3)))

    blk = (gtok, hw)
    in_specs = [pl.BlockSpec(blk, lambda h, g, b: (q_row(g, b), h))]
    for col0 in (nhq, 2 * nhq):
        for j in range(3):
            in_specs.append(pl.BlockSpec(blk, lambda h, g, b, j=j, col0=col0: (k_row(g, b, j), col0 + h)))
    in_specs.append(pl.BlockSpec(blk, lambda h, g, b: (n_batch * ng + b, nhq + h)))
    in_specs.append(pl.BlockSpec(blk, lambda h, g, b: (n_batch * ng + b, 2 * nhq + h)))
    in_specs.append(pl.BlockSpec((1, NA_HEADS_PER_STEP, gtok, 3 * gtok), lambda h, g, b: (bias_type(g), h, 0, 0)))
    return pl.pallas_call(
        functools.partial(_na_kernel, scale=NA_HEAD_DIM ** -0.5),
        grid=(nhq, ng + (1 if ctx_out else 0), n_batch),
        in_specs=in_specs,
        out_specs=pl.BlockSpec(blk, lambda h, g, b: (q_row(g, b), h)),
        out_shape=jax.ShapeDtypeStruct((out_rows, hd), BF16),
        compiler_params=_params(3),
        name="na_attention",
    )(*([qkv] * 9), bias_tab)


def _rope_tables(seq, width):
    t = jnp.arange(seq)
    pos = jnp.stack([t // GRID_W, t % GRID_W], axis=-1).astype(F32)
    n_freq = DA_HEAD_DIM // 4
    inv_freq = ROPE_BASE ** (-jnp.arange(n_freq, dtype=F32) / n_freq)
    ang = pos[:, :, None] * inv_freq
    cos = jnp.broadcast_to(jnp.cos(ang)[:, :, None, :], (seq, 2, 2, n_freq)).reshape(seq, DA_HEAD_DIM)
    sign = jnp.array([-1.0, 1.0], F32)[None, None, :, None]
    sin = (jnp.broadcast_to(jnp.sin(ang)[:, :, None, :], (seq, 2, 2, n_freq)) * sign).reshape(seq, DA_HEAD_DIM)
    reps = width // DA_HEAD_DIM
    return jnp.tile(cos, (1, reps)), jnp.tile(sin, (1, reps))


def _da_kernel(lam_ref, q_ref, kl_ref, kc_ref, vl_ref, vc_ref, sw_ref, o_ref, *, scale, nq_lat, out_scale):
    dh = DA_HEAD_DIM
    lam = lam_ref[0]
    lat_mask = jnp.where(pl.program_id(2) >= nq_lat, MASK_NEG, 0.0)
    probs = []
    for comp in range(2):
        sl = slice(comp * dh, (comp + 1) * dh)
        q = q_ref[:, sl]
        s_l = _dot_nt(q, kl_ref[:, sl]) * scale + lat_mask
        s_c = _dot_nt(q, kc_ref[:, sl]) * scale
        m = jnp.maximum(jnp.max(s_l, axis=1, keepdims=True), jnp.max(s_c, axis=1, keepdims=True))
        p_l = jnp.exp(s_l - m)
        p_c = jnp.exp(s_c - m)
        inv = 1.0 / (jnp.sum(p_l, axis=1, keepdims=True) + jnp.sum(p_c, axis=1, keepdims=True))
        probs.append((p_l * inv, p_c * inv))
    a_l = (probs[0][0] - lam * probs[1][0]).astype(BF16)
    a_c = (probs[0][1] - lam * probs[1][1]).astype(BF16)
    o = _dot(a_l, vl_ref[...]) + _dot(a_c, vc_ref[...])
    y = o * lax.rsqrt(jnp.mean(o * o, axis=-1, keepdims=True) + DA_SUBLN_EPS) * sw_ref[...]
    o_ref[...] = (y * out_scale).astype(o_ref.dtype)


def _diff_attention(qkv, lam_full, subln_w, lambda_init, n_batch, seq, lc, ctx_out):
    hw = 2 * DA_HEAD_DIM
    bq = 256
    assert lc == bq
    nq_lat = seq // bq
    t_lat = n_batch * seq
    out_rows = t_lat + (n_batch * lc if ctx_out else 0)
    ctx_blk0 = t_lat // bq

    def q_row(b, i):
        return jnp.where(i < nq_lat, b * nq_lat + i, ctx_blk0 + b)

    kern = functools.partial(_da_kernel, scale=DA_HEAD_DIM ** -0.5, nq_lat=nq_lat, out_scale=1.0 - lambda_init)
    return pl.pallas_call(
        kern,
        grid=(n_batch, DA_HEADS, nq_lat + (1 if ctx_out else 0)),
        in_specs=[
            pl.BlockSpec(memory_space=pltpu.SMEM),
            pl.BlockSpec((bq, hw), lambda b, h, i: (q_row(b, i), h)),
            pl.BlockSpec((seq, hw), lambda b, h, i: (b, DA_HEADS + h)),
            pl.BlockSpec((bq, hw), lambda b, h, i: (ctx_blk0 + b, DA_HEADS + h)),
            pl.BlockSpec((seq, hw), lambda b, h, i: (b, 2 * DA_HEADS + h)),
            pl.BlockSpec((bq, hw), lambda b, h, i: (ctx_blk0 + b, 2 * DA_HEADS + h)),
            pl.BlockSpec((1, hw), lambda b, h, i: (0, 0)),
        ],
        out_specs=pl.BlockSpec((bq, hw), lambda b, h, i: (q_row(b, i), h)),
        out_shape=jax.ShapeDtypeStruct((out_rows, DA_HEADS * hw), BF16),
        compiler_params=_params(3),
        name="diff_attention",
    )(lam_full.reshape(1).astype(F32), qkv, qkv, qkv, qkv, qkv, subln_w.reshape(1, hw))


def _conv_kernel(prev_ref, cur_ref, next_ref, w_ref, b_ref, o_ref, *, rb, lat_blocks, lat_blocks_per_seq, ctx_blocks_per_seq):
    i = pl.program_id(0)
    is_lat = i < lat_blocks
    pos = jnp.where(is_lat, i % lat_blocks_per_seq, (i - lat_blocks) % ctx_blocks_per_seq)
    per_seq = jnp.where(is_lat, lat_blocks_per_seq, ctx_blocks_per_seq)
    halo = prev_ref.shape[0]
    prev = jnp.where(pos == 0, 0.0, prev_ref[...].astype(F32))
    nxt = jnp.where(pos == per_seq - 1, 0.0, next_ref[...].astype(F32))
    cat = jnp.concatenate([prev, cur_ref[...].astype(F32), nxt], axis=0)
    n = rb + 2 * halo
    acc = b_ref[...]
    for k in range(SSM_CONV):
        shifted = cat if k == SSM_CONV // 2 else pltpu.roll(cat, (SSM_CONV // 2 - k) % n, 0)
        acc = acc + w_ref[k:k + 1, :] * shifted[halo:halo + rb]
    o_ref[...] = (acc * _sigmoid(acc)).astype(o_ref.dtype)


def _ssm_conv(proj, conv_w, conv_b, n_batch, seq, lc):
    t = proj.shape[0]
    rb, ct, halo = 256, 2048, SUBLANES_BF16
    assert seq % rb == 0 and lc % rb == 0 and SSM_D_INNER % ct == 0 and SSM_CONV_DIM % ct == 0
    col0 = SSM_D_INNER // ct
    hb = rb // halo
    n_halo_blocks = t // halo
    kern = functools.partial(_conv_kernel, rb=rb, lat_blocks=n_batch * seq // rb,
                             lat_blocks_per_seq=seq // rb, ctx_blocks_per_seq=lc // rb)
    return pl.pallas_call(
        kern,
        grid=(t // rb, SSM_CONV_DIM // ct),
        in_specs=[
            pl.BlockSpec((halo, ct), lambda i, j: (jnp.maximum(i * hb - 1, 0), col0 + j)),
            pl.BlockSpec((rb, ct), lambda i, j: (i, col0 + j)),
            pl.BlockSpec((halo, ct), lambda i, j: (jnp.minimum((i + 1) * hb, n_halo_blocks - 1), col0 + j)),
            pl.BlockSpec((SSM_CONV, ct), lambda i, j: (0, j)),
            pl.BlockSpec((1, ct), lambda i, j: (0, j)),
        ],
        out_specs=pl.BlockSpec((rb, ct), lambda i, j: (i, j)),
        out_shape=jax.ShapeDtypeStruct((t, SSM_CONV_DIM), BF16),
        compiler_params=_params(2),
        name="ssm_conv",
    )(proj, proj, proj, conv_w, conv_b.reshape(1, SSM_CONV_DIM))


def _ssd_kernel(xbc_ref, dt_ref, dtb_ref, alog_ref, tri_ref, exp_ref, y_ref, state_ref):
    q = SSM_CHUNK
    nh = SSM_HEADS
    gw = SSM_GROUP_WIDTH
    direction = pl.program_id(1)
    fwd = direction == 0

    @pl.when(pl.program_id(2) == 0)
    def _():
        state_ref[...] = jnp.zeros_like(state_ref)

    pre = dt_ref[...] + dtb_ref[...]
    dt_all = jnp.maximum(pre, 0.0) + jnp.log1p(jnp.exp(-jnp.abs(pre)))
    a_all = dt_all * (-jnp.exp(alog_ref[...]))
    tri = tri_ref[0]
    acum_all = sum(_dot(tri, piece) for piece in _split3(a_all))
    lane = lax.broadcasted_iota(jnp.int32, (q, 2 * nh), 1)
    mine = jnp.where(lane < nh, 0, 1) == direction
    first_head = lax.broadcasted_iota(jnp.int32, (q, 2 * SSM_HEAD_DIM), 1) < SSM_HEAD_DIM
    expand = exp_ref[...]
    dt_x = sum(_dot(piece, expand) for piece in _split3(jnp.where(mine, dt_all, 0.0)))
    acum_x = sum(_dot(piece, expand) for piece in _split3(jnp.where(mine, acum_all, 0.0)))
    total_x = jnp.where(fwd, acum_x[q - 1:q], acum_x[0:1])
    acum = jnp.where(fwd, acum_all, pltpu.roll(acum_all, nh, 1))
    acum_t = acum.T
    seen = tri.astype(F32) > 0.5

    for g in range(SSM_GROUPS):
        cs = slice(g * gw, (g + 1) * gw)
        b_g = xbc_ref[:, SSM_D_INNER + g * SSM_STATE:SSM_D_INNER + (g + 1) * SSM_STATE]
        c_off = SSM_D_INNER + SSM_GROUPS * SSM_STATE
        c_g = xbc_ref[:, c_off + g * SSM_STATE:c_off + (g + 1) * SSM_STATE]
        xdt = xbc_ref[:, cs].astype(F32) * dt_x[:, cs]
        cb = _dot_nt(c_g, b_g)
        h_start = state_ref[g]
        y_off = _dot(c_g, h_start.astype(BF16)) * jnp.exp(acum_x[:, cs])
        y_diag = []
        for pr in range(SSM_HEADS_PER_GROUP // 2):
            mats = []
            for r in (2 * pr, 2 * pr + 1):
                hh = g * SSM_HEADS_PER_GROUP + r
                col = jnp.sum(jnp.where(lane == hh, acum, 0.0), axis=1, keepdims=True)
                seg = col - acum_t[hh:hh + 1, :]
                mats.append((cb * jnp.where(seen, jnp.exp(seg), 0.0)).astype(BF16))
            x2 = xdt[:, pr * 2 * SSM_HEAD_DIM:(pr + 1) * 2 * SSM_HEAD_DIM]
            rhs = jnp.concatenate([jnp.where(first_head, x2, 0.0), jnp.where(first_head, 0.0, x2)], axis=0)
            y_diag.append(_dot(jnp.concatenate(mats, axis=1), rhs.astype(BF16)))
        y_ref[0, :, cs] = jnp.concatenate(y_diag, axis=1) + y_off
        to_end = jnp.exp(total_x[:, cs] - acum_x[:, cs])
        b_t = b_g.astype(F32).T.astype(BF16)
        state_ref[g] = h_start * jnp.exp(total_x[:, cs]) + _dot(b_t, (xdt * to_end).astype(BF16))


def _ssd_scan(xbc, dt_raw, dt_bias, a_log, n_batch, seq, lc):
    t = xbc.shape[0]
    q = SSM_CHUNK
    n_lat, n_ctx = seq // q, lc // q
    steps = n_lat + n_ctx
    ctx_blk0 = n_batch * n_lat

    def blk(b, d, s):
        ctx_chunk = jnp.where(d == 0, s, n_ctx - 1 - s)
        lat_chunk = jnp.where(d == 0, s - n_ctx, steps - 1 - s)
        return jnp.where(s < n_ctx, ctx_blk0 + b * n_ctx + ctx_chunk, b * n_lat + lat_chunk)

    idx = np.arange(q)
    tri = np.stack([idx[None, :] <= idx[:, None], idx[None, :] >= idx[:, None]]).astype(np.float32)
    expand = np.tile(np.repeat(np.eye(SSM_HEADS, dtype=np.float32), SSM_HEAD_DIM, axis=1), (2, 1))
    return pl.pallas_call(
        _ssd_kernel,
        grid=(n_batch, 2, steps),
        in_specs=[
            pl.BlockSpec((q, SSM_CONV_DIM), lambda b, d, s: (blk(b, d, s), 0)),
            pl.BlockSpec((q, 2 * SSM_HEADS), lambda b, d, s: (blk(b, d, s), 0)),
            pl.BlockSpec((1, 2 * SSM_HEADS), lambda b, d, s: (0, 0)),
            pl.BlockSpec((1, 2 * SSM_HEADS), lambda b, d, s: (0, 0)),
            pl.BlockSpec((1, q, q), lambda b, d, s: (d, 0, 0)),
            pl.BlockSpec((2 * SSM_HEADS, SSM_D_INNER), lambda b, d, s: (0, 0)),
        ],
        out_specs=pl.BlockSpec((1, q, SSM_D_INNER), lambda b, d, s: (d, blk(b, d, s), 0)),
        out_shape=jax.ShapeDtypeStruct((2, t, SSM_D_INNER), F32),
        scratch_shapes=[pltpu.VMEM((SSM_GROUPS, SSM_STATE, SSM_GROUP_WIDTH), F32)],
        compiler_params=_params(3),
        name="ssd_scan",
    )(xbc, dt_raw, dt_bias.reshape(1, -1), a_log.reshape(1, -1), jnp.asarray(tri, BF16), jnp.asarray(expand, BF16))


def _ssm_finish_kernel(yf_ref, yb_ref, x_ref, z_ref, dsk_ref, nw_ref, o_ref):
    gw = SSM_GROUP_WIDTH
    z = z_ref[...].astype(F32)
    y = (yf_ref[0] + yb_ref[0] + x_ref[...].astype(F32) * dsk_ref[...]) * (z * _sigmoid(z))
    for g in range(SSM_GROUPS):
        cs = slice(g * gw, (g + 1) * gw)
        yg = y[:, cs]
        o_ref[:, cs] = (yg * lax.rsqrt(jnp.mean(yg * yg, axis=-1, keepdims=True) + EPS) * nw_ref[:, cs]).astype(o_ref.dtype)


def _ssm_finish(y2, xbc, proj, d_skip, norm_w):
    t = xbc.shape[0]
    di = SSM_D_INNER
    rb = 128
    dsk = jnp.repeat(d_skip.astype(F32), SSM_HEAD_DIM).reshape(1, di)
    return pl.pallas_call(
        _ssm_finish_kernel,
        grid=(t // rb,),
        in_specs=[
            pl.BlockSpec((1, rb, di), lambda i: (0, i, 0)),
            pl.BlockSpec((1, rb, di), lambda i: (1, i, 0)),
            pl.BlockSpec((rb, di), lambda i: (i, 0)),
            pl.BlockSpec((rb, di), lambda i: (i, 0)),
            pl.BlockSpec((1, di), lambda i: (0, 0)),
            pl.BlockSpec((1, di), lambda i: (0, 0)),
        ],
        out_specs=pl.BlockSpec((rb, di), lambda i: (i, 0)),
        out_shape=jax.ShapeDtypeStruct((t, di), BF16),
        compiler_params=_params(1),
        name="ssm_finish",
    )(y2, y2, xbc, proj, dsk, norm_w.reshape(1, di))


def _mamba2(u, w_in, conv_w, conv_b, dt_bias, a_log, d_skip, norm_w, n_batch, seq, lc):
    t = u.shape[0]
    n_main = SSM_D_INNER + SSM_CONV_DIM
    proj = _mm(u, w_in[:, :n_main].astype(BF16), t)
    dt_raw = _mm(u, w_in[:, n_main:].astype(BF16), t, out_dtype=F32)
    xbc = _ssm_conv(proj, conv_w, conv_b, n_batch, seq, lc)
    y2 = _ssd_scan(xbc, dt_raw, dt_bias, a_log, n_batch, seq, lc)
    return _ssm_finish(y2, xbc, proj, d_skip, norm_w)


def kernel(x, c, ctx, c_ctx, ada_w, ada_b, norm_mix_w, norm_mlp_w, mlp_w1, mlp_w2, na_w_qkv, na_w_o, na_rpb,
           ssm_w_in, ssm_conv_w, ssm_conv_b, ssm_dt_bias, ssm_a_log, ssm_d, ssm_norm_w, ssm_w_out,
           da_w_qkv, da_w_o, da_lambda, da_subln_w, final_norm_w):
    n_batch, seq, d = x.shape
    lc = ctx.shape[1]
    t_lat, t_ctx = n_batch * seq, n_batch * lc
    t = t_lat + t_ctx
    geom = dict(t_lat=t_lat, seq=seq, n_batch=n_batch)
    assert d == D_MODEL and seq % GRID_W == 0 and n_batch < 8

    cond = jnp.concatenate([c, c_ctx[None, :], jnp.zeros((8 - n_batch - 1, d), F32)], axis=0)
    mod = _adaln_all(cond, ada_w, ada_b)
    h = jnp.concatenate([x.reshape(t_lat, d), ctx.reshape(t_ctx, d)], axis=0)

    for i in range(DEPTH):
        last = i == DEPTH - 1
        mixer, j = i % N_MIXERS, i // N_MIXERS
        mod_l = mod[i]
        rows_out = t_lat if last else t
        u = _norm_mod(h, norm_mix_w[i], mod_l, 0, t, geom)
        if mixer == 0:
            qkv = _mm(u, na_w_qkv[j].astype(BF16), t)
            bias_tab = _na_bias_table(na_rpb[j], seq // GRID_W)
            y = _na_attention(qkv, bias_tab, n_batch, seq, lc, not last)
            w_o = na_w_o[j]
        elif mixer == 1:
            y = _mamba2(u, ssm_w_in[j], ssm_conv_w[j], ssm_conv_b[j], ssm_dt_bias[j], ssm_a_log[j],
                        ssm_d[j], ssm_norm_w[j], n_batch, seq, lc)
            w_o = ssm_w_out[j]
        else:
            lambda_init = 0.8 - 0.6 * math.exp(-0.3 * i)
            qk_w = 2 * DA_HEADS * DA_HEAD_DIM
            cos_t, sin_t = _rope_tables(seq, 1024)
            qkv = _mm_rope(u, da_w_qkv[j].astype(BF16), cos_t, sin_t, t_lat, 2 * qk_w)
            lam = da_lambda[j].astype(F32)
            lam_full = jnp.exp(jnp.sum(lam[0] * lam[1])) - jnp.exp(jnp.sum(lam[2] * lam[3])) + lambda_init
            y = _diff_attention(qkv, lam_full, da_subln_w[j], lambda_init, n_batch, seq, lc, not last)
            w_o = da_w_o[j]
        h = _mm_gated(y, w_o.astype(BF16), h, mod_l, 2, rows_out, geom)
        u = _norm_mod(h, norm_mlp_w[i], mod_l, 3, rows_out, geom)
        hidden = _mm(u, mlp_w1[i].astype(BF16), rows_out, act="relu2")
        h = _mm_gated(hidden, mlp_w2[i].astype(BF16), h, mod_l, 5, rows_out, geom)

    return _final_norm(h, final_norm_w, t_lat).reshape(n_batch, seq, d)
```

```python
import functools
import math

import jax
import jax.numpy as jnp
import numpy as np
from jax import lax
from jax.experimental import pallas as pl
from jax.experimental.pallas import tpu as pltpu

F32 = jnp.float32
BF16 = jnp.bfloat16

D_MODEL = 2048
DEPTH = 4
GRID_W = 64
N_MIXERS = 3
EPS = 1e-6

NA_HEADS = 16
NA_HEAD_DIM = 128
NA_WIN_ROWS = 8
NA_WIN_COLS = 16
NA_GROUP_ROWS = 4
NA_KEY_ROWS = 3 * NA_GROUP_ROWS
NA_HEADS_PER_STEP = 4

SSM_D_INNER = 4096
SSM_HEAD_DIM = 64
SSM_HEADS = 64
SSM_GROUPS = 8
SSM_HEADS_PER_GROUP = 8
SSM_STATE = 128
SSM_CONV = 5
SSM_CHUNK = 128
SSM_CONV_DIM = SSM_D_INNER + 2 * SSM_GROUPS * SSM_STATE
SSM_GROUP_WIDTH = SSM_HEADS_PER_GROUP * SSM_HEAD_DIM

DA_HEADS = 8
DA_HEAD_DIM = 128
DA_SUBLN_EPS = 1e-5
ROPE_BASE = 10000.0

MASK_NEG = -1e30

V7X_VMEM_BYTES = 64 * 1024 * 1024
VMEM_LIMIT = (V7X_VMEM_BYTES * 3) // 4
SUBLANES_BF16 = 16


def _params(n_grid_dims):
    return pltpu.CompilerParams(
        dimension_semantics=("arbitrary",) * n_grid_dims, vmem_limit_bytes=VMEM_LIMIT)


def _sigmoid(x):
    return 1.0 / (1.0 + jnp.exp(-x))


def _dot(a, b):
    return jnp.dot(a, b, preferred_element_type=F32)


def _dot_nt(a, b):
    return lax.dot_general(a, b, (((1,), (1,)), ((), ())), preferred_element_type=F32)


def _split3(v):
    hi = v.astype(BF16)
    r1 = v - hi.astype(F32)
    mid = r1.astype(BF16)
    lo = (r1 - mid.astype(F32)).astype(BF16)
    return hi, mid, lo


def _mod_row(row0, t_lat, seq, n_batch):
    return jnp.where(row0 < t_lat, row0 // seq, n_batch)


def _adaln_kernel(c_ref, w_ref, b_ref, o_ref):
    x = c_ref[...]
    x = (x * _sigmoid(x)).astype(BF16)
    o_ref[0] = _dot(x, w_ref[0].astype(BF16)) + b_ref[0]


def _adaln_all(cond, ada_w, ada_b):
    depth, d, n = ada_w.shape
    bn = 1024
    return pl.pallas_call(
        _adaln_kernel,
        grid=(depth, n // bn),
        in_specs=[
            pl.BlockSpec((8, d), lambda l, j: (0, 0)),
            pl.BlockSpec((1, d, bn), lambda l, j: (l, 0, j)),
            pl.BlockSpec((1, 1, bn), lambda l, j: (l, 0, j)),
        ],
        out_specs=pl.BlockSpec((1, 8, bn), lambda l, j: (l, 0, j)),
        out_shape=jax.ShapeDtypeStruct((depth, 8, n), F32),
        compiler_params=_params(2),
        name="adaln",
    )(cond, ada_w, ada_b.reshape(depth, 1, n))


def _norm_mod_kernel(h_ref, nw_ref, mod_ref, o_ref, *, bm, shift_k, t_lat, seq, n_batch):
    d = h_ref.shape[1]
    midx = _mod_row(pl.program_id(0) * bm, t_lat, seq, n_batch)
    x = h_ref[...]
    y = x * lax.rsqrt(jnp.mean(x * x, axis=-1, keepdims=True) + EPS) * nw_ref[...]
    shift = mod_ref[pl.ds(midx, 1), shift_k * d:(shift_k + 1) * d]
    scale = mod_ref[pl.ds(midx, 1), (shift_k + 1) * d:(shift_k + 2) * d]
    o_ref[...] = (y * (1.0 + scale) + shift).astype(o_ref.dtype)


def _norm_mod(h, norm_w, mod_l, shift_k, m_rows, geom):
    t, d = h.shape
    bm = 256
    kern = functools.partial(_norm_mod_kernel, bm=bm, shift_k=shift_k, **geom)
    return pl.pallas_call(
        kern,
        grid=(m_rows // bm,),
        in_specs=[
            pl.BlockSpec((bm, d), lambda i: (i, 0)),
            pl.BlockSpec((1, d), lambda i: (0, 0)),
            pl.BlockSpec(mod_l.shape, lambda i: (0, 0)),
        ],
        out_specs=pl.BlockSpec((bm, d), lambda i: (i, 0)),
        out_shape=jax.ShapeDtypeStruct((m_rows, d), BF16),
        compiler_params=_params(1),
        name="norm_mod",
    )(h, norm_w.reshape(1, d), mod_l)


def _final_norm_kernel(h_ref, nw_ref, o_ref):
    x = h_ref[...]
    o_ref[...] = x * lax.rsqrt(jnp.mean(x * x, axis=-1, keepdims=True) + EPS) * nw_ref[...]


def _final_norm(h, norm_w, m_rows):
    d = h.shape[1]
    bm = 256
    return pl.pallas_call(
        _final_norm_kernel,
        grid=(m_rows // bm,),
        in_specs=[pl.BlockSpec((bm, d), lambda i: (i, 0)), pl.BlockSpec((1, d), lambda i: (0, 0))],
        out_specs=pl.BlockSpec((bm, d), lambda i: (i, 0)),
        out_shape=jax.ShapeDtypeStruct((m_rows, d), F32),
        compiler_params=_params(1),
        name="final_norm",
    )(h, norm_w.reshape(1, d))


def _mm_blocks(m_rows, k, n):
    bm = 1024 if m_rows % 1024 == 0 else 256
    bn = 1024 if n % 1024 == 0 else (512 if n % 512 == 0 else 128)
    if k >= 8192:
        bm, bn = min(bm, 512), min(bn, 512)
    elif k >= 4096:
        bn = min(bn, 512)
    return bm, bn


def _mm_kernel(x_ref, w_ref, o_ref, *, act):
    acc = _dot(x_ref[...], w_ref[...])
    if act == "relu2":
        acc = jnp.maximum(acc, 0.0)
        acc = acc * acc
    o_ref[...] = acc.astype(o_ref.dtype)


def _mm(x, w, m_rows, out_dtype=BF16, act=None):
    k, n = w.shape
    bm, bn = _mm_blocks(m_rows, k, n)
    return pl.pallas_call(
        functools.partial(_mm_kernel, act=act),
        grid=(m_rows // bm, n // bn),
        in_specs=[pl.BlockSpec((bm, k), lambda i, j: (i, 0)), pl.BlockSpec((k, bn), lambda i, j: (0, j))],
        out_specs=pl.BlockSpec((bm, bn), lambda i, j: (i, j)),
        out_shape=jax.ShapeDtypeStruct((m_rows, n), out_dtype),
        compiler_params=_params(2),
        name="mm_" + (act or "plain"),
    )(x, w)


def _mm_gated_kernel(x_ref, w_ref, res_ref, gate_ref, o_ref, *, bm, t_lat, seq, n_batch):
    midx = _mod_row(pl.program_id(0) * bm, t_lat, seq, n_batch)
    acc = _dot(x_ref[...], w_ref[...])
    o_ref[...] = res_ref[...] + gate_ref[pl.ds(midx, 1), :] * acc


def _mm_gated(x, w, res, mod_l, gate_k, m_rows, geom):
    k, n = w.shape
    bm, bn = _mm_blocks(m_rows, k, n)
    gate_blk0 = gate_k * (n // bn)
    return pl.pallas_call(
        functools.partial(_mm_gated_kernel, bm=bm, **geom),
        grid=(m_rows // bm, n // bn),
        in_specs=[
            pl.BlockSpec((bm, k), lambda i, j: (i, 0)),
            pl.BlockSpec((k, bn), lambda i, j: (0, j)),
            pl.BlockSpec((bm, bn), lambda i, j: (i, j)),
            pl.BlockSpec((8, bn), lambda i, j: (0, gate_blk0 + j)),
        ],
        out_specs=pl.BlockSpec((bm, bn), lambda i, j: (i, j)),
        out_shape=jax.ShapeDtypeStruct(res.shape, F32),
        input_output_aliases={2: 0},
        compiler_params=_params(2),
        name="mm_gated",
    )(x, w, res, mod_l)


def _mm_rope_kernel(x_ref, w_ref, cos_ref, sin_ref, o_ref, *, rope_row_blocks, rope_col_blocks):
    acc = _dot(x_ref[...], w_ref[...])
    do_rope = (pl.program_id(0) < rope_row_blocks) & (pl.program_id(1) < rope_col_blocks)

    @pl.when(do_rope)
    def _():
        n = acc.shape[1]
        lane = lax.broadcasted_iota(jnp.int32, acc.shape, 1)
        partner = jnp.where((lane % 64) < 32, pltpu.roll(acc, n - 32, 1), pltpu.roll(acc, 32, 1))
        o_ref[...] = (acc * cos_ref[...] + partner * sin_ref[...]).astype(o_ref.dtype)

    @pl.when(jnp.logical_not(do_rope))
    def _():
        o_ref[...] = acc.astype(o_ref.dtype)


def _mm_rope(x, w, cos_t, sin_t, t_lat, rope_cols):
    m_rows = x.shape[0]
    k, n = w.shape
    bm, bn = _mm_blocks(m_rows, k, n)
    seq_blocks = cos_t.shape[0] // bm
    kern = functools.partial(_mm_rope_kernel, rope_row_blocks=t_lat // bm, rope_col_blocks=rope_cols // bn)
    return pl.pallas_call(
        kern,
        grid=(m_rows // bm, n // bn),
        in_specs=[
            pl.BlockSpec((bm, k), lambda i, j: (i, 0)),
            pl.BlockSpec((k, bn), lambda i, j: (0, j)),
            pl.BlockSpec((bm, bn), lambda i, j: (i % seq_blocks, 0)),
            pl.BlockSpec((bm, bn), lambda i, j: (i % seq_blocks, 0)),
        ],
        out_specs=pl.BlockSpec((bm, bn), lambda i, j: (i, j)),
        out_shape=jax.ShapeDtypeStruct((m_rows, n), BF16),
        compiler_params=_params(2),
        name="mm_rope",
    )(x, w, cos_t, sin_t)


def _na_bias_table(rpb, rows):
    heads = rpb.shape[0]
    gq, gk = NA_GROUP_ROWS, NA_KEY_ROWS
    col = np.arange(GRID_W)
    col_start = np.clip(col - NA_WIN_COLS // 2, 0, GRID_W - NA_WIN_COLS)
    col_mask = (col[None, :] >= col_start[:, None]) & (col[None, :] < col_start[:, None] + NA_WIN_COLS)
    dc_idx = np.clip(col[None, :] - col[:, None], -(NA_WIN_COLS - 1), NA_WIN_COLS - 1) + NA_WIN_COLS - 1
    onehot = (dc_idx[None] == np.arange(2 * NA_WIN_COLS - 1)[:, None, None]).astype(np.float32)
    per_dr = jnp.einsum("hrd,dqk->hrqk", rpb.astype(F32), onehot, precision=lax.Precision.HIGHEST)
    per_dr = jnp.where(col_mask[None, None], per_dr, MASK_NEG)
    masked = jnp.full((heads, GRID_W, GRID_W), MASK_NEG, F32)
    tabs = []
    for r0, ks in ((0, 0), (gq, 0), (rows - gq, rows - gk)):
        q_rows = []
        for r in range(r0, r0 + gq):
            rs = min(max(r - NA_WIN_ROWS // 2, 0), rows - NA_WIN_ROWS)
            blocks = [per_dr[:, kr - r + NA_WIN_ROWS - 1] if rs <= kr < rs + NA_WIN_ROWS else masked
                      for kr in range(ks, ks + gk)]
            q_rows.append(jnp.stack(blocks, axis=2))
        tabs.append(jnp.stack(q_rows, axis=1).reshape(heads, gq * GRID_W, gk * GRID_W))
    tabs.append(jnp.full_like(tabs[0], MASK_NEG))
    return jnp.stack(tabs)


def _na_kernel(q_ref, k0_ref, k1_ref, k2_ref, v0_ref, v1_ref, v2_ref, kc_ref, vc_ref, bias_ref, o_ref, *, scale):
    dh = NA_HEAD_DIM
    kb = q_ref.shape[0]
    for hh in range(NA_HEADS_PER_STEP):
        sl = slice(hh * dh, (hh + 1) * dh)
        q = q_ref[:, sl]
        s_loc = jnp.concatenate([_dot_nt(q, kr[:, sl]) for kr in (k0_ref, k1_ref, k2_ref)], axis=1)
        s_loc = s_loc * scale + bias_ref[0, hh]
        s_ctx = _dot_nt(q, kc_ref[:, sl]) * scale
        m = jnp.maximum(jnp.max(s_loc, axis=1, keepdims=True), jnp.max(s_ctx, axis=1, keepdims=True))
        p_loc = jnp.exp(s_loc - m)
        p_ctx = jnp.exp(s_ctx - m)
        denom = jnp.sum(p_loc, axis=1, keepdims=True) + jnp.sum(p_ctx, axis=1, keepdims=True)
        p_loc = p_loc.astype(BF16)
        o = _dot(p_ctx.astype(BF16), vc_ref[:, sl])
        for j, vr in enumerate((v0_ref, v1_ref, v2_ref)):
            o = o + _dot(p_loc[:, j * kb:(j + 1) * kb], vr[:, sl])
        o_ref[:, sl] = (o * (1.0 / denom)).astype(o_ref.dtype)


def _na_attention(qkv, bias_tab, n_batch, seq, lc, ctx_out):
    hd = NA_HEADS * NA_HEAD_DIM
    gtok = NA_GROUP_ROWS * GRID_W
    ng = seq // gtok
    assert lc == gtok and ng >= 4
    hw = NA_HEADS_PER_STEP * NA_HEAD_DIM
    nhq = NA_HEADS // NA_HEADS_PER_STEP
    t_lat = n_batch * seq
    out_rows = t_lat + (n_batch * lc if ctx_out else 0)

    def q_row(g, b):
        return jnp.where(g < ng, b * ng + g, n_batch * ng + b)

    def k_row(g, b, j):
        return b * ng + jnp.clip(g - 1, 0, ng - 3) + j

    def bias_type(g):
        return jnp.where(g == 0, 0, jnp.where(g < ng - 1, 1, jnp.where(g == ng - 1, 2, 3)))

    blk = (gtok, hw)
    in_specs = [pl.BlockSpec(blk, lambda h, g, b: (q_row(g, b), h))]
    for col0 in (nhq, 2 * nhq):
        for j in range(3):
            in_specs.append(pl.BlockSpec(blk, lambda h, g, b, j=j, col0=col0: (k_row(g, b, j), col0 + h)))
    in_specs.append(pl.BlockSpec(blk, lambda h, g, b: (n_batch * ng + b, nhq + h)))
    in_specs.append(pl.BlockSpec(blk, lambda h, g, b: (n_batch * ng + b, 2 * nhq + h)))
    in_specs.append(pl.BlockSpec((1, NA_HEADS_PER_STEP, gtok, 3 * gtok), lambda h, g, b: (bias_type(g), h, 0, 0)))
    return pl.pallas_call(
        functools.partial(_na_kernel, scale=NA_HEAD_DIM ** -0.5),
        grid=(nhq, ng + (1 if ctx_out else 0), n_batch),
        in_specs=in_specs,
        out_specs=pl.BlockSpec(blk, lambda h, g, b: (q_row(g, b), h)),
        out_shape=jax.ShapeDtypeStruct((out_rows, hd), BF16),
        compiler_params=_params(3),
        name="na_attention",
    )(*([qkv] * 9), bias_tab)


def _rope_tables(seq, width):
    t = jnp.arange(seq)
    pos = jnp.stack([t // GRID_W, t % GRID_W], axis=-1).astype(F32)
    n_freq = DA_HEAD_DIM // 4
    inv_freq = ROPE_BASE ** (-jnp.arange(n_freq, dtype=F32) / n_freq)
    ang = pos[:, :, None] * inv_freq
    cos = jnp.broadcast_to(jnp.cos(ang)[:, :, None, :], (seq, 2, 2, n_freq)).reshape(seq, DA_HEAD_DIM)
    sign = jnp.array([-1.0, 1.0], F32)[None, None, :, None]
    sin = (jnp.broadcast_to(jnp.sin(ang)[:, :, None, :], (seq, 2, 2, n_freq)) * sign).reshape(seq, DA_HEAD_DIM)
    reps = width // DA_HEAD_DIM
    return jnp.tile(cos, (1, reps)), jnp.tile(sin, (1, reps))


def _da_kernel(lam_ref, q_ref, kl_ref, kc_ref, vl_ref, vc_ref, sw_ref, o_ref, *, scale, nq_lat, out_scale):
    dh = DA_HEAD_DIM
    lam = lam_ref[0]
    lat_mask = jnp.where(pl.program_id(2) >= nq_lat, MASK_NEG, 0.0)
    probs = []
    for comp in range(2):
        sl = slice(comp * dh, (comp + 1) * dh)
        q = q_ref[:, sl]
        s_l = _dot_nt(q, kl_ref[:, sl]) * scale + lat_mask
        s_c = _dot_nt(q, kc_ref[:, sl]) * scale
        m = jnp.maximum(jnp.max(s_l, axis=1, keepdims=True), jnp.max(s_c, axis=1, keepdims=True))
        p_l = jnp.exp(s_l - m)
        p_c = jnp.exp(s_c - m)
        inv = 1.0 / (jnp.sum(p_l, axis=1, keepdims=True) + jnp.sum(p_c, axis=1, keepdims=True))
        probs.append((p_l * inv, p_c * inv))
    a_l = (probs[0][0] - lam * probs[1][0]).astype(BF16)
    a_c = (probs[0][1] - lam * probs[1][1]).astype(BF16)
    o = _dot(a_l, vl_ref[...]) + _dot(a_c, vc_ref[...])
    y = o * lax.rsqrt(jnp.mean(o * o, axis=-1, keepdims=True) + DA_SUBLN_EPS) * sw_ref[...]
    o_ref[...] = (y * out_scale).astype(o_ref.dtype)


def _diff_attention(qkv, lam_full, subln_w, lambda_init, n_batch, seq, lc, ctx_out):
    hw = 2 * DA_HEAD_DIM
    bq = 256
    assert lc == bq
    nq_lat = seq // bq
    t_lat = n_batch * seq
    out_rows = t_lat + (n_batch * lc if ctx_out else 0)
    ctx_blk0 = t_lat // bq

    def q_row(b, i):
        return jnp.where(i < nq_lat, b * nq_lat + i, ctx_blk0 + b)

    kern = functools.partial(_da_kernel, scale=DA_HEAD_DIM ** -0.5, nq_lat=nq_lat, out_scale=1.0 - lambda_init)
    return pl.pallas_call(
        kern,
        grid=(n_batch, DA_HEADS, nq_lat + (1 if ctx_out else 0)),
        in_specs=[
            pl.BlockSpec(memory_space=pltpu.SMEM),
            pl.BlockSpec((bq, hw), lambda b, h, i: (q_row(b, i), h)),
            pl.BlockSpec((seq, hw), lambda b, h, i: (b, DA_HEADS + h)),
            pl.BlockSpec((bq, hw), lambda b, h, i: (ctx_blk0 + b, DA_HEADS + h)),
            pl.BlockSpec((seq, hw), lambda b, h, i: (b, 2 * DA_HEADS + h)),
            pl.BlockSpec((bq, hw), lambda b, h, i: (ctx_blk0 + b, 2 * DA_HEADS + h)),
            pl.BlockSpec((1, hw), lambda b, h, i: (0, 0)),
        ],
        out_specs=pl.BlockSpec((bq, hw), lambda b, h, i: (q_row(b, i), h)),
        out_shape=jax.ShapeDtypeStruct((out_rows, DA_HEADS * hw), BF16),
        compiler_params=_params(3),
        name="diff_attention",
    )(lam_full.reshape(1).astype(F32), qkv, qkv, qkv, qkv, qkv, subln_w.reshape(1, hw))


def _conv_kernel(prev_ref, cur_ref, next_ref, w_ref, b_ref, o_ref, *, rb, lat_blocks, lat_blocks_per_seq, ctx_blocks_per_seq):
    i = pl.program_id(0)
    is_lat = i < lat_blocks
    pos = jnp.where(is_lat, i % lat_blocks_per_seq, (i - lat_blocks) % ctx_blocks_per_seq)
    per_seq = jnp.where(is_lat, lat_blocks_per_seq, ctx_blocks_per_seq)
    halo = prev_ref.shape[0]
    prev = jnp.where(pos == 0, 0.0, prev_ref[...].astype(F32))
    nxt = jnp.where(pos == per_seq - 1, 0.0, next_ref[...].astype(F32))
    cat = jnp.concatenate([prev, cur_ref[...].astype(F32), nxt], axis=0)
    n = rb + 2 * halo
    acc = b_ref[...]
    for k in range(SSM_CONV):
        shifted = cat if k == SSM_CONV // 2 else pltpu.roll(cat, (SSM_CONV // 2 - k) % n, 0)
        acc = acc + w_ref[k:k + 1, :] * shifted[halo:halo + rb]
    o_ref[...] = (acc * _sigmoid(acc)).astype(o_ref.dtype)


def _ssm_conv(proj, conv_w, conv_b, n_batch, seq, lc):
    t = proj.shape[0]
    rb, ct, halo = 256, 2048, SUBLANES_BF16
    assert seq % rb == 0 and lc % rb == 0 and SSM_D_INNER % ct == 0 and SSM_CONV_DIM % ct == 0
    col0 = SSM_D_INNER // ct
    hb = rb // halo
    n_halo_blocks = t // halo
    kern = functools.partial(_conv_kernel, rb=rb, lat_blocks=n_batch * seq // rb,
                             lat_blocks_per_seq=seq // rb, ctx_blocks_per_seq=lc // rb)
    return pl.pallas_call(
        kern,
        grid=(t // rb, SSM_CONV_DIM // ct),
        in_specs=[
            pl.BlockSpec((halo, ct), lambda i, j: (jnp.maximum(i * hb - 1, 0), col0 + j)),
            pl.BlockSpec((rb, ct), lambda i, j: (i, col0 + j)),
            pl.BlockSpec((halo, ct), lambda i, j: (jnp.minimum((i + 1) * hb, n_halo_blocks - 1), col0 + j)),
            pl.BlockSpec((SSM_CONV, ct), lambda i, j: (0, j)),
            pl.BlockSpec((1, ct), lambda i, j: (0, j)),
        ],
        out_specs=pl.BlockSpec((rb, ct), lambda i, j: (i, j)),
        out_shape=jax.ShapeDtypeStruct((t, SSM_CONV_DIM), BF16),
        compiler_params=_params(2),
        name="ssm_conv",
    )(proj, proj, proj, conv_w, conv_b.reshape(1, SSM_CONV_DIM))


def _ssd_kernel(xbc_ref, dt_ref, dtb_ref, alog_ref, tri_ref, exp_ref, y_ref, state_ref):
    q = SSM_CHUNK
    nh = SSM_HEADS
    gw = SSM_GROUP_WIDTH
    direction = pl.program_id(1)
    fwd = direction == 0

    @pl.when(pl.program_id(2) == 0)
    def _():
        state_ref[...] = jnp.zeros_like(state_ref)

    pre = dt_ref[...] + dtb_ref[...]
    dt_all = jnp.maximum(pre, 0.0) + jnp.log1p(jnp.exp(-jnp.abs(pre)))
    a_all = dt_all * (-jnp.exp(alog_ref[...]))
    tri = tri_ref[0]
    acum_all = sum(_dot(tri, piece) for piece in _split3(a_all))
    lane = lax.broadcasted_iota(jnp.int32, (q, 2 * nh), 1)
    mine = jnp.where(lane < nh, 0, 1) == direction
    first_head = lax.broadcasted_iota(jnp.int32, (q, 2 * SSM_HEAD_DIM), 1) < SSM_HEAD_DIM
    expand = exp_ref[...]
    dt_x = sum(_dot(piece, expand) for piece in _split3(jnp.where(mine, dt_all, 0.0)))
    acum_x = sum(_dot(piece, expand) for piece in _split3(jnp.where(mine, acum_all, 0.0)))
    total_x = jnp.where(fwd, acum_x[q - 1:q], acum_x[0:1])
    acum = jnp.where(fwd, acum_all, pltpu.roll(acum_all, nh, 1))
    acum_t = acum.T
    seen = tri.astype(F32) > 0.5

    for g in range(SSM_GROUPS):
        cs = slice(g * gw, (g + 1) * gw)
        b_g = xbc_ref[:, SSM_D_INNER + g * SSM_STATE:SSM_D_INNER + (g + 1) * SSM_STATE]
        c_off = SSM_D_INNER + SSM_GROUPS * SSM_STATE
        c_g = xbc_ref[:, c_off + g * SSM_STATE:c_off + (g + 1) * SSM_STATE]
        xdt = xbc_ref[:, cs].astype(F32) * dt_x[:, cs]
        cb = _dot_nt(c_g, b_g)
        h_start = state_ref[g]
        y_off = _dot(c_g, h_start.astype(BF16)) * jnp.exp(acum_x[:, cs])
        y_diag = []
        for pr in range(SSM_HEADS_PER_GROUP // 2):
            mats = []
            for r in (2 * pr, 2 * pr + 1):
                hh = g * SSM_HEADS_PER_GROUP + r
                col = jnp.sum(jnp.where(lane == hh, acum, 0.0), axis=1, keepdims=True)
                seg = col - acum_t[hh:hh + 1, :]
                mats.append((cb * jnp.where(seen, jnp.exp(seg), 0.0)).astype(BF16))
            x2 = xdt[:, pr * 2 * SSM_HEAD_DIM:(pr + 1) * 2 * SSM_HEAD_DIM]
            rhs = jnp.concatenate([jnp.where(first_head, x2, 0.0), jnp.where(first_head, 0.0, x2)], axis=0)
            y_diag.append(_dot(jnp.concatenate(mats, axis=1), rhs.astype(BF16)))
        y_ref[0, :, cs] = jnp.concatenate(y_diag, axis=1) + y_off
        to_end = jnp.exp(total_x[:, cs] - acum_x[:, cs])
        b_t = b_g.astype(F32).T.astype(BF16)
        state_ref[g] = h_start * jnp.exp(total_x[:, cs]) + _dot(b_t, (xdt * to_end).astype(BF16))


def _ssd_scan(xbc, dt_raw, dt_bias, a_log, n_batch, seq, lc):
    t = xbc.shape[0]
    q = SSM_CHUNK
    n_lat, n_ctx = seq // q, lc // q
    steps = n_lat + n_ctx
    ctx_blk0 = n_batch * n_lat

    def blk(b, d, s):
        ctx_chunk = jnp.where(d == 0, s, n_ctx - 1 - s)
        lat_chunk = jnp.where(d == 0, s - n_ctx, steps - 1 - s)
        return jnp.where(s < n_ctx, ctx_blk0 + b * n_ctx + ctx_chunk, b * n_lat + lat_chunk)

    idx = np.arange(q)
    tri = np.stack([idx[None, :] <= idx[:, None], idx[None, :] >= idx[:, None]]).astype(np.float32)
    expand = np.tile(np.repeat(np.eye(SSM_HEADS, dtype=np.float32), SSM_HEAD_DIM, axis=1), (2, 1))
    return pl.pallas_call(
        _ssd_kernel,
        grid=(n_batch, 2, steps),
        in_specs=[
            pl.BlockSpec((q, SSM_CONV_DIM), lambda b, d, s: (blk(b, d, s), 0)),
            pl.BlockSpec((q, 2 * SSM_HEADS), lambda b, d, s: (blk(b, d, s), 0)),
            pl.BlockSpec((1, 2 * SSM_HEADS), lambda b, d, s: (0, 0)),
            pl.BlockSpec((1, 2 * SSM_HEADS), lambda b, d, s: (0, 0)),
            pl.BlockSpec((1, q, q), lambda b, d, s: (d, 0, 0)),
            pl.BlockSpec((2 * SSM_HEADS, SSM_D_INNER), lambda b, d, s: (0, 0)),
        ],
        out_specs=pl.BlockSpec((1, q, SSM_D_INNER), lambda b, d, s: (d, blk(b, d, s), 0)),
        out_shape=jax.ShapeDtypeStruct((2, t, SSM_D_INNER), F32),
        scratch_shapes=[pltpu.VMEM((SSM_GROUPS, SSM_STATE, SSM_GROUP_WIDTH), F32)],
        compiler_params=_params(3),
        name="ssd_scan",
    )(xbc, dt_raw, dt_bias.reshape(1, -1), a_log.reshape(1, -1), jnp.asarray(tri, BF16), jnp.asarray(expand, BF16))


def _ssm_finish_kernel(yf_ref, yb_ref, x_ref, z_ref, dsk_ref, nw_ref, o_ref):
    gw = SSM_GROUP_WIDTH
    z = z_ref[...].astype(F32)
    y = (yf_ref[0] + yb_ref[0] + x_ref[...].astype(F32) * dsk_ref[...]) * (z * _sigmoid(z))
    for g in range(SSM_GROUPS):
        cs = slice(g * gw, (g + 1) * gw)
        yg = y[:, cs]
        o_ref[:, cs] = (yg * lax.rsqrt(jnp.mean(yg * yg, axis=-1, keepdims=True) + EPS) * nw_ref[:, cs]).astype(o_ref.dtype)


def _ssm_finish(y2, xbc, proj, d_skip, norm_w):
    t = xbc.shape[0]
    di = SSM_D_INNER
    rb = 128
    dsk = jnp.repeat(d_skip.astype(F32), SSM_HEAD_DIM).reshape(1, di)
    return pl.pallas_call(
        _ssm_finish_kernel,
        grid=(t // rb,),
        in_specs=[
            pl.BlockSpec((1, rb, di), lambda i: (0, i, 0)),
            pl.BlockSpec((1, rb, di), lambda i: (1, i, 0)),
            pl.BlockSpec((rb, di), lambda i: (i, 0)),
            pl.BlockSpec((rb, di), lambda i: (i, 0)),
            pl.BlockSpec((1, di), lambda i: (0, 0)),
            pl.BlockSpec((1, di), lambda i: (0, 0)),
        ],
        out_specs=pl.BlockSpec((rb, di), lambda i: (i, 0)),
        out_shape=jax.ShapeDtypeStruct((t, di), BF16),
        compiler_params=_params(1),
        name="ssm_finish",
    )(y2, y2, xbc, proj, dsk, norm_w.reshape(1, di))


def _mamba2(u, w_in, conv_w, conv_b, dt_bias, a_log, d_skip, norm_w, n_batch, seq, lc):
    t = u.shape[0]
    n_main = SSM_D_INNER + SSM_CONV_DIM
    proj = _mm(u, w_in[:, :n_main].astype(BF16), t)
    dt_raw = _mm(u, w_in[:, n_main:].astype(BF16), t, out_dtype=F32)
    xbc = _ssm_conv(proj, conv_w, conv_b, n_batch, seq, lc)
    y2 = _ssd_scan(xbc, dt_raw, dt_bias, a_log, n_batch, seq, lc)
    return _ssm_finish(y2, xbc, proj, d_skip, norm_w)


def kernel(x, c, ctx, c_ctx, ada_w, ada_b, norm_mix_w, norm_mlp_w, mlp_w1, mlp_w2, na_w_qkv, na_w_o, na_rpb,
           ssm_w_in, ssm_conv_w, ssm_conv_b, ssm_dt_bias, ssm_a_log, ssm_d, ssm_norm_w, ssm_w_out,
           da_w_qkv, da_w_o, da_lambda, da_subln_w, final_norm_w):
    n_batch, seq, d = x.shape
    lc = ctx.shape[1]
    t_lat, t_ctx = n_batch * seq, n_batch * lc
    t = t_lat + t_ctx
    geom = dict(t_lat=t_lat, seq=seq, n_batch=n_batch)
    assert d == D_MODEL and seq % GRID_W == 0 and n_batch < 8

    cond = jnp.concatenate([c, c_ctx[None, :], jnp.zeros((8 - n_batch - 1, d), F32)], axis=0)
    mod = _adaln_all(cond, ada_w, ada_b)
    h = jnp.concatenate([x.reshape(t_lat, d), ctx.reshape(t_ctx, d)], axis=0)

    for i in range(DEPTH):
        last = i == DEPTH - 1
        mixer, j = i % N_MIXERS, i // N_MIXERS
        mod_l = mod[i]
        rows_out = t_lat if last else t
        u = _norm_mod(h, norm_mix_w[i], mod_l, 0, t, geom)
        if mixer == 0:
            qkv = _mm(u, na_w_qkv[j].astype(BF16), t)
            bias_tab = _na_bias_table(na_rpb[j], seq // GRID_W)
            y = _na_attention(qkv, bias_tab, n_batch, seq, lc, not last)
            w_o = na_w_o[j]
        elif mixer == 1:
            y = _mamba2(u, ssm_w_in[j], ssm_conv_w[j], ssm_conv_b[j], ssm_dt_bias[j], ssm_a_log[j],
                        ssm_d[j], ssm_norm_w[j], n_batch, seq, lc)
            w_o = ssm_w_out[j]
        else:
            lambda_init = 0.8 - 0.6 * math.exp(-0.3 * i)
            qk_w = 2 * DA_HEADS * DA_HEAD_DIM
            cos_t, sin_t = _rope_tables(seq, 1024)
            qkv = _mm_rope(u, da_w_qkv[j].astype(BF16), cos_t, sin_t, t_lat, 2 * qk_w)
            lam = da_lambda[j].astype(F32)
            lam_full = jnp.exp(jnp.sum(lam[0] * lam[1])) - jnp.exp(jnp.sum(lam[2] * lam[3])) + lambda_init
            y = _diff_attention(qkv, lam_full, da_subln_w[j], lambda_init, n_batch, seq, lc, not last)
            w_o = da_w_o[j]
        h = _mm_gated(y, w_o.astype(BF16), h, mod_l, 2, rows_out, geom)
        u = _norm_mod(h, norm_mlp_w[i], mod_l, 3, rows_out, geom)
        hidden = _mm(u, mlp_w1[i].astype(BF16), rows_out, act="relu2")
        h = _mm_gated(hidden, mlp_w2[i].astype(BF16), h, mod_l, 5, rows_out, geom)

    return _final_norm(h, final_norm_w, t_lat).reshape(n_batch, seq, d)
```

```python
import functools
import math

import jax
import jax.numpy as jnp
import numpy as np
from jax import lax
from jax.experimental import pallas as pl
from jax.experimental.pallas import tpu as pltpu

F32 = jnp.float32
BF16 = jnp.bfloat16

D_MODEL = 2048
DEPTH = 4
GRID_W = 64
N_MIXERS = 3
EPS = 1e-6

NA_HEADS = 16
NA_HEAD_DIM = 128
NA_WIN_ROWS = 8
NA_WIN_COLS = 16
NA_GROUP_ROWS = 4
NA_KEY_ROWS = 3 * NA_GROUP_ROWS
NA_HEADS_PER_STEP = 4

SSM_D_INNER = 4096
SSM_HEAD_DIM = 64
SSM_HEADS = 64
SSM_GROUPS = 8
SSM_HEADS_PER_GROUP = 8
SSM_STATE = 128
SSM_CONV = 5
SSM_CHUNK = 128
SSM_CONV_DIM = SSM_D_INNER + 2 * SSM_GROUPS * SSM_STATE
SSM_GROUP_WIDTH = SSM_HEADS_PER_GROUP * SSM_HEAD_DIM

DA_HEADS = 8
DA_HEAD_DIM = 128
DA_SUBLN_EPS = 1e-5
DA_Q_BLOCK = 1024
DA_Q_SUBBLOCKS = 4
ROPE_BASE = 10000.0

MASK_NEG = -1e30

V7X_VMEM_BYTES = 64 * 1024 * 1024
VMEM_LIMIT = (V7X_VMEM_BYTES * 3) // 4
SUBLANES_BF16 = 16


def _params(n_grid_dims):
    return pltpu.CompilerParams(
        dimension_semantics=("arbitrary",) * n_grid_dims, vmem_limit_bytes=VMEM_LIMIT)


def _sigmoid(x):
    return 1.0 / (1.0 + jnp.exp(-x))


def _dot(a, b):
    return jnp.dot(a, b, preferred_element_type=F32)


def _dot_nt(a, b):
    return lax.dot_general(a, b, (((1,), (1,)), ((), ())), preferred_element_type=F32)


def _split3(v):
    hi = v.astype(BF16)
    r1 = v - hi.astype(F32)
    mid = r1.astype(BF16)
    lo = (r1 - mid.astype(F32)).astype(BF16)
    return hi, mid, lo


def _mod_row(row0, t_lat, seq, n_batch):
    return jnp.where(row0 < t_lat, row0 // seq, n_batch)


def _adaln_kernel(c_ref, w_ref, b_ref, o_ref):
    x = c_ref[...]
    x = (x * _sigmoid(x)).astype(BF16)
    o_ref[0] = _dot(x, w_ref[0].astype(BF16)) + b_ref[0]


def _adaln_all(cond, ada_w, ada_b):
    depth, d, n = ada_w.shape
    bn = 1024
    return pl.pallas_call(
        _adaln_kernel,
        grid=(depth, n // bn),
        in_specs=[
            pl.BlockSpec((8, d), lambda l, j: (0, 0)),
            pl.BlockSpec((1, d, bn), lambda l, j: (l, 0, j)),
            pl.BlockSpec((1, 1, bn), lambda l, j: (l, 0, j)),
        ],
        out_specs=pl.BlockSpec((1, 8, bn), lambda l, j: (l, 0, j)),
        out_shape=jax.ShapeDtypeStruct((depth, 8, n), F32),
        compiler_params=_params(2),
        name="adaln",
    )(cond, ada_w, ada_b.reshape(depth, 1, n))


def _final_norm_kernel(h_ref, nw_ref, o_ref):
    x = h_ref[...]
    o_ref[...] = x * lax.rsqrt(jnp.mean(x * x, axis=-1, keepdims=True) + EPS) * nw_ref[...]


def _final_norm(h, norm_w, m_rows):
    d = h.shape[1]
    bm = 256
    return pl.pallas_call(
        _final_norm_kernel,
        grid=(m_rows // bm,),
        in_specs=[pl.BlockSpec((bm, d), lambda i: (i, 0)), pl.BlockSpec((1, d), lambda i: (0, 0))],
        out_specs=pl.BlockSpec((bm, d), lambda i: (i, 0)),
        out_shape=jax.ShapeDtypeStruct((m_rows, d), F32),
        compiler_params=_params(1),
        name="final_norm",
    )(h, norm_w.reshape(1, d))


def _mm_blocks(m_rows, k, n):
    bm = 1024 if m_rows % 1024 == 0 else 256
    bn = 1024 if n % 1024 == 0 else (512 if n % 512 == 0 else 128)
    if k >= 8192:
        bm, bn = min(bm, 512), min(bn, 512)
    elif k >= 4096:
        bn = min(bn, 512)
    return bm, bn


def _norm_mm_kernel(*refs, bm, shift_k, act, rope_row_blocks, rope_col_blocks, has_side, t_lat, seq, n_batch):
    h_ref, nw_ref, mod_ref, w_ref = refs[:4]
    pos = 4
    if rope_col_blocks:
        cos_ref, sin_ref = refs[pos:pos + 2]
        pos += 2
    if has_side:
        side_w_ref = refs[pos]
        pos += 1
    o_ref = refs[pos]
    pos += 1
    if has_side:
        side_o_ref = refs[pos]
        pos += 1
    u_ref = refs[pos]
    i, j = pl.program_id(0), pl.program_id(1)

    @pl.when(j == 0)
    def _():
        d = h_ref.shape[1]
        midx = _mod_row(i * bm, t_lat, seq, n_batch)
        x = h_ref[...]
        y = x * lax.rsqrt(jnp.mean(x * x, axis=-1, keepdims=True) + EPS) * nw_ref[...]
        shift = mod_ref[pl.ds(midx, 1), shift_k * d:(shift_k + 1) * d]
        scale = mod_ref[pl.ds(midx, 1), (shift_k + 1) * d:(shift_k + 2) * d]
        u_ref[...] = (y * (1.0 + scale) + shift).astype(u_ref.dtype)
        if has_side:
            side_o_ref[...] = _dot(u_ref[...], side_w_ref[...])

    acc = _dot(u_ref[...], w_ref[...])
    if act == "relu2":
        acc = jnp.maximum(acc, 0.0)
        acc = acc * acc
    if not rope_col_blocks:
        o_ref[...] = acc.astype(o_ref.dtype)
        return

    do_rope = (i < rope_row_blocks) & (j < rope_col_blocks)

    @pl.when(do_rope)
    def _():
        hd = cos_ref.shape[1]
        lane = lax.broadcasted_iota(jnp.int32, (bm, hd), 1)
        first = (lane % (hd // 2)) < hd // 4
        for c in range(acc.shape[1] // hd):
            xh = acc[:, c * hd:(c + 1) * hd]
            partner = jnp.where(first, pltpu.roll(xh, hd - hd // 4, 1), pltpu.roll(xh, hd // 4, 1))
            o_ref[:, c * hd:(c + 1) * hd] = (xh * cos_ref[...] + partner * sin_ref[...]).astype(o_ref.dtype)

    @pl.when(jnp.logical_not(do_rope))
    def _():
        o_ref[...] = acc.astype(o_ref.dtype)


def _norm_mm(h, norm_w, mod_l, shift_k, w, m_rows, geom, act=None, rope=None, side_w=None):
    d = h.shape[1]
    k, n = w.shape
    bm, bn = _mm_blocks(m_rows, k, n)
    kern = functools.partial(
        _norm_mm_kernel, bm=bm, shift_k=shift_k, act=act, has_side=side_w is not None,
        rope_row_blocks=(rope[2] // bm if rope else 0), rope_col_blocks=(rope[3] // bn if rope else 0), **geom)
    in_specs = [
        pl.BlockSpec((bm, d), lambda i, j: (i, 0)),
        pl.BlockSpec((1, d), lambda i, j: (0, 0)),
        pl.BlockSpec(mod_l.shape, lambda i, j: (0, 0)),
        pl.BlockSpec((k, bn), lambda i, j: (0, j)),
    ]
    args = [h, norm_w.reshape(1, d), mod_l, w]
    out_specs = [pl.BlockSpec((bm, bn), lambda i, j: (i, j))]
    out_shape = [jax.ShapeDtypeStruct((m_rows, n), BF16)]
    if rope:
        seq_blocks = rope[0].shape[0] // bm
        hd = rope[0].shape[1]
        in_specs += [pl.BlockSpec((bm, hd), lambda i, j: (i % seq_blocks, 0))] * 2
        args += [rope[0], rope[1]]
    if side_w is not None:
        ns = side_w.shape[1]
        in_specs.append(pl.BlockSpec((k, ns), lambda i, j: (0, 0)))
        args.append(side_w)
        out_specs.append(pl.BlockSpec((bm, ns), lambda i, j: (i, 0)))
        out_shape.append(jax.ShapeDtypeStruct((m_rows, ns), F32))
    outs = pl.pallas_call(
        kern,
        grid=(m_rows // bm, n // bn),
        in_specs=in_specs,
        out_specs=out_specs,
        out_shape=out_shape,
        scratch_shapes=[pltpu.VMEM((bm, d), BF16)],
        compiler_params=_params(2),
        name="norm_mm" + ("_" + act if act else "") + ("_rope" if rope else "") + ("_side" if side_w is not None else ""),
    )(*args)
    return outs if side_w is not None else outs[0]


def _mm_gated_kernel(x_ref, w_ref, res_ref, gate_ref, o_ref, *, bm, row_off, t_lat, seq, n_batch):
    midx = _mod_row(row_off + pl.program_id(0) * bm, t_lat, seq, n_batch)
    acc = _dot(x_ref[...], w_ref[...])
    o_ref[...] = res_ref[...] + gate_ref[pl.ds(midx, 1), :] * acc


def _mm_gated(x, w, res, mod_l, gate_k, m_rows, geom, row_off=0):
    k, n = w.shape
    bm, bn = _mm_blocks(m_rows, k, n)
    assert row_off % bm == 0
    blk_off = row_off // bm
    gate_blk0 = gate_k * (n // bn)
    return pl.pallas_call(
        functools.partial(_mm_gated_kernel, bm=bm, row_off=row_off, **geom),
        grid=(m_rows // bm, n // bn),
        in_specs=[
            pl.BlockSpec((bm, k), lambda i, j: (i, 0)),
            pl.BlockSpec((k, bn), lambda i, j: (0, j)),
            pl.BlockSpec((bm, bn), lambda i, j: (blk_off + i, j)),
            pl.BlockSpec((8, bn), lambda i, j: (0, gate_blk0 + j)),
        ],
        out_specs=pl.BlockSpec((bm, bn), lambda i, j: (blk_off + i, j)),
        out_shape=jax.ShapeDtypeStruct(res.shape, F32),
        input_output_aliases={2: 0},
        compiler_params=_params(2),
        name="mm_gated",
    )(x, w, res, mod_l)


def _na_pair_layout(rows):
    gq, gk = NA_GROUP_ROWS, NA_KEY_ROWS
    pairs = {}
    idx = np.zeros((3, gq, gk // 2), np.int32)
    for typ, (r0, ks) in enumerate(((0, 0), (gq, 0), (rows - gq, rows - gk))):
        for i in range(gq):
            r = r0 + i
            rs = min(max(r - NA_WIN_ROWS // 2, 0), rows - NA_WIN_ROWS)
            dr = [kr - r + NA_WIN_ROWS - 1 if rs <= kr < rs + NA_WIN_ROWS else -1 for kr in range(ks, ks + gk)]
            for p in range(gk // 2):
                idx[typ, i, p] = pairs.setdefault((dr[2 * p], dr[2 * p + 1]), len(pairs))
    return list(pairs), idx


def _na_pair_table(rpb, pairs, out_scale):
    heads = rpb.shape[0]
    col = np.arange(GRID_W)
    col_start = np.clip(col - NA_WIN_COLS // 2, 0, GRID_W - NA_WIN_COLS)
    col_mask = (col[None, :] >= col_start[:, None]) & (col[None, :] < col_start[:, None] + NA_WIN_COLS)
    dc_idx = np.clip(col[None, :] - col[:, None], -(NA_WIN_COLS - 1), NA_WIN_COLS - 1) + NA_WIN_COLS - 1
    onehot = (dc_idx[None] == np.arange(2 * NA_WIN_COLS - 1)[:, None, None]).astype(np.float32)
    per_dr = jnp.einsum("hrd,dqk->hrqk", rpb.astype(F32), onehot, precision=lax.Precision.HIGHEST)
    per_dr = jnp.where(col_mask[None, None], per_dr * out_scale, MASK_NEG)
    masked = jnp.full((heads, GRID_W, GRID_W), MASK_NEG, F32)
    block = lambda dr: masked if dr < 0 else per_dr[:, dr]
    return jnp.stack([jnp.concatenate([block(a), block(b)], axis=-1) for a, b in pairs], axis=1)


def _softmax_pv(scores, values, out_dtype):
    m = functools.reduce(jnp.maximum, [jnp.max(t, axis=1, keepdims=True) for t in scores])
    probs = [jnp.exp2(t - m) for t in scores]
    denom = functools.reduce(jnp.add, [jnp.sum(p, axis=1, keepdims=True) for p in probs])
    o = None
    for p, v in zip(probs, values):
        part = _dot(p.astype(BF16), v)
        o = part if o is None else o + part
    return (o * (1.0 / denom)).astype(out_dtype)


def _na_kernel(idx_ref, q_ref, k0_ref, k1_ref, k2_ref, v0_ref, v1_ref, v2_ref, kc_ref, vc_ref, pair_ref, o_ref,
               *, exp2_scale, n_groups):
    dh = NA_HEAD_DIM
    gq, n_pairs = NA_GROUP_ROWS, NA_KEY_ROWS // 2
    g = pl.program_id(1)
    typ = jnp.where(g == 0, 0, jnp.where(g == n_groups - 1, 2, 1))
    for hh in range(NA_HEADS_PER_STEP):
        sl = slice(hh * dh, (hh + 1) * dh)
        q = q_ref[:, sl]
        bias = jnp.concatenate(
            [jnp.concatenate([pair_ref[hh, idx_ref[(typ * gq + i) * n_pairs + p]] for p in range(n_pairs)], axis=1)
             for i in range(gq)], axis=0)
        s_loc = jnp.concatenate([_dot_nt(q, kr[:, sl]) for kr in (k0_ref, k1_ref, k2_ref)], axis=1)
        t_loc = s_loc * exp2_scale + bias
        t_ctx = _dot_nt(q, kc_ref[:, sl]) * exp2_scale
        kb = k0_ref.shape[0]
        scores = [t_loc[:, j * kb:(j + 1) * kb] for j in range(3)] + [t_ctx]
        values = [vr[:, sl] for vr in (v0_ref, v1_ref, v2_ref, vc_ref)]
        o_ref[:, sl] = _softmax_pv(scores, values, o_ref.dtype)


def _na_ctx_kernel(q_ref, kc_ref, vc_ref, o_ref, *, exp2_scale):
    dh = NA_HEAD_DIM
    for hh in range(NA_HEADS_PER_STEP):
        sl = slice(hh * dh, (hh + 1) * dh)
        t_ctx = _dot_nt(q_ref[:, sl], kc_ref[:, sl]) * exp2_scale
        o_ref[:, sl] = _softmax_pv([t_ctx], [vc_ref[:, sl]], o_ref.dtype)


def _na_attention(qkv, rpb, n_batch, seq, lc, ctx_out):
    hd = NA_HEADS * NA_HEAD_DIM
    gtok = NA_GROUP_ROWS * GRID_W
    ng = seq // gtok
    assert lc == gtok and ng >= 4
    hw = NA_HEADS_PER_STEP * NA_HEAD_DIM
    nhq = NA_HEADS // NA_HEADS_PER_STEP
    t_lat = n_batch * seq
    ctx_blk0 = n_batch * ng
    exp2_scale = NA_HEAD_DIM ** -0.5 * math.log2(math.e)
    pairs, idx = _na_pair_layout(seq // GRID_W)
    pair_tab = _na_pair_table(rpb, pairs, math.log2(math.e))

    def k_row(g, b, j):
        return b * ng + jnp.clip(g - 1, 0, ng - 3) + j

    blk = (gtok, hw)
    in_specs = [pl.BlockSpec(memory_space=pltpu.SMEM), pl.BlockSpec(blk, lambda h, g, b: (b * ng + g, h))]
    for col0 in (nhq, 2 * nhq):
        for j in range(3):
            in_specs.append(pl.BlockSpec(blk, lambda h, g, b, j=j, col0=col0: (k_row(g, b, j), col0 + h)))
    in_specs.append(pl.BlockSpec(blk, lambda h, g, b: (ctx_blk0 + b, nhq + h)))
    in_specs.append(pl.BlockSpec(blk, lambda h, g, b: (ctx_blk0 + b, 2 * nhq + h)))
    in_specs.append(pl.BlockSpec((NA_HEADS_PER_STEP, len(pairs), GRID_W, 2 * GRID_W), lambda h, g, b: (h, 0, 0, 0)))
    y_lat = pl.pallas_call(
        functools.partial(_na_kernel, exp2_scale=exp2_scale, n_groups=ng),
        grid=(nhq, ng, n_batch),
        in_specs=in_specs,
        out_specs=pl.BlockSpec(blk, lambda h, g, b: (b * ng + g, h)),
        out_shape=jax.ShapeDtypeStruct((t_lat, hd), BF16),
        compiler_params=_params(3),
        name="na_attention",
    )(jnp.asarray(idx.reshape(-1)), *([qkv] * 9), pair_tab)
    if not ctx_out:
        return y_lat, None
    y_ctx = pl.pallas_call(
        functools.partial(_na_ctx_kernel, exp2_scale=exp2_scale),
        grid=(nhq, n_batch),
        in_specs=[
            pl.BlockSpec(blk, lambda h, b: (ctx_blk0 + b, h)),
            pl.BlockSpec(blk, lambda h, b: (ctx_blk0 + b, nhq + h)),
            pl.BlockSpec(blk, lambda h, b: (ctx_blk0 + b, 2 * nhq + h)),
        ],
        out_specs=pl.BlockSpec(blk, lambda h, b: (b, h)),
        out_shape=jax.ShapeDtypeStruct((n_batch * lc, hd), BF16),
        compiler_params=_params(2),
        name="na_attention_ctx",
    )(qkv, qkv, qkv)
    return y_lat, y_ctx


def _rope_tables(seq, width):
    t = jnp.arange(seq)
    pos = jnp.stack([t // GRID_W, t % GRID_W], axis=-1).astype(F32)
    n_freq = DA_HEAD_DIM // 4
    inv_freq = ROPE_BASE ** (-jnp.arange(n_freq, dtype=F32) / n_freq)
    ang = pos[:, :, None] * inv_freq
    cos = jnp.broadcast_to(jnp.cos(ang)[:, :, None, :], (seq, 2, 2, n_freq)).reshape(seq, DA_HEAD_DIM)
    sign = jnp.array([-1.0, 1.0], F32)[None, None, :, None]
    sin = (jnp.broadcast_to(jnp.sin(ang)[:, :, None, :], (seq, 2, 2, n_freq)) * sign).reshape(seq, DA_HEAD_DIM)
    reps = width // DA_HEAD_DIM
    return jnp.tile(cos, (1, reps)), jnp.tile(sin, (1, reps))


def _da_kernel(lam_ref, q_ref, *refs, exp2_scale, out_scale, n_sub, with_latent):
    n_kv = 2 if with_latent else 1
    k_refs, v_refs = refs[:n_kv], refs[n_kv:2 * n_kv]
    sw_ref, o_ref = refs[2 * n_kv], refs[2 * n_kv + 1]
    dh = DA_HEAD_DIM
    lam = lam_ref[0]
    rows = q_ref.shape[0] // n_sub
    for sub in range(n_sub):
        rs = slice(sub * rows, (sub + 1) * rows)
        probs, sums = [], []
        for comp in range(2):
            sl = slice(comp * dh, (comp + 1) * dh)
            q = q_ref[rs, sl]
            s = [_dot_nt(q, k_ref[:, sl]) for k_ref in k_refs]
            m = functools.reduce(jnp.maximum, [jnp.max(x, axis=1, keepdims=True) for x in s])
            p = [jnp.exp2((x - m) * exp2_scale) for x in s]
            probs.append(p)
            sums.append(functools.reduce(jnp.add, [jnp.sum(x, axis=1, keepdims=True) for x in p]))
        ratio = lam * sums[0] / sums[1]
        o = None
        for p1, p2, v_ref in zip(probs[0], probs[1], v_refs):
            part = _dot((p1 - ratio * p2).astype(BF16), v_ref[...])
            o = part if o is None else o + part
        o = o * (1.0 / sums[0])
        y = o * lax.rsqrt(jnp.mean(o * o, axis=-1, keepdims=True) + DA_SUBLN_EPS) * sw_ref[...]
        o_ref[rs, :] = (y * out_scale).astype(o_ref.dtype)


def _diff_attention(qkv, lam_full, subln_w, lambda_init, n_batch, seq, lc, ctx_out):
    hw = 2 * DA_HEAD_DIM
    bq, n_sub = DA_Q_BLOCK, DA_Q_SUBBLOCKS
    nq = seq // bq
    t_lat = n_batch * seq
    ctx_blk0 = t_lat // lc
    lam1 = lam_full.reshape(1).astype(F32)
    sw = subln_w.reshape(1, hw)
    common = dict(exp2_scale=DA_HEAD_DIM ** -0.5 * math.log2(math.e), out_scale=1.0 - lambda_init)
    smem = pl.BlockSpec(memory_space=pltpu.SMEM)
    y_lat = pl.pallas_call(
        functools.partial(_da_kernel, n_sub=n_sub, with_latent=True, **common),
        grid=(n_batch, DA_HEADS, nq),
        in_specs=[
            smem,
            pl.BlockSpec((bq, hw), lambda b, h, i: (b * nq + i, h)),
            pl.BlockSpec((seq, hw), lambda b, h, i: (b, DA_HEADS + h)),
            pl.BlockSpec((lc, hw), lambda b, h, i: (ctx_blk0 + b, DA_HEADS + h)),
            pl.BlockSpec((seq, hw), lambda b, h, i: (b, 2 * DA_HEADS + h)),
            pl.BlockSpec((lc, hw), lambda b, h, i: (ctx_blk0 + b, 2 * DA_HEADS + h)),
            pl.BlockSpec((1, hw), lambda b, h, i: (0, 0)),
        ],
        out_specs=pl.BlockSpec((bq, hw), lambda b, h, i: (b * nq + i, h)),
        out_shape=jax.ShapeDtypeStruct((t_lat, DA_HEADS * hw), BF16),
        compiler_params=_params(3),
        name="diff_attention",
    )(lam1, qkv, qkv, qkv, qkv, qkv, sw)
    if not ctx_out:
        return y_lat, None
    y_ctx = pl.pallas_call(
        functools.partial(_da_kernel, n_sub=1, with_latent=False, **common),
        grid=(n_batch, DA_HEADS),
        in_specs=[
            smem,
            pl.BlockSpec((lc, hw), lambda b, h: (ctx_blk0 + b, h)),
            pl.BlockSpec((lc, hw), lambda b, h: (ctx_blk0 + b, DA_HEADS + h)),
            pl.BlockSpec((lc, hw), lambda b, h: (ctx_blk0 + b, 2 * DA_HEADS + h)),
            pl.BlockSpec((1, hw), lambda b, h: (0, 0)),
        ],
        out_specs=pl.BlockSpec((lc, hw), lambda b, h: (b, h)),
        out_shape=jax.ShapeDtypeStruct((n_batch * lc, DA_HEADS * hw), BF16),
        compiler_params=_params(2),
        name="diff_attention_ctx",
    )(lam1, qkv, qkv, qkv, sw)
    return y_lat, y_ctx


def _conv_kernel(prev_ref, cur_ref, next_ref, w_ref, b_ref, o_ref, *, rb, lat_blocks, lat_blocks_per_seq, ctx_blocks_per_seq):
    i = pl.program_id(0)
    is_lat = i < lat_blocks
    pos = jnp.where(is_lat, i % lat_blocks_per_seq, (i - lat_blocks) % ctx_blocks_per_seq)
    per_seq = jnp.where(is_lat, lat_blocks_per_seq, ctx_blocks_per_seq)
    halo = prev_ref.shape[0]
    prev = jnp.where(pos == 0, 0.0, prev_ref[...].astype(F32))
    nxt = jnp.where(pos == per_seq - 1, 0.0, next_ref[...].astype(F32))
    cat = jnp.concatenate([prev, cur_ref[...].astype(F32), nxt], axis=0)
    n = rb + 2 * halo
    acc = b_ref[...]
    for k in range(SSM_CONV):
        shifted = cat if k == SSM_CONV // 2 else pltpu.roll(cat, (SSM_CONV // 2 - k) % n, 0)
        acc = acc + w_ref[k:k + 1, :] * shifted[halo:halo + rb]
    o_ref[...] = (acc * _sigmoid(acc)).astype(o_ref.dtype)


def _ssm_conv(proj, conv_w, conv_b, n_batch, seq, lc):
    t = proj.shape[0]
    rb, ct, halo = 256, 2048, SUBLANES_BF16
    assert seq % rb == 0 and lc % rb == 0 and SSM_D_INNER % ct == 0 and SSM_CONV_DIM % ct == 0
    col0 = SSM_D_INNER // ct
    hb = rb // halo
    n_halo_blocks = t // halo
    kern = functools.partial(_conv_kernel, rb=rb, lat_blocks=n_batch * seq // rb,
                             lat_blocks_per_seq=seq // rb, ctx_blocks_per_seq=lc // rb)
    return pl.pallas_call(
        kern,
        grid=(t // rb, SSM_CONV_DIM // ct),
        in_specs=[
            pl.BlockSpec((halo, ct), lambda i, j: (jnp.maximum(i * hb - 1, 0), col0 + j)),
            pl.BlockSpec((rb, ct), lambda i, j: (i, col0 + j)),
            pl.BlockSpec((halo, ct), lambda i, j: (jnp.minimum((i + 1) * hb, n_halo_blocks - 1), col0 + j)),
            pl.BlockSpec((SSM_CONV, ct), lambda i, j: (0, j)),
            pl.BlockSpec((1, ct), lambda i, j: (0, j)),
        ],
        out_specs=pl.BlockSpec((rb, ct), lambda i, j: (i, j)),
        out_shape=jax.ShapeDtypeStruct((t, SSM_CONV_DIM), BF16),
        compiler_params=_params(2),
        name="ssm_conv",
    )(proj, proj, proj, conv_w, conv_b.reshape(1, SSM_CONV_DIM))


def _ssd_kernel(xbc_ref, dt_ref, dtb_ref, alog_ref, tri_ref, exp_ref, y_ref, state_ref):
    q = SSM_CHUNK
    nh = SSM_HEADS
    gw = SSM_GROUP_WIDTH
    direction = pl.program_id(1)
    fwd = direction == 0

    @pl.when(pl.program_id(2) == 0)
    def _():
        state_ref[...] = jnp.zeros_like(state_ref)

    pre = dt_ref[...] + dtb_ref[...]
    dt_all = jnp.maximum(pre, 0.0) + jnp.log1p(jnp.exp(-jnp.abs(pre)))
    a_all = dt_all * (-jnp.exp(alog_ref[...]))
    tri = tri_ref[0]
    acum_all = sum(_dot(tri, piece) for piece in _split3(a_all))
    lane = lax.broadcasted_iota(jnp.int32, (q, 2 * nh), 1)
    mine = jnp.where(lane < nh, 0, 1) == direction
    first_head = lax.broadcasted_iota(jnp.int32, (q, 2 * SSM_HEAD_DIM), 1) < SSM_HEAD_DIM
    expand = exp_ref[...]
    dt_x = sum(_dot(piece, expand) for piece in _split3(jnp.where(mine, dt_all, 0.0)))
    acum_x = sum(_dot(piece, expand) for piece in _split3(jnp.where(mine, acum_all, 0.0)))
    total_x = jnp.where(fwd, acum_x[q - 1:q], acum_x[0:1])
    acum = jnp.where(fwd, acum_all, pltpu.roll(acum_all, nh, 1))
    acum_t = acum.T
    seen = tri.astype(F32) > 0.5

    for g in range(SSM_GROUPS):
        cs = slice(g * gw, (g + 1) * gw)
        b_g = xbc_ref[:, SSM_D_INNER + g * SSM_STATE:SSM_D_INNER + (g + 1) * SSM_STATE]
        c_off = SSM_D_INNER + SSM_GROUPS * SSM_STATE
        c_g = xbc_ref[:, c_off + g * SSM_STATE:c_off + (g + 1) * SSM_STATE]
        xdt = xbc_ref[:, cs].astype(F32) * dt_x[:, cs]
        cb = _dot_nt(c_g, b_g)
        h_start = state_ref[g]
        y_off = _dot(c_g, h_start.astype(BF16)) * jnp.exp(acum_x[:, cs])
        y_diag = []
        for pr in range(SSM_HEADS_PER_GROUP // 2):
            mats = []
            for r in (2 * pr, 2 * pr + 1):
                hh = g * SSM_HEADS_PER_GROUP + r
                col = jnp.sum(jnp.where(lane == hh, acum, 0.0), axis=1, keepdims=True)
                seg = col - acum_t[hh:hh + 1, :]
                mats.append((cb * jnp.where(seen, jnp.exp(seg), 0.0)).astype(BF16))
            x2 = xdt[:, pr * 2 * SSM_HEAD_DIM:(pr + 1) * 2 * SSM_HEAD_DIM]
            rhs = jnp.concatenate([jnp.where(first_head, x2, 0.0), jnp.where(first_head, 0.0, x2)], axis=0)
            y_diag.append(_dot(jnp.concatenate(mats, axis=1), rhs.astype(BF16)))
        y_ref[0, :, cs] = jnp.concatenate(y_diag, axis=1) + y_off
        to_end = jnp.exp(total_x[:, cs] - acum_x[:, cs])
        b_t = b_g.astype(F32).T.astype(BF16)
        state_ref[g] = h_start * jnp.exp(total_x[:, cs]) + _dot(b_t, (xdt * to_end).astype(BF16))


def _ssd_scan(xbc, dt_raw, dt_bias, a_log, n_batch, seq, lc):
    t = xbc.shape[0]
    q = SSM_CHUNK
    n_lat, n_ctx = seq // q, lc // q
    steps = n_lat + n_ctx
    ctx_blk0 = n_batch * n_lat

    def blk(b, d, s):
        ctx_chunk = jnp.where(d == 0, s, n_ctx - 1 - s)
        lat_chunk = jnp.where(d == 0, s - n_ctx, steps - 1 - s)
        return jnp.where(s < n_ctx, ctx_blk0 + b * n_ctx + ctx_chunk, b * n_lat + lat_chunk)

    idx = np.arange(q)
    tri = np.stack([idx[None, :] <= idx[:, None], idx[None, :] >= idx[:, None]]).astype(np.float32)
    expand = np.tile(np.repeat(np.eye(SSM_HEADS, dtype=np.float32), SSM_HEAD_DIM, axis=1), (2, 1))
    return pl.pallas_call(
        _ssd_kernel,
        grid=(n_batch, 2, steps),
        in_specs=[
            pl.BlockSpec((q, SSM_CONV_DIM), lambda b, d, s: (blk(b, d, s), 0)),
            pl.BlockSpec((q, 2 * SSM_HEADS), lambda b, d, s: (blk(b, d, s), 0)),
            pl.BlockSpec((1, 2 * SSM_HEADS), lambda b, d, s: (0, 0)),
            pl.BlockSpec((1, 2 * SSM_HEADS), lambda b, d, s: (0, 0)),
            pl.BlockSpec((1, q, q), lambda b, d, s: (d, 0, 0)),
            pl.BlockSpec((2 * SSM_HEADS, SSM_D_INNER), lambda b, d, s: (0, 0)),
        ],
        out_specs=pl.BlockSpec((1, q, SSM_D_INNER), lambda b, d, s: (d, blk(b, d, s), 0)),
        out_shape=jax.ShapeDtypeStruct((2, t, SSM_D_INNER), F32),
        scratch_shapes=[pltpu.VMEM((SSM_GROUPS, SSM_STATE, SSM_GROUP_WIDTH), F32)],
        compiler_params=_params(3),
        name="ssd_scan",
    )(xbc, dt_raw, dt_bias.reshape(1, -1), a_log.reshape(1, -1), jnp.asarray(tri, BF16), jnp.asarray(expand, BF16))


def _ssm_finish_kernel(yf_ref, yb_ref, x_ref, z_ref, dsk_ref, nw_ref, o_ref):
    gw = SSM_GROUP_WIDTH
    z = z_ref[...].astype(F32)
    y = (yf_ref[0] + yb_ref[0] + x_ref[...].astype(F32) * dsk_ref[...]) * (z * _sigmoid(z))
    for g in range(SSM_GROUPS):
        cs = slice(g * gw, (g + 1) * gw)
        yg = y[:, cs]
        o_ref[:, cs] = (yg * lax.rsqrt(jnp.mean(yg * yg, axis=-1, keepdims=True) + EPS) * nw_ref[:, cs]).astype(o_ref.dtype)


def _ssm_finish(y2, xbc, proj, d_skip, norm_w):
    t = xbc.shape[0]
    di = SSM_D_INNER
    rb = 128
    dsk = jnp.repeat(d_skip.astype(F32), SSM_HEAD_DIM).reshape(1, di)
    return pl.pallas_call(
        _ssm_finish_kernel,
        grid=(t // rb,),
        in_specs=[
            pl.BlockSpec((1, rb, di), lambda i: (0, i, 0)),
            pl.BlockSpec((1, rb, di), lambda i: (1, i, 0)),
            pl.BlockSpec((rb, di), lambda i: (i, 0)),
            pl.BlockSpec((rb, di), lambda i: (i, 0)),
            pl.BlockSpec((1, di), lambda i: (0, 0)),
            pl.BlockSpec((1, di), lambda i: (0, 0)),
        ],
        out_specs=pl.BlockSpec((rb, di), lambda i: (i, 0)),
        out_shape=jax.ShapeDtypeStruct((t, di), BF16),
        compiler_params=_params(1),
        name="ssm_finish",
    )(y2, y2, xbc, proj, dsk, norm_w.reshape(1, di))


def _mamba2(h, pre_norm_w, mod_l, geom, w_in, conv_w, conv_b, dt_bias, a_log, d_skip, norm_w, n_batch, seq, lc):
    t = h.shape[0]
    n_main = SSM_D_INNER + SSM_CONV_DIM
    proj, dt_raw = _norm_mm(h, pre_norm_w, mod_l, 0, w_in[:, :n_main].astype(BF16), t, geom,
                            side_w=w_in[:, n_main:].astype(BF16))
    xbc = _ssm_conv(proj, conv_w, conv_b, n_batch, seq, lc)
    y2 = _ssd_scan(xbc, dt_raw, dt_bias, a_log, n_batch, seq, lc)
    return _ssm_finish(y2, xbc, proj, d_skip, norm_w)


def kernel(x, c, ctx, c_ctx, ada_w, ada_b, norm_mix_w, norm_mlp_w, mlp_w1, mlp_w2, na_w_qkv, na_w_o, na_rpb,
           ssm_w_in, ssm_conv_w, ssm_conv_b, ssm_dt_bias, ssm_a_log, ssm_d, ssm_norm_w, ssm_w_out,
           da_w_qkv, da_w_o, da_lambda, da_subln_w, final_norm_w):
    n_batch, seq, d = x.shape
    lc = ctx.shape[1]
    t_lat, t_ctx = n_batch * seq, n_batch * lc
    t = t_lat + t_ctx
    geom = dict(t_lat=t_lat, seq=seq, n_batch=n_batch)
    assert d == D_MODEL and seq % GRID_W == 0 and n_batch < 8

    cond = jnp.concatenate([c, c_ctx[None, :], jnp.zeros((8 - n_batch - 1, d), F32)], axis=0)
    mod = _adaln_all(cond, ada_w, ada_b)
    h = jnp.concatenate([x.reshape(t_lat, d), ctx.reshape(t_ctx, d)], axis=0)

    for i in range(DEPTH):
        last = i == DEPTH - 1
        mixer, j = i % N_MIXERS, i // N_MIXERS
        y_ctx = None
        mod_l = mod[i]
        rows_out = t_lat if last else t
        if mixer == 0:
            qkv = _norm_mm(h, norm_mix_w[i], mod_l, 0, na_w_qkv[j].astype(BF16), t, geom)
            y, y_ctx = _na_attention(qkv, na_rpb[j], n_batch, seq, lc, not last)
            w_o = na_w_o[j]
        elif mixer == 1:
            y = _mamba2(h, norm_mix_w[i], mod_l, geom, ssm_w_in[j], ssm_conv_w[j], ssm_conv_b[j], ssm_dt_bias[j],
                        ssm_a_log[j], ssm_d[j], ssm_norm_w[j], n_batch, seq, lc)
            w_o = ssm_w_out[j]
        else:
            lambda_init = 0.8 - 0.6 * math.exp(-0.3 * i)
            qk_w = 2 * DA_HEADS * DA_HEAD_DIM
            cos_t, sin_t = _rope_tables(seq, DA_HEAD_DIM)
            qkv = _norm_mm(h, norm_mix_w[i], mod_l, 0, da_w_qkv[j].astype(BF16), t, geom,
                           rope=(cos_t, sin_t, t_lat, 2 * qk_w))
            lam = da_lambda[j].astype(F32)
            lam_full = jnp.exp(jnp.sum(lam[0] * lam[1])) - jnp.exp(jnp.sum(lam[2] * lam[3])) + lambda_init
            y, y_ctx = _diff_attention(qkv, lam_full, da_subln_w[j], lambda_init, n_batch, seq, lc, not last)
            w_o = da_w_o[j]
        w_o = w_o.astype(BF16)
        if y_ctx is None:
            h = _mm_gated(y, w_o, h, mod_l, 2, rows_out, geom)
        else:
            h = _mm_gated(y, w_o, h, mod_l, 2, t_lat, geom)
            h = _mm_gated(y_ctx, w_o, h, mod_l, 2, t_ctx, geom, row_off=t_lat)
        hidden = _norm_mm(h, norm_mlp_w[i], mod_l, 3, mlp_w1[i].astype(BF16), rows_out, geom, act="relu2")
        h = _mm_gated(hidden, mlp_w2[i].astype(BF16), h, mod_l, 5, rows_out, geom)

    return _final_norm(h, final_norm_w, t_lat).reshape(n_batch, seq, d)
```

```python
import functools
import math

import jax
import jax.numpy as jnp
import numpy as np
from jax import lax
from jax.experimental import pallas as pl
from jax.experimental.pallas import tpu as pltpu

F32 = jnp.float32
BF16 = jnp.bfloat16

D_MODEL = 2048
DEPTH = 4
GRID_W = 64
N_MIXERS = 3
EPS = 1e-6

NA_HEADS = 16
NA_HEAD_DIM = 128
NA_WIN_ROWS = 8
NA_WIN_COLS = 16
NA_GROUP_ROWS = 4
NA_KEY_ROWS = 3 * NA_GROUP_ROWS
NA_HEADS_PER_STEP = 4

SSM_D_INNER = 4096
SSM_HEAD_DIM = 64
SSM_HEADS = 64
SSM_GROUPS = 8
SSM_HEADS_PER_GROUP = 8
SSM_STATE = 128
SSM_CONV = 5
SSM_CHUNK = 128
SSM_CONV_DIM = SSM_D_INNER + 2 * SSM_GROUPS * SSM_STATE
SSM_GROUP_WIDTH = SSM_HEADS_PER_GROUP * SSM_HEAD_DIM

DA_HEADS = 8
DA_HEAD_DIM = 128
DA_SUBLN_EPS = 1e-5
DA_Q_BLOCK = 1024
DA_Q_SUBBLOCKS = 4
ROPE_BASE = 10000.0

MASK_NEG = -1e30

V7X_VMEM_BYTES = 64 * 1024 * 1024
VMEM_LIMIT = (V7X_VMEM_BYTES * 3) // 4
MM_INNER_COLS = 512
VMEM_LIMIT_F32_WEIGHTS = (V7X_VMEM_BYTES * 29) // 32
SUBLANES_BF16 = 16


def _params(n_grid_dims, vmem_limit=VMEM_LIMIT):
    return pltpu.CompilerParams(
        dimension_semantics=("arbitrary",) * n_grid_dims, vmem_limit_bytes=vmem_limit)


def _sigmoid(x):
    return 1.0 / (1.0 + jnp.exp(-x))


def _dot(a, b):
    return jnp.dot(a, b, preferred_element_type=F32)


def _dot_nt(a, b):
    return lax.dot_general(a, b, (((1,), (1,)), ((), ())), preferred_element_type=F32)


def _split3(v):
    hi = v.astype(BF16)
    r1 = v - hi.astype(F32)
    mid = r1.astype(BF16)
    lo = (r1 - mid.astype(F32)).astype(BF16)
    return hi, mid, lo


def _mod_row(row0, t_lat, seq, n_batch):
    return jnp.where(row0 < t_lat, row0 // seq, n_batch)


def _adaln_kernel(c_ref, w_ref, b_ref, o_ref):
    x = c_ref[...]
    x = (x * _sigmoid(x)).astype(BF16)
    o_ref[0] = _dot(x, w_ref[0].astype(BF16)) + b_ref[0]


def _adaln_all(cond, ada_w, ada_b):
    depth, d, n = ada_w.shape
    bn = 1024
    return pl.pallas_call(
        _adaln_kernel,
        grid=(depth, n // bn),
        in_specs=[
            pl.BlockSpec((8, d), lambda l, j: (0, 0)),
            pl.BlockSpec((1, d, bn), lambda l, j: (l, 0, j)),
            pl.BlockSpec((1, 1, bn), lambda l, j: (l, 0, j)),
        ],
        out_specs=pl.BlockSpec((1, 8, bn), lambda l, j: (l, 0, j)),
        out_shape=jax.ShapeDtypeStruct((depth, 8, n), F32),
        compiler_params=_params(2),
        name="adaln",
    )(cond, ada_w, ada_b.reshape(depth, 1, n))


def _final_norm_kernel(h_ref, nw_ref, o_ref):
    x = h_ref[...]
    o_ref[...] = x * lax.rsqrt(jnp.mean(x * x, axis=-1, keepdims=True) + EPS) * nw_ref[...]


def _final_norm(h, norm_w, m_rows):
    d = h.shape[1]
    bm = 256
    return pl.pallas_call(
        _final_norm_kernel,
        grid=(m_rows // bm,),
        in_specs=[pl.BlockSpec((bm, d), lambda i: (i, 0)), pl.BlockSpec((1, d), lambda i: (0, 0))],
        out_specs=pl.BlockSpec((bm, d), lambda i: (i, 0)),
        out_shape=jax.ShapeDtypeStruct((m_rows, d), F32),
        compiler_params=_params(1),
        name="final_norm",
    )(h, norm_w.reshape(1, d))


def _mm_blocks(m_rows, k, n):
    bm = 1024 if m_rows % 1024 == 0 else 256
    bn = 1024 if n % 1024 == 0 else (512 if n % 512 == 0 else 128)
    if k >= 8192:
        bm, bn = min(bm, 512), min(bn, 512)
    elif k >= 4096:
        bn = min(bn, 512)
    return bm, bn


def _norm_mm_kernel(*refs, bm, shift_k, act, rope_row_blocks, rope_col_blocks, has_side, t_lat, seq, n_batch):
    h_ref, nw_ref, mod_ref, w_ref = refs[:4]
    pos = 4
    if rope_col_blocks:
        cos_ref, sin_ref = refs[pos:pos + 2]
        pos += 2
    if has_side:
        side_w_ref = refs[pos]
        pos += 1
    o_ref = refs[pos]
    pos += 1
    if has_side:
        side_o_ref = refs[pos]
        pos += 1
    u_ref = refs[pos]
    i, j = pl.program_id(0), pl.program_id(1)

    @pl.when(j == 0)
    def _():
        d = h_ref.shape[1]
        midx = _mod_row(i * bm, t_lat, seq, n_batch)
        x = h_ref[...]
        y = x * lax.rsqrt(jnp.mean(x * x, axis=-1, keepdims=True) + EPS) * nw_ref[...]
        shift = mod_ref[pl.ds(midx, 1), shift_k * d:(shift_k + 1) * d]
        scale = mod_ref[pl.ds(midx, 1), (shift_k + 1) * d:(shift_k + 2) * d]
        u_ref[...] = (y * (1.0 + scale) + shift).astype(u_ref.dtype)
        if has_side:
            side_o_ref[...] = _dot(u_ref[...], side_w_ref[...])

    bn = o_ref.shape[1]
    hn = min(bn, MM_INNER_COLS)
    for c0 in range(0, bn, hn):
        acc = _dot(u_ref[...], w_ref[:, c0:c0 + hn].astype(BF16))
        if act == "relu2":
            acc = jnp.maximum(acc, 0.0)
            acc = acc * acc
        if not rope_col_blocks:
            o_ref[:, c0:c0 + hn] = acc.astype(o_ref.dtype)
            continue

        do_rope = (i < rope_row_blocks) & (j < rope_col_blocks)

        @pl.when(do_rope)
        def _(acc=acc, c0=c0):
            hd = cos_ref.shape[1]
            lane = lax.broadcasted_iota(jnp.int32, (bm, hd), 1)
            first = (lane % (hd // 2)) < hd // 4
            for c in range(hn // hd):
                xh = acc[:, c * hd:(c + 1) * hd]
                partner = jnp.where(first, pltpu.roll(xh, hd - hd // 4, 1), pltpu.roll(xh, hd // 4, 1))
                o_ref[:, c0 + c * hd:c0 + (c + 1) * hd] = (xh * cos_ref[...] + partner * sin_ref[...]).astype(o_ref.dtype)

        @pl.when(jnp.logical_not(do_rope))
        def _(acc=acc, c0=c0):
            o_ref[:, c0:c0 + hn] = acc.astype(o_ref.dtype)


def _norm_mm(h, norm_w, mod_l, shift_k, w_stack, layer, m_rows, geom, act=None, rope=None, side_w=None, n_cols=None):
    d = h.shape[1]
    k = w_stack.shape[1]
    n = n_cols or w_stack.shape[2]
    bm, bn = _mm_blocks(m_rows, k, n)
    kern = functools.partial(
        _norm_mm_kernel, bm=bm, shift_k=shift_k, act=act, has_side=side_w is not None,
        rope_row_blocks=(rope[2] // bm if rope else 0), rope_col_blocks=(rope[3] // bn if rope else 0), **geom)
    in_specs = [
        pl.BlockSpec((bm, d), lambda i, j: (i, 0)),
        pl.BlockSpec((1, d), lambda i, j: (0, 0)),
        pl.BlockSpec(mod_l.shape, lambda i, j: (0, 0)),
        pl.BlockSpec((None, k, bn), lambda i, j: (layer, 0, j)),
    ]
    args = [h, norm_w.reshape(1, d), mod_l, w_stack]
    out_specs = [pl.BlockSpec((bm, bn), lambda i, j: (i, j))]
    out_shape = [jax.ShapeDtypeStruct((m_rows, n), BF16)]
    if rope:
        seq_blocks = rope[0].shape[0] // bm
        hd = rope[0].shape[1]
        in_specs += [pl.BlockSpec((bm, hd), lambda i, j: (i % seq_blocks, 0))] * 2
        args += [rope[0], rope[1]]
    if side_w is not None:
        ns = side_w.shape[1]
        in_specs.append(pl.BlockSpec((k, ns), lambda i, j: (0, 0)))
        args.append(side_w)
        out_specs.append(pl.BlockSpec((bm, ns), lambda i, j: (i, 0)))
        out_shape.append(jax.ShapeDtypeStruct((m_rows, ns), F32))
    outs = pl.pallas_call(
        kern,
        grid=(m_rows // bm, n // bn),
        in_specs=in_specs,
        out_specs=out_specs,
        out_shape=out_shape,
        scratch_shapes=[pltpu.VMEM((bm, d), BF16)],
        compiler_params=_params(2, VMEM_LIMIT_F32_WEIGHTS),
        name="norm_mm" + ("_" + act if act else "") + ("_rope" if rope else "") + ("_side" if side_w is not None else ""),
    )(*args)
    return outs if side_w is not None else outs[0]


def _mm_gated_kernel(x_ref, w_ref, res_ref, gate_ref, o_ref, *, bm, row_off, t_lat, seq, n_batch):
    midx = _mod_row(row_off + pl.program_id(0) * bm, t_lat, seq, n_batch)
    acc = _dot(x_ref[...], w_ref[...])
    o_ref[...] = res_ref[...] + gate_ref[pl.ds(midx, 1), :] * acc


def _mm_gated(x, w, res, mod_l, gate_k, m_rows, geom, row_off=0):
    k, n = w.shape
    bm, bn = _mm_blocks(m_rows, k, n)
    assert row_off % bm == 0
    blk_off = row_off // bm
    gate_blk0 = gate_k * (n // bn)
    return pl.pallas_call(
        functools.partial(_mm_gated_kernel, bm=bm, row_off=row_off, **geom),
        grid=(m_rows // bm, n // bn),
        in_specs=[
            pl.BlockSpec((bm, k), lambda i, j: (i, 0)),
            pl.BlockSpec((k, bn), lambda i, j: (0, j)),
            pl.BlockSpec((bm, bn), lambda i, j: (blk_off + i, j)),
            pl.BlockSpec((8, bn), lambda i, j: (0, gate_blk0 + j)),
        ],
        out_specs=pl.BlockSpec((bm, bn), lambda i, j: (blk_off + i, j)),
        out_shape=jax.ShapeDtypeStruct(res.shape, F32),
        input_output_aliases={2: 0},
        compiler_params=_params(2),
        name="mm_gated",
    )(x, w, res, mod_l)


def _na_pair_layout(rows):
    gq, gk = NA_GROUP_ROWS, NA_KEY_ROWS
    pairs = {}
    idx = np.zeros((3, gq, gk // 2), np.int32)
    for typ, (r0, ks) in enumerate(((0, 0), (gq, 0), (rows - gq, rows - gk))):
        for i in range(gq):
            r = r0 + i
            rs = min(max(r - NA_WIN_ROWS // 2, 0), rows - NA_WIN_ROWS)
            dr = [kr - r + NA_WIN_ROWS - 1 if rs <= kr < rs + NA_WIN_ROWS else -1 for kr in range(ks, ks + gk)]
            for p in range(gk // 2):
                idx[typ, i, p] = pairs.setdefault((dr[2 * p], dr[2 * p + 1]), len(pairs))
    return list(pairs), idx


def _na_pair_table(rpb, pairs, out_scale):
    heads = rpb.shape[0]
    col = np.arange(GRID_W)
    col_start = np.clip(col - NA_WIN_COLS // 2, 0, GRID_W - NA_WIN_COLS)
    col_mask = (col[None, :] >= col_start[:, None]) & (col[None, :] < col_start[:, None] + NA_WIN_COLS)
    dc_idx = np.clip(col[None, :] - col[:, None], -(NA_WIN_COLS - 1), NA_WIN_COLS - 1) + NA_WIN_COLS - 1
    onehot = (dc_idx[None] == np.arange(2 * NA_WIN_COLS - 1)[:, None, None]).astype(np.float32)
    per_dr = jnp.einsum("hrd,dqk->hrqk", rpb.astype(F32), onehot, precision=lax.Precision.HIGHEST)
    per_dr = jnp.where(col_mask[None, None], per_dr * out_scale, MASK_NEG)
    masked = jnp.full((heads, GRID_W, GRID_W), MASK_NEG, F32)
    block = lambda dr: masked if dr < 0 else per_dr[:, dr]
    return jnp.stack([jnp.concatenate([block(a), block(b)], axis=-1) for a, b in pairs], axis=1)


def _softmax_pv(scores, values, out_dtype):
    m = functools.reduce(jnp.maximum, [jnp.max(t, axis=1, keepdims=True) for t in scores])
    probs = [jnp.exp2(t - m) for t in scores]
    denom = functools.reduce(jnp.add, [jnp.sum(p, axis=1, keepdims=True) for p in probs])
    o = None
    for p, v in zip(probs, values):
        part = _dot(p.astype(BF16), v)
        o = part if o is None else o + part
    return (o * (1.0 / denom)).astype(out_dtype)


def _na_kernel(idx_ref, q_ref, k0_ref, k1_ref, k2_ref, v0_ref, v1_ref, v2_ref, kc_ref, vc_ref, pair_ref, o_ref,
               *, exp2_scale, n_groups):
    dh = NA_HEAD_DIM
    gq, n_pairs = NA_GROUP_ROWS, NA_KEY_ROWS // 2
    g = pl.program_id(1)
    typ = jnp.where(g == 0, 0, jnp.where(g == n_groups - 1, 2, 1))
    for hh in range(NA_HEADS_PER_STEP):
        sl = slice(hh * dh, (hh + 1) * dh)
        q = q_ref[:, sl]
        bias = jnp.concatenate(
            [jnp.concatenate([pair_ref[hh, idx_ref[(typ * gq + i) * n_pairs + p]] for p in range(n_pairs)], axis=1)
             for i in range(gq)], axis=0)
        s_loc = jnp.concatenate([_dot_nt(q, kr[:, sl]) for kr in (k0_ref, k1_ref, k2_ref)], axis=1)
        t_loc = s_loc * exp2_scale + bias
        t_ctx = _dot_nt(q, kc_ref[:, sl]) * exp2_scale
        kb = k0_ref.shape[0]
        scores = [t_loc[:, j * kb:(j + 1) * kb] for j in range(3)] + [t_ctx]
        values = [vr[:, sl] for vr in (v0_ref, v1_ref, v2_ref, vc_ref)]
        o_ref[:, sl] = _softmax_pv(scores, values, o_ref.dtype)


def _na_ctx_kernel(q_ref, kc_ref, vc_ref, o_ref, *, exp2_scale):
    dh = NA_HEAD_DIM
    for hh in range(NA_HEADS_PER_STEP):
        sl = slice(hh * dh, (hh + 1) * dh)
        t_ctx = _dot_nt(q_ref[:, sl], kc_ref[:, sl]) * exp2_scale
        o_ref[:, sl] = _softmax_pv([t_ctx], [vc_ref[:, sl]], o_ref.dtype)


def _na_attention(qkv, rpb, n_batch, seq, lc, ctx_out):
    hd = NA_HEADS * NA_HEAD_DIM
    gtok = NA_GROUP_ROWS * GRID_W
    ng = seq // gtok
    assert lc == gtok and ng >= 4
    hw = NA_HEADS_PER_STEP * NA_HEAD_DIM
    nhq = NA_HEADS // NA_HEADS_PER_STEP
    t_lat = n_batch * seq
    ctx_blk0 = n_batch * ng
    exp2_scale = NA_HEAD_DIM ** -0.5 * math.log2(math.e)
    pairs, idx = _na_pair_layout(seq // GRID_W)
    pair_tab = _na_pair_table(rpb, pairs, math.log2(math.e))

    def k_row(g, b, j):
        return b * ng + jnp.clip(g - 1, 0, ng - 3) + j

    blk = (gtok, hw)
    in_specs = [pl.BlockSpec(memory_space=pltpu.SMEM), pl.BlockSpec(blk, lambda h, g, b: (b * ng + g, h))]
    for col0 in (nhq, 2 * nhq):
        for j in range(3):
            in_specs.append(pl.BlockSpec(blk, lambda h, g, b, j=j, col0=col0: (k_row(g, b, j), col0 + h)))
    in_specs.append(pl.BlockSpec(blk, lambda h, g, b: (ctx_blk0 + b, nhq + h)))
    in_specs.append(pl.BlockSpec(blk, lambda h, g, b: (ctx_blk0 + b, 2 * nhq + h)))
    in_specs.append(pl.BlockSpec((NA_HEADS_PER_STEP, len(pairs), GRID_W, 2 * GRID_W), lambda h, g, b: (h, 0, 0, 0)))
    y_lat = pl.pallas_call(
        functools.partial(_na_kernel, exp2_scale=exp2_scale, n_groups=ng),
        grid=(nhq, ng, n_batch),
        in_specs=in_specs,
        out_specs=pl.BlockSpec(blk, lambda h, g, b: (b * ng + g, h)),
        out_shape=jax.ShapeDtypeStruct((t_lat, hd), BF16),
        compiler_params=_params(3),
        name="na_attention",
    )(jnp.asarray(idx.reshape(-1)), *([qkv] * 9), pair_tab)
    if not ctx_out:
        return y_lat, None
    y_ctx = pl.pallas_call(
        functools.partial(_na_ctx_kernel, exp2_scale=exp2_scale),
        grid=(nhq, n_batch),
        in_specs=[
            pl.BlockSpec(blk, lambda h, b: (ctx_blk0 + b, h)),
            pl.BlockSpec(blk, lambda h, b: (ctx_blk0 + b, nhq + h)),
            pl.BlockSpec(blk, lambda h, b: (ctx_blk0 + b, 2 * nhq + h)),
        ],
        out_specs=pl.BlockSpec(blk, lambda h, b: (b, h)),
        out_shape=jax.ShapeDtypeStruct((n_batch * lc, hd), BF16),
        compiler_params=_params(2),
        name="na_attention_ctx",
    )(qkv, qkv, qkv)
    return y_lat, y_ctx


def _rope_tables(seq, width):
    t = jnp.arange(seq)
    pos = jnp.stack([t // GRID_W, t % GRID_W], axis=-1).astype(F32)
    n_freq = DA_HEAD_DIM // 4
    inv_freq = ROPE_BASE ** (-jnp.arange(n_freq, dtype=F32) / n_freq)
    ang = pos[:, :, None] * inv_freq
    cos = jnp.broadcast_to(jnp.cos(ang)[:, :, None, :], (seq, 2, 2, n_freq)).reshape(seq, DA_HEAD_DIM)
    sign = jnp.array([-1.0, 1.0], F32)[None, None, :, None]
    sin = (jnp.broadcast_to(jnp.sin(ang)[:, :, None, :], (seq, 2, 2, n_freq)) * sign).reshape(seq, DA_HEAD_DIM)
    reps = width // DA_HEAD_DIM
    return jnp.tile(cos, (1, reps)), jnp.tile(sin, (1, reps))


def _da_kernel(lam_ref, q_ref, *refs, exp2_scale, out_scale, n_sub, with_latent):
    n_kv = 2 if with_latent else 1
    k_refs, v_refs = refs[:n_kv], refs[n_kv:2 * n_kv]
    sw_ref, o_ref = refs[2 * n_kv], refs[2 * n_kv + 1]
    dh = DA_HEAD_DIM
    lam = lam_ref[0]
    rows = q_ref.shape[0] // n_sub
    for sub in range(n_sub):
        rs = slice(sub * rows, (sub + 1) * rows)
        probs, sums = [], []
        for comp in range(2):
            sl = slice(comp * dh, (comp + 1) * dh)
            q = q_ref[rs, sl]
            s = [_dot_nt(q, k_ref[:, sl]) for k_ref in k_refs]
            m = functools.reduce(jnp.maximum, [jnp.max(x, axis=1, keepdims=True) for x in s])
            p = [jnp.exp2((x - m) * exp2_scale) for x in s]
            probs.append(p)
            sums.append(functools.reduce(jnp.add, [jnp.sum(x, axis=1, keepdims=True) for x in p]))
        ratio = lam * sums[0] / sums[1]
        o = None
        for p1, p2, v_ref in zip(probs[0], probs[1], v_refs):
            part = _dot((p1 - ratio * p2).astype(BF16), v_ref[...])
            o = part if o is None else o + part
        o = o * (1.0 / sums[0])
        y = o * lax.rsqrt(jnp.mean(o * o, axis=-1, keepdims=True) + DA_SUBLN_EPS) * sw_ref[...]
        o_ref[rs, :] = (y * out_scale).astype(o_ref.dtype)


def _diff_attention(qkv, lam_full, subln_w, lambda_init, n_batch, seq, lc, ctx_out):
    hw = 2 * DA_HEAD_DIM
    bq, n_sub = DA_Q_BLOCK, DA_Q_SUBBLOCKS
    nq = seq // bq
    t_lat = n_batch * seq
    ctx_blk0 = t_lat // lc
    lam1 = lam_full.reshape(1).astype(F32)
    sw = subln_w.reshape(1, hw)
    common = dict(exp2_scale=DA_HEAD_DIM ** -0.5 * math.log2(math.e), out_scale=1.0 - lambda_init)
    smem = pl.BlockSpec(memory_space=pltpu.SMEM)
    y_lat = pl.pallas_call(
        functools.partial(_da_kernel, n_sub=n_sub, with_latent=True, **common),
        grid=(n_batch, DA_HEADS, nq),
        in_specs=[
            smem,
            pl.BlockSpec((bq, hw), lambda b, h, i: (b * nq + i, h)),
            pl.BlockSpec((seq, hw), lambda b, h, i: (b, DA_HEADS + h)),
            pl.BlockSpec((lc, hw), lambda b, h, i: (ctx_blk0 + b, DA_HEADS + h)),
            pl.BlockSpec((seq, hw), lambda b, h, i: (b, 2 * DA_HEADS + h)),
            pl.BlockSpec((lc, hw), lambda b, h, i: (ctx_blk0 + b, 2 * DA_HEADS + h)),
            pl.BlockSpec((1, hw), lambda b, h, i: (0, 0)),
        ],
        out_specs=pl.BlockSpec((bq, hw), lambda b, h, i: (b * nq + i, h)),
        out_shape=jax.ShapeDtypeStruct((t_lat, DA_HEADS * hw), BF16),
        compiler_params=_params(3),
        name="diff_attention",
    )(lam1, qkv, qkv, qkv, qkv, qkv, sw)
    if not ctx_out:
        return y_lat, None
    y_ctx = pl.pallas_call(
        functools.partial(_da_kernel, n_sub=1, with_latent=False, **common),
        grid=(n_batch, DA_HEADS),
        in_specs=[
            smem,
            pl.BlockSpec((lc, hw), lambda b, h: (ctx_blk0 + b, h)),
            pl.BlockSpec((lc, hw), lambda b, h: (ctx_blk0 + b, DA_HEADS + h)),
            pl.BlockSpec((lc, hw), lambda b, h: (ctx_blk0 + b, 2 * DA_HEADS + h)),
            pl.BlockSpec((1, hw), lambda b, h: (0, 0)),
        ],
        out_specs=pl.BlockSpec((lc, hw), lambda b, h: (b, h)),
        out_shape=jax.ShapeDtypeStruct((n_batch * lc, DA_HEADS * hw), BF16),
        compiler_params=_params(2),
        name="diff_attention_ctx",
    )(lam1, qkv, qkv, qkv, sw)
    return y_lat, y_ctx


def _conv_kernel(prev_ref, cur_ref, next_ref, w_ref, b_ref, o_ref, *, rb, lat_blocks, lat_blocks_per_seq, ctx_blocks_per_seq):
    i = pl.program_id(0)
    is_lat = i < lat_blocks
    pos = jnp.where(is_lat, i % lat_blocks_per_seq, (i - lat_blocks) % ctx_blocks_per_seq)
    per_seq = jnp.where(is_lat, lat_blocks_per_seq, ctx_blocks_per_seq)
    halo = prev_ref.shape[0]
    prev = jnp.where(pos == 0, 0.0, prev_ref[...].astype(F32))
    nxt = jnp.where(pos == per_seq - 1, 0.0, next_ref[...].astype(F32))
    cat = jnp.concatenate([prev, cur_ref[...].astype(F32), nxt], axis=0)
    n = rb + 2 * halo
    acc = b_ref[...]
    for k in range(SSM_CONV):
        shifted = cat if k == SSM_CONV // 2 else pltpu.roll(cat, (SSM_CONV // 2 - k) % n, 0)
        acc = acc + w_ref[k:k + 1, :] * shifted[halo:halo + rb]
    o_ref[...] = (acc * _sigmoid(acc)).astype(o_ref.dtype)


def _ssm_conv(proj, conv_w, conv_b, n_batch, seq, lc):
    t = proj.shape[0]
    rb, ct, halo = 256, 2048, SUBLANES_BF16
    assert seq % rb == 0 and lc % rb == 0 and SSM_D_INNER % ct == 0 and SSM_CONV_DIM % ct == 0
    col0 = SSM_D_INNER // ct
    hb = rb // halo
    n_halo_blocks = t // halo
    kern = functools.partial(_conv_kernel, rb=rb, lat_blocks=n_batch * seq // rb,
                             lat_blocks_per_seq=seq // rb, ctx_blocks_per_seq=lc // rb)
    return pl.pallas_call(
        kern,
        grid=(t // rb, SSM_CONV_DIM // ct),
        in_specs=[
            pl.BlockSpec((halo, ct), lambda i, j: (jnp.maximum(i * hb - 1, 0), col0 + j)),
            pl.BlockSpec((rb, ct), lambda i, j: (i, col0 + j)),
            pl.BlockSpec((halo, ct), lambda i, j: (jnp.minimum((i + 1) * hb, n_halo_blocks - 1), col0 + j)),
            pl.BlockSpec((SSM_CONV, ct), lambda i, j: (0, j)),
            pl.BlockSpec((1, ct), lambda i, j: (0, j)),
        ],
        out_specs=pl.BlockSpec((rb, ct), lambda i, j: (i, j)),
        out_shape=jax.ShapeDtypeStruct((t, SSM_CONV_DIM), BF16),
        compiler_params=_params(2),
        name="ssm_conv",
    )(proj, proj, proj, conv_w, conv_b.reshape(1, SSM_CONV_DIM))


def _ssd_kernel(xbc_ref, dt_ref, dtb_ref, alog_ref, tri_ref, half_ref, y_ref, state_ref):
    q = SSM_CHUNK
    nh = SSM_HEADS
    p2 = 2 * SSM_HEAD_DIM
    gw = SSM_GROUP_WIDTH
    log2e = math.log2(math.e)
    direction = pl.program_id(1)
    fwd = direction == 0

    @pl.when(pl.program_id(2) == 0)
    def _():
        state_ref[...] = jnp.zeros_like(state_ref)

    pre = dt_ref[...] + dtb_ref[...]
    dt_all = jnp.maximum(pre, 0.0) + jnp.log1p(jnp.exp(-jnp.abs(pre)))
    a_all = dt_all * (jnp.exp(alog_ref[...]) * -log2e)
    tri = tri_ref[0]
    acum_all = sum(_dot(tri, piece) for piece in _split3(a_all))
    acum = jnp.where(fwd, acum_all, pltpu.roll(acum_all, nh, 1))
    acum_t = acum.T
    dt_t = jnp.where(fwd, dt_all, pltpu.roll(dt_all, nh, 1)).T
    src_t = acum_t - jnp.log(dt_t) * log2e
    lane = lax.broadcasted_iota(jnp.int32, (q, q), 1)
    last = jnp.where(fwd, q - 1, 0)
    total = jnp.sum(jnp.where(lane == last, acum_t, 0.0), axis=1, keepdims=True)
    to_end_t = dt_t * jnp.exp2(total - acum_t)
    chunk_decay = jnp.broadcast_to(jnp.exp2(total), (q, q))
    seen = tri.astype(F32) > 0.5
    first_head = lane < SSM_HEAD_DIM
    c_off = SSM_D_INNER + SSM_GROUPS * SSM_STATE

    for g in range(SSM_GROUPS):
        b_g = xbc_ref[:, SSM_D_INNER + g * SSM_STATE:SSM_D_INNER + (g + 1) * SSM_STATE]
        c_g = xbc_ref[:, c_off + g * SSM_STATE:c_off + (g + 1) * SSM_STATE]
        cb = _dot_nt(c_g, b_g)
        b_t = b_g.astype(F32).T
        y_carried = _dot(c_g, state_ref[g].astype(BF16))
        for pr in range(SSM_HEADS_PER_GROUP // 2):
            ps = slice(pr * p2, (pr + 1) * p2)
            cols = slice(g * gw + pr * p2, g * gw + (pr + 1) * p2)
            x2 = xbc_ref[:, cols]
            x_bd = jnp.concatenate([x2 * half_ref[0], x2 * half_ref[1]], axis=0)
            intra, to_state, carry_scale, decay = [], [], [], []
            for r in (2 * pr, 2 * pr + 1):
                hh = g * SSM_HEADS_PER_GROUP + r
                col = jnp.sum(jnp.where(lane == hh, acum, 0.0), axis=1, keepdims=True)
                intra.append((cb * jnp.where(seen, jnp.exp2(col - src_t[hh:hh + 1, :]), 0.0)).astype(BF16))
                to_state.append((b_t * to_end_t[hh:hh + 1, :]).astype(BF16))
                carry_scale.append(jnp.exp2(col))
                decay.append(chunk_decay[hh:hh + 1, :])
            y_ref[0, :, cols] = (_dot(jnp.concatenate(intra, axis=1), x_bd)
                                 + jnp.where(first_head, carry_scale[0], carry_scale[1]) * y_carried[:, ps])
            state_ref[g, :, ps] = (state_ref[g, :, ps] * jnp.where(first_head[:1], decay[0], decay[1])
                                   + _dot(jnp.concatenate(to_state, axis=1), x_bd))


def _ssd_scan(xbc, dt_raw, dt_bias, a_log, n_batch, seq, lc):
    t = xbc.shape[0]
    q = SSM_CHUNK
    n_lat, n_ctx = seq // q, lc // q
    steps = n_lat + n_ctx
    ctx_blk0 = n_batch * n_lat

    def blk(b, d, s):
        ctx_chunk = jnp.where(d == 0, s, n_ctx - 1 - s)
        lat_chunk = jnp.where(d == 0, s - n_ctx, steps - 1 - s)
        return jnp.where(s < n_ctx, ctx_blk0 + b * n_ctx + ctx_chunk, b * n_lat + lat_chunk)

    idx = np.arange(q)
    tri = np.stack([idx[None, :] <= idx[:, None], idx[None, :] >= idx[:, None]]).astype(np.float32)
    first = np.broadcast_to(np.arange(2 * SSM_HEAD_DIM)[None, :] < SSM_HEAD_DIM, (q, 2 * SSM_HEAD_DIM))
    half = np.stack([first, ~first]).astype(np.float32)
    return pl.pallas_call(
        _ssd_kernel,
        grid=(n_batch, 2, steps),
        in_specs=[
            pl.BlockSpec((q, SSM_CONV_DIM), lambda b, d, s: (blk(b, d, s), 0)),
            pl.BlockSpec((q, 2 * SSM_HEADS), lambda b, d, s: (blk(b, d, s), 0)),
            pl.BlockSpec((1, 2 * SSM_HEADS), lambda b, d, s: (0, 0)),
            pl.BlockSpec((1, 2 * SSM_HEADS), lambda b, d, s: (0, 0)),
            pl.BlockSpec((1, q, q), lambda b, d, s: (d, 0, 0)),
            pl.BlockSpec((2, q, 2 * SSM_HEAD_DIM), lambda b, d, s: (0, 0, 0)),
        ],
        out_specs=pl.BlockSpec((1, q, SSM_D_INNER), lambda b, d, s: (d, blk(b, d, s), 0)),
        out_shape=jax.ShapeDtypeStruct((2, t, SSM_D_INNER), F32),
        scratch_shapes=[pltpu.VMEM((SSM_GROUPS, SSM_STATE, SSM_GROUP_WIDTH), F32)],
        compiler_params=_params(3),
        name="ssd_scan",
    )(xbc, dt_raw, dt_bias.reshape(1, -1), a_log.reshape(1, -1), jnp.asarray(tri, BF16), jnp.asarray(half, BF16))


def _ssm_finish_kernel(yf_ref, yb_ref, x_ref, z_ref, dsk_ref, nw_ref, o_ref):
    gw = SSM_GROUP_WIDTH
    z = z_ref[...].astype(F32)
    y = (yf_ref[0] + yb_ref[0] + x_ref[...].astype(F32) * dsk_ref[...]) * (z * _sigmoid(z))
    for g in range(SSM_GROUPS):
        cs = slice(g * gw, (g + 1) * gw)
        yg = y[:, cs]
        o_ref[:, cs] = (yg * lax.rsqrt(jnp.mean(yg * yg, axis=-1, keepdims=True) + EPS) * nw_ref[:, cs]).astype(o_ref.dtype)


def _ssm_finish(y2, xbc, proj, d_skip, norm_w):
    t = xbc.shape[0]
    di = SSM_D_INNER
    rb = 128
    dsk = jnp.repeat(d_skip.astype(F32), SSM_HEAD_DIM).reshape(1, di)
    return pl.pallas_call(
        _ssm_finish_kernel,
        grid=(t // rb,),
        in_specs=[
            pl.BlockSpec((1, rb, di), lambda i: (0, i, 0)),
            pl.BlockSpec((1, rb, di), lambda i: (1, i, 0)),
            pl.BlockSpec((rb, di), lambda i: (i, 0)),
            pl.BlockSpec((rb, di), lambda i: (i, 0)),
            pl.BlockSpec((1, di), lambda i: (0, 0)),
            pl.BlockSpec((1, di), lambda i: (0, 0)),
        ],
        out_specs=pl.BlockSpec((rb, di), lambda i: (i, 0)),
        out_shape=jax.ShapeDtypeStruct((t, di), BF16),
        compiler_params=_params(1),
        name="ssm_finish",
    )(y2, y2, xbc, proj, dsk, norm_w.reshape(1, di))


def _mamba2(h, pre_norm_w, mod_l, geom, w_in, conv_w, conv_b, dt_bias, a_log, d_skip, norm_w, n_batch, seq, lc):
    t = h.shape[0]
    n_main = SSM_D_INNER + SSM_CONV_DIM
    proj, dt_raw = _norm_mm(h, pre_norm_w, mod_l, 0, w_in[None], 0, t, geom, n_cols=n_main,
                            side_w=w_in[:, n_main:].astype(BF16))
    xbc = _ssm_conv(proj, conv_w, conv_b, n_batch, seq, lc)
    y2 = _ssd_scan(xbc, dt_raw, dt_bias, a_log, n_batch, seq, lc)
    return _ssm_finish(y2, xbc, proj, d_skip, norm_w)


def kernel(x, c, ctx, c_ctx, ada_w, ada_b, norm_mix_w, norm_mlp_w, mlp_w1, mlp_w2, na_w_qkv, na_w_o, na_rpb,
           ssm_w_in, ssm_conv_w, ssm_conv_b, ssm_dt_bias, ssm_a_log, ssm_d, ssm_norm_w, ssm_w_out,
           da_w_qkv, da_w_o, da_lambda, da_subln_w, final_norm_w):
    n_batch, seq, d = x.shape
    lc = ctx.shape[1]
    t_lat, t_ctx = n_batch * seq, n_batch * lc
    t = t_lat + t_ctx
    geom = dict(t_lat=t_lat, seq=seq, n_batch=n_batch)
    assert d == D_MODEL and seq % GRID_W == 0 and n_batch < 8

    cond = jnp.concatenate([c, c_ctx[None, :], jnp.zeros((8 - n_batch - 1, d), F32)], axis=0)
    mod = _adaln_all(cond, ada_w, ada_b)
    h = jnp.concatenate([x.reshape(t_lat, d), ctx.reshape(t_ctx, d)], axis=0)

    for i in range(DEPTH):
        last = i == DEPTH - 1
        mixer, j = i % N_MIXERS, i // N_MIXERS
        y_ctx = None
        mod_l = mod[i]
        rows_out = t_lat if last else t
        if mixer == 0:
            qkv = _norm_mm(h, norm_mix_w[i], mod_l, 0, na_w_qkv, j, t, geom)
            y, y_ctx = _na_attention(qkv, na_rpb[j], n_batch, seq, lc, not last)
            w_o = na_w_o[j]
        elif mixer == 1:
            y = _mamba2(h, norm_mix_w[i], mod_l, geom, ssm_w_in[j], ssm_conv_w[j], ssm_conv_b[j], ssm_dt_bias[j],
                        ssm_a_log[j], ssm_d[j], ssm_norm_w[j], n_batch, seq, lc)
            w_o = ssm_w_out[j]
        else:
            lambda_init = 0.8 - 0.6 * math.exp(-0.3 * i)
            qk_w = 2 * DA_HEADS * DA_HEAD_DIM
            cos_t, sin_t = _rope_tables(seq, DA_HEAD_DIM)
            qkv = _norm_mm(h, norm_mix_w[i], mod_l, 0, da_w_qkv.astype(BF16), j, t, geom,
                           rope=(cos_t, sin_t, t_lat, 2 * qk_w))
            lam = da_lambda[j].astype(F32)
            lam_full = jnp.exp(jnp.sum(lam[0] * lam[1])) - jnp.exp(jnp.sum(lam[2] * lam[3])) + lambda_init
            y, y_ctx = _diff_attention(qkv, lam_full, da_subln_w[j], lambda_init, n_batch, seq, lc, not last)
            w_o = da_w_o[j]
        w_o = w_o.astype(BF16)
        if y_ctx is None:
            h = _mm_gated(y, w_o, h, mod_l, 2, rows_out, geom)
        else:
            h = _mm_gated(y, w_o, h, mod_l, 2, t_lat, geom)
            h = _mm_gated(y_ctx, w_o, h, mod_l, 2, t_ctx, geom, row_off=t_lat)
        hidden = _norm_mm(h, norm_mlp_w[i], mod_l, 3, mlp_w1, i, rows_out, geom, act="relu2")
        h = _mm_gated(hidden, mlp_w2[i].astype(BF16), h, mod_l, 5, rows_out, geom)

    return _final_norm(h, final_norm_w, t_lat).reshape(n_batch, seq, d)
```

```python
import functools
import math

import jax
import jax.numpy as jnp
import numpy as np
from jax import lax
from jax.experimental import pallas as pl
from jax.experimental.pallas import tpu as pltpu

F32 = jnp.float32
BF16 = jnp.bfloat16

D_MODEL = 2048
DEPTH = 4
GRID_W = 64
N_MIXERS = 3
EPS = 1e-6

NA_HEADS = 16
NA_HEAD_DIM = 128
NA_WIN_ROWS = 8
NA_WIN_COLS = 16
NA_GROUP_ROWS = 4
NA_KEY_ROWS = 3 * NA_GROUP_ROWS
NA_HEADS_PER_STEP = 4

SSM_D_INNER = 4096
SSM_HEAD_DIM = 64
SSM_HEADS = 64
SSM_GROUPS = 8
SSM_HEADS_PER_GROUP = 8
SSM_STATE = 128
SSM_CONV = 5
SSM_CHUNK = 128
SSM_CONV_DIM = SSM_D_INNER + 2 * SSM_GROUPS * SSM_STATE
SSM_GROUP_WIDTH = SSM_HEADS_PER_GROUP * SSM_HEAD_DIM

DA_HEADS = 8
DA_HEAD_DIM = 128
DA_SUBLN_EPS = 1e-5
DA_Q_BLOCK = 1024
DA_Q_SUBBLOCKS = 4
ROPE_BASE = 10000.0

MASK_NEG = -1e30

V7X_VMEM_BYTES = 64 * 1024 * 1024
VMEM_LIMIT = (V7X_VMEM_BYTES * 3) // 4
MM_INNER_COLS = 512
VMEM_LIMIT_F32_WEIGHTS = (V7X_VMEM_BYTES * 29) // 32
SUBLANES_BF16 = 16
LANES = 128


def _params(n_grid_dims, vmem_limit=VMEM_LIMIT):
    return pltpu.CompilerParams(
        dimension_semantics=("arbitrary",) * n_grid_dims, vmem_limit_bytes=vmem_limit)


def _sigmoid(x):
    return 1.0 / (1.0 + jnp.exp(-x))


def _dot(a, b):
    return jnp.dot(a, b, preferred_element_type=F32)


def _dot_nt(a, b):
    return lax.dot_general(a, b, (((1,), (1,)), ((), ())), preferred_element_type=F32)


def _split3(v):
    hi = v.astype(BF16)
    r1 = v - hi.astype(F32)
    mid = r1.astype(BF16)
    lo = (r1 - mid.astype(F32)).astype(BF16)
    return hi, mid, lo


def _mod_row(row0, t_lat, seq, n_batch):
    return jnp.where(row0 < t_lat, row0 // seq, n_batch)


def _adaln_kernel(c_ref, w_ref, b_ref, o_ref):
    x = c_ref[...]
    x = (x * _sigmoid(x)).astype(BF16)
    o_ref[0] = _dot(x, w_ref[0].astype(BF16)) + b_ref[0]


def _adaln_all(cond, ada_w, ada_b):
    depth, d, n = ada_w.shape
    bn = 1024
    return pl.pallas_call(
        _adaln_kernel,
        grid=(depth, n // bn),
        in_specs=[
            pl.BlockSpec((8, d), lambda l, j: (0, 0)),
            pl.BlockSpec((1, d, bn), lambda l, j: (l, 0, j)),
            pl.BlockSpec((1, 1, bn), lambda l, j: (l, 0, j)),
        ],
        out_specs=pl.BlockSpec((1, 8, bn), lambda l, j: (l, 0, j)),
        out_shape=jax.ShapeDtypeStruct((depth, 8, n), F32),
        compiler_params=_params(2),
        name="adaln",
    )(cond, ada_w, ada_b.reshape(depth, 1, n))


def _final_norm_kernel(h_ref, nw_ref, o_ref):
    x = h_ref[...]
    o_ref[...] = x * lax.rsqrt(jnp.mean(x * x, axis=-1, keepdims=True) + EPS) * nw_ref[...]


def _final_norm(h, norm_w, m_rows):
    d = h.shape[1]
    bm = 256
    return pl.pallas_call(
        _final_norm_kernel,
        grid=(m_rows // bm,),
        in_specs=[pl.BlockSpec((bm, d), lambda i: (i, 0)), pl.BlockSpec((1, d), lambda i: (0, 0))],
        out_specs=pl.BlockSpec((bm, d), lambda i: (i, 0)),
        out_shape=jax.ShapeDtypeStruct((m_rows, d), F32),
        compiler_params=_params(1),
        name="final_norm",
    )(h, norm_w.reshape(1, d))


def _mm_blocks(m_rows, k, n):
    bm = 1024 if m_rows % 1024 == 0 else 256
    bn = 1024 if n % 1024 == 0 else (512 if n % 512 == 0 else 128)
    if k >= 8192:
        bm, bn = min(bm, 512), min(bn, 512)
    elif k >= 4096:
        bn = min(bn, 512)
    return bm, bn


def _next_norm_outputs(hb, j, midx, nw_ref, nscale_ref, xs_ref, ssq_ref):
    g = nw_ref[...] * (1.0 + nscale_ref[pl.ds(midx, 1), :])
    xs_ref[...] = (hb * g).astype(xs_ref.dtype)
    sq = hb * hb
    part = functools.reduce(jnp.add, [sq[:, c:c + LANES] for c in range(0, sq.shape[1], LANES)])

    @pl.when(j == 0)
    def _():
        ssq_ref[...] = part

    @pl.when(j > 0)
    def _():
        ssq_ref[...] = ssq_ref[...] + part


def _prep_kernel(x_ref, c_ref, nw_ref, nscale_ref, h_ref, xs_ref, ssq_ref, *, bm, lat_blocks, t_lat, seq, n_batch):
    i, j = pl.program_id(0), pl.program_id(1)
    midx = _mod_row(i * bm, t_lat, seq, n_batch)

    def emit(src_ref):
        hb = src_ref[...]
        h_ref[...] = hb
        _next_norm_outputs(hb, j, midx, nw_ref, nscale_ref, xs_ref, ssq_ref)

    pl.when(i < lat_blocks)(lambda: emit(x_ref))
    pl.when(i >= lat_blocks)(lambda: emit(c_ref))


def _prep(x2d, c2d, norm_w, mod_next, geom):
    t_lat, d = x2d.shape
    t_ctx = c2d.shape[0]
    t = t_lat + t_ctx
    bm = 256
    bn = 1024
    lat_blocks = t_lat // bm
    sc0 = d // bn
    return pl.pallas_call(
        functools.partial(_prep_kernel, bm=bm, lat_blocks=lat_blocks, **geom),
        grid=(t // bm, d // bn),
        in_specs=[
            pl.BlockSpec((bm, bn), lambda i, j: (jnp.minimum(i, lat_blocks - 1), j)),
            pl.BlockSpec((bm, bn), lambda i, j: (jnp.maximum(i - lat_blocks, 0), j)),
            pl.BlockSpec((1, bn), lambda i, j: (0, j)),
            pl.BlockSpec((8, bn), lambda i, j: (0, sc0 + j)),
        ],
        out_specs=[
            pl.BlockSpec((bm, bn), lambda i, j: (i, j)),
            pl.BlockSpec((bm, bn), lambda i, j: (i, j)),
            pl.BlockSpec((bm, LANES), lambda i, j: (i, 0)),
        ],
        out_shape=[
            jax.ShapeDtypeStruct((t, d), F32),
            jax.ShapeDtypeStruct((t, d), BF16),
            jax.ShapeDtypeStruct((t, LANES), F32),
        ],
        compiler_params=_params(2),
        name="prep",
    )(x2d, c2d, norm_w.reshape(1, d), mod_next)


def _mm_pre_kernel(*refs, bm, d_norm, shift_k, act, rope_row_blocks, rope_col_blocks, t_lat, seq, n_batch):
    xs_ref, ssq_ref, mod_ref, w_ref = refs[:4]
    pos = 4
    if rope_col_blocks:
        cos_ref, sin_ref = refs[pos:pos + 2]
        pos += 2
    o_ref, sw_ref = refs[pos], refs[pos + 1]
    i, j = pl.program_id(0), pl.program_id(1)
    bn = o_ref.shape[1]
    hn = min(bn, MM_INNER_COLS)

    @pl.when(i == 0)
    def _():
        shift = mod_ref[:, shift_k * d_norm:(shift_k + 1) * d_norm].astype(BF16)
        for c0 in range(0, bn, hn):
            sw_ref[j, :, c0:c0 + hn] = _dot(shift, w_ref[:, c0:c0 + hn].astype(BF16))

    midx = _mod_row(i * bm, t_lat, seq, n_batch)
    r = lax.rsqrt(jnp.sum(ssq_ref[...], axis=1, keepdims=True) * (1.0 / d_norm) + EPS)
    do_rope = (i < rope_row_blocks) & (j < rope_col_blocks)
    for c0 in range(0, bn, hn):
        acc = _dot(xs_ref[...], w_ref[:, c0:c0 + hn].astype(BF16))
        acc = acc * r + sw_ref[j, pl.ds(midx, 1), c0:c0 + hn]
        if act == "relu2":
            acc = jnp.maximum(acc, 0.0)
            acc = acc * acc
        if not rope_col_blocks:
            o_ref[:, c0:c0 + hn] = acc.astype(o_ref.dtype)
            continue

        @pl.when(do_rope)
        def _(acc=acc, c0=c0):
            hd = cos_ref.shape[1]
            lane = lax.broadcasted_iota(jnp.int32, (bm, hd), 1)
            first = (lane % (hd // 2)) < hd // 4
            for c in range(hn // hd):
                xh = acc[:, c * hd:(c + 1) * hd]
                partner = jnp.where(first, pltpu.roll(xh, hd - hd // 4, 1), pltpu.roll(xh, hd // 4, 1))
                o_ref[:, c0 + c * hd:c0 + (c + 1) * hd] = (xh * cos_ref[...] + partner * sin_ref[...]).astype(o_ref.dtype)

        @pl.when(jnp.logical_not(do_rope))
        def _(acc=acc, c0=c0):
            o_ref[:, c0:c0 + hn] = acc.astype(o_ref.dtype)


def _mm_pre(xs, ssq, mod_l, shift_k, w_stack, layer, m_rows, geom, act=None, rope=None, col_start=0, n_cols=None,
            out_dtype=BF16):
    d = xs.shape[1]
    k = w_stack.shape[1]
    n = n_cols or w_stack.shape[2]
    bm, bn = _mm_blocks(m_rows, k, n)
    assert col_start % bn == 0
    cb0 = col_start // bn
    kern = functools.partial(
        _mm_pre_kernel, bm=bm, d_norm=d, shift_k=shift_k, act=act,
        rope_row_blocks=(rope[2] // bm if rope else 0), rope_col_blocks=(rope[3] // bn if rope else 0), **geom)
    in_specs = [
        pl.BlockSpec((bm, d), lambda i, j: (i, 0)),
        pl.BlockSpec((bm, LANES), lambda i, j: (i, 0)),
        pl.BlockSpec(mod_l.shape, lambda i, j: (0, 0)),
        pl.BlockSpec((None, k, bn), lambda i, j: (layer, 0, cb0 + j)),
    ]
    args = [xs, ssq, mod_l, w_stack]
    if rope:
        seq_blocks = rope[0].shape[0] // bm
        hd = rope[0].shape[1]
        in_specs += [pl.BlockSpec((bm, hd), lambda i, j: (i % seq_blocks, 0))] * 2
        args += [rope[0], rope[1]]
    return pl.pallas_call(
        kern,
        grid=(m_rows // bm, n // bn),
        in_specs=in_specs,
        out_specs=pl.BlockSpec((bm, bn), lambda i, j: (i, j)),
        out_shape=jax.ShapeDtypeStruct((m_rows, n), out_dtype),
        scratch_shapes=[pltpu.VMEM((n // bn, 8, bn), F32)],
        compiler_params=_params(2, VMEM_LIMIT_F32_WEIGHTS),
        name="mm_pre" + ("_" + act if act else "") + ("_rope" if rope else ""),
    )(*args)


def _mm_res_kernel(*refs, bm, n_k, row_off, emit_next, t_lat, seq, n_batch):
    x_ref, w_ref, res_ref, gate_ref = refs[:4]
    pos = 4
    if emit_next:
        nw_ref, nscale_ref = refs[pos:pos + 2]
        pos += 4
    o_ref = refs[pos]
    pos += 1
    if emit_next:
        xs_ref, ssq_ref = refs[pos:pos + 2]
        pos += 2
    acc_ref = refs[pos] if n_k > 1 else None
    i, j, kk = pl.program_id(0), pl.program_id(1), pl.program_id(2)
    midx = _mod_row(row_off + i * bm, t_lat, seq, n_batch)
    part = _dot(x_ref[...], w_ref[...])

    def finish(acc):
        hb = res_ref[...] + gate_ref[pl.ds(midx, 1), :] * acc
        o_ref[...] = hb
        if emit_next:
            _next_norm_outputs(hb, j, midx, nw_ref, nscale_ref, xs_ref, ssq_ref)

    if n_k == 1:
        finish(part)
        return

    @pl.when(kk == 0)
    def _():
        acc_ref[...] = part

    @pl.when((kk > 0) & (kk < n_k - 1))
    def _():
        acc_ref[...] = acc_ref[...] + part

    @pl.when(kk == n_k - 1)
    def _():
        finish(acc_ref[...] + part)


def _mm_res(x, w, bufs, mod_l, gate_k, m_rows, geom, row_off=0, next_norm=None):
    h, xs, ssq = bufs
    k, n = w.shape
    n_k = 2 if k >= 8192 else 1
    kb = k // n_k
    bm = 1024 if m_rows % 1024 == 0 else 256
    bn = 512 if k >= 4096 else 1024
    assert row_off % bm == 0 and n % bn == 0
    blk_off = row_off // bm
    nb = n // bn
    emit_next = next_norm is not None
    in_specs = [
        pl.BlockSpec((bm, kb), lambda i, j, kk: (i, kk)),
        pl.BlockSpec((kb, bn), lambda i, j, kk: (kk, j)),
        pl.BlockSpec((bm, bn), lambda i, j, kk: (blk_off + i, j)),
        pl.BlockSpec((8, bn), lambda i, j, kk: (0, gate_k * nb + j)),
    ]
    args = [x, w, h, mod_l]
    out_specs = [pl.BlockSpec((bm, bn), lambda i, j, kk: (blk_off + i, j))]
    out_shape = [jax.ShapeDtypeStruct(h.shape, F32)]
    aliases = {2: 0}
    if emit_next:
        nw, mod_next, scale_slot = next_norm
        in_specs += [
            pl.BlockSpec((1, bn), lambda i, j, kk: (0, j)),
            pl.BlockSpec((8, bn), lambda i, j, kk: (0, scale_slot * nb + j)),
            pl.BlockSpec(memory_space=pl.ANY),
            pl.BlockSpec(memory_space=pl.ANY),
        ]
        args += [nw.reshape(1, n), mod_next, xs, ssq]
        out_specs += [
            pl.BlockSpec((bm, bn), lambda i, j, kk: (blk_off + i, j)),
            pl.BlockSpec((bm, LANES), lambda i, j, kk: (blk_off + i, 0)),
        ]
        out_shape += [jax.ShapeDtypeStruct(xs.shape, BF16), jax.ShapeDtypeStruct(ssq.shape, F32)]
        aliases.update({6: 1, 7: 2})
    outs = pl.pallas_call(
        functools.partial(_mm_res_kernel, bm=bm, n_k=n_k, row_off=row_off, emit_next=emit_next, **geom),
        grid=(m_rows // bm, nb, n_k),
        in_specs=in_specs,
        out_specs=out_specs,
        out_shape=out_shape,
        scratch_shapes=[pltpu.VMEM((bm, bn), F32)] if n_k > 1 else [],
        input_output_aliases=aliases,
        compiler_params=_params(3),
        name="mm_res",
    )(*args)
    return (outs[0], outs[1], outs[2]) if emit_next else (outs[0], xs, ssq)


def _na_pair_layout(rows):
    gq, gk = NA_GROUP_ROWS, NA_KEY_ROWS
    pairs = {}
    idx = np.zeros((3, gq, gk // 2), np.int32)
    for typ, (r0, ks) in enumerate(((0, 0), (gq, 0), (rows - gq, rows - gk))):
        for i in range(gq):
            r = r0 + i
            rs = min(max(r - NA_WIN_ROWS // 2, 0), rows - NA_WIN_ROWS)
            dr = [kr - r + NA_WIN_ROWS - 1 if rs <= kr < rs + NA_WIN_ROWS else -1 for kr in range(ks, ks + gk)]
            for p in range(gk // 2):
                idx[typ, i, p] = pairs.setdefault((dr[2 * p], dr[2 * p + 1]), len(pairs))
    return list(pairs), idx


def _na_pair_table(rpb, pairs, out_scale):
    heads = rpb.shape[0]
    col = np.arange(GRID_W)
    col_start = np.clip(col - NA_WIN_COLS // 2, 0, GRID_W - NA_WIN_COLS)
    col_mask = (col[None, :] >= col_start[:, None]) & (col[None, :] < col_start[:, None] + NA_WIN_COLS)
    dc_idx = np.clip(col[None, :] - col[:, None], -(NA_WIN_COLS - 1), NA_WIN_COLS - 1) + NA_WIN_COLS - 1
    onehot = (dc_idx[None] == np.arange(2 * NA_WIN_COLS - 1)[:, None, None]).astype(np.float32)
    per_dr = jnp.einsum("hrd,dqk->hrqk", rpb.astype(F32), onehot, precision=lax.Precision.HIGHEST)
    per_dr = jnp.where(col_mask[None, None], per_dr * out_scale, MASK_NEG)
    masked = jnp.full((heads, GRID_W, GRID_W), MASK_NEG, F32)
    block = lambda dr: masked if dr < 0 else per_dr[:, dr]
    return jnp.stack([jnp.concatenate([block(a), block(b)], axis=-1) for a, b in pairs], axis=1)


def _softmax_pv(scores, values, out_dtype):
    m = functools.reduce(jnp.maximum, [jnp.max(t, axis=1, keepdims=True) for t in scores])
    probs = [jnp.exp2(t - m) for t in scores]
    denom = functools.reduce(jnp.add, [jnp.sum(p, axis=1, keepdims=True) for p in probs])
    o = None
    for p, v in zip(probs, values):
        part = _dot(p.astype(BF16), v)
        o = part if o is None else o + part
    return (o * (1.0 / denom)).astype(out_dtype)


def _na_kernel(idx_ref, q_ref, k0_ref, k1_ref, k2_ref, v0_ref, v1_ref, v2_ref, kc_ref, vc_ref, pair_ref, o_ref,
               *, exp2_scale, n_groups):
    dh = NA_HEAD_DIM
    gq, n_pairs = NA_GROUP_ROWS, NA_KEY_ROWS // 2
    g = pl.program_id(1)
    typ = jnp.where(g == 0, 0, jnp.where(g == n_groups - 1, 2, 1))
    for hh in range(NA_HEADS_PER_STEP):
        sl = slice(hh * dh, (hh + 1) * dh)
        q = q_ref[:, sl]
        bias = jnp.concatenate(
            [jnp.concatenate([pair_ref[hh, idx_ref[(typ * gq + i) * n_pairs + p]] for p in range(n_pairs)], axis=1)
             for i in range(gq)], axis=0)
        s_loc = jnp.concatenate([_dot_nt(q, kr[:, sl]) for kr in (k0_ref, k1_ref, k2_ref)], axis=1)
        t_loc = s_loc * exp2_scale + bias
        t_ctx = _dot_nt(q, kc_ref[:, sl]) * exp2_scale
        kb = k0_ref.shape[0]
        scores = [t_loc[:, j * kb:(j + 1) * kb] for j in range(3)] + [t_ctx]
        values = [vr[:, sl] for vr in (v0_ref, v1_ref, v2_ref, vc_ref)]
        o_ref[:, sl] = _softmax_pv(scores, values, o_ref.dtype)


def _na_ctx_kernel(q_ref, kc_ref, vc_ref, o_ref, *, exp2_scale):
    dh = NA_HEAD_DIM
    for hh in range(NA_HEADS_PER_STEP):
        sl = slice(hh * dh, (hh + 1) * dh)
        t_ctx = _dot_nt(q_ref[:, sl], kc_ref[:, sl]) * exp2_scale
        o_ref[:, sl] = _softmax_pv([t_ctx], [vc_ref[:, sl]], o_ref.dtype)


def _na_attention(qkv, rpb, n_batch, seq, lc, ctx_out):
    hd = NA_HEADS * NA_HEAD_DIM
    gtok = NA_GROUP_ROWS * GRID_W
    ng = seq // gtok
    assert lc == gtok and ng >= 4
    hw = NA_HEADS_PER_STEP * NA_HEAD_DIM
    nhq = NA_HEADS // NA_HEADS_PER_STEP
    t_lat = n_batch * seq
    ctx_blk0 = n_batch * ng
    exp2_scale = NA_HEAD_DIM ** -0.5 * math.log2(math.e)
    pairs, idx = _na_pair_layout(seq // GRID_W)
    pair_tab = _na_pair_table(rpb, pairs, math.log2(math.e))

    def k_row(g, b, j):
        return b * ng + jnp.clip(g - 1, 0, ng - 3) + j

    blk = (gtok, hw)
    in_specs = [pl.BlockSpec(memory_space=pltpu.SMEM), pl.BlockSpec(blk, lambda h, g, b: (b * ng + g, h))]
    for col0 in (nhq, 2 * nhq):
        for j in range(3):
            in_specs.append(pl.BlockSpec(blk, lambda h, g, b, j=j, col0=col0: (k_row(g, b, j), col0 + h)))
    in_specs.append(pl.BlockSpec(blk, lambda h, g, b: (ctx_blk0 + b, nhq + h)))
    in_specs.append(pl.BlockSpec(blk, lambda h, g, b: (ctx_blk0 + b, 2 * nhq + h)))
    in_specs.append(pl.BlockSpec((NA_HEADS_PER_STEP, len(pairs), GRID_W, 2 * GRID_W), lambda h, g, b: (h, 0, 0, 0)))
    y_lat = pl.pallas_call(
        functools.partial(_na_kernel, exp2_scale=exp2_scale, n_groups=ng),
        grid=(nhq, ng, n_batch),
        in_specs=in_specs,
        out_specs=pl.BlockSpec(blk, lambda h, g, b: (b * ng + g, h)),
        out_shape=jax.ShapeDtypeStruct((t_lat, hd), BF16),
        compiler_params=_params(3),
        name="na_attention",
    )(jnp.asarray(idx.reshape(-1)), *([qkv] * 9), pair_tab)
    if not ctx_out:
        return y_lat, None
    y_ctx = pl.pallas_call(
        functools.partial(_na_ctx_kernel, exp2_scale=exp2_scale),
        grid=(nhq, n_batch),
        in_specs=[
            pl.BlockSpec(blk, lambda h, b: (ctx_blk0 + b, h)),
            pl.BlockSpec(blk, lambda h, b: (ctx_blk0 + b, nhq + h)),
            pl.BlockSpec(blk, lambda h, b: (ctx_blk0 + b, 2 * nhq + h)),
        ],
        out_specs=pl.BlockSpec(blk, lambda h, b: (b, h)),
        out_shape=jax.ShapeDtypeStruct((n_batch * lc, hd), BF16),
        compiler_params=_params(2),
        name="na_attention_ctx",
    )(qkv, qkv, qkv)
    return y_lat, y_ctx


def _rope_tables(seq, width):
    t = jnp.arange(seq)
    pos = jnp.stack([t // GRID_W, t % GRID_W], axis=-1).astype(F32)
    n_freq = DA_HEAD_DIM // 4
    inv_freq = ROPE_BASE ** (-jnp.arange(n_freq, dtype=F32) / n_freq)
    ang = pos[:, :, None] * inv_freq
    cos = jnp.broadcast_to(jnp.cos(ang)[:, :, None, :], (seq, 2, 2, n_freq)).reshape(seq, DA_HEAD_DIM)
    sign = jnp.array([-1.0, 1.0], F32)[None, None, :, None]
    sin = (jnp.broadcast_to(jnp.sin(ang)[:, :, None, :], (seq, 2, 2, n_freq)) * sign).reshape(seq, DA_HEAD_DIM)
    reps = width // DA_HEAD_DIM
    return jnp.tile(cos, (1, reps)), jnp.tile(sin, (1, reps))


def _da_kernel(lam_ref, q_ref, *refs, exp2_scale, out_scale, n_sub, with_latent):
    n_kv = 2 if with_latent else 1
    k_refs, v_refs = refs[:n_kv], refs[n_kv:2 * n_kv]
    sw_ref, o_ref = refs[2 * n_kv], refs[2 * n_kv + 1]
    dh = DA_HEAD_DIM
    lam = lam_ref[0]
    rows = q_ref.shape[0] // n_sub
    for sub in range(n_sub):
        rs = slice(sub * rows, (sub + 1) * rows)
        probs, sums = [], []
        for comp in range(2):
            sl = slice(comp * dh, (comp + 1) * dh)
            q = q_ref[rs, sl]
            s = [_dot_nt(q, k_ref[:, sl]) for k_ref in k_refs]
            m = functools.reduce(jnp.maximum, [jnp.max(x, axis=1, keepdims=True) for x in s])
            p = [jnp.exp2((x - m) * exp2_scale) for x in s]
            probs.append(p)
            sums.append(functools.reduce(jnp.add, [jnp.sum(x, axis=1, keepdims=True) for x in p]))
        ratio = lam * sums[0] / sums[1]
        o = None
        for p1, p2, v_ref in zip(probs[0], probs[1], v_refs):
            part = _dot((p1 - ratio * p2).astype(BF16), v_ref[...])
            o = part if o is None else o + part
        o = o * (1.0 / sums[0])
        y = o * lax.rsqrt(jnp.mean(o * o, axis=-1, keepdims=True) + DA_SUBLN_EPS) * sw_ref[...]
        o_ref[rs, :] = (y * out_scale).astype(o_ref.dtype)


def _diff_attention(qkv, lam_full, subln_w, lambda_init, n_batch, seq, lc, ctx_out):
    hw = 2 * DA_HEAD_DIM
    bq, n_sub = DA_Q_BLOCK, DA_Q_SUBBLOCKS
    nq = seq // bq
    t_lat = n_batch * seq
    ctx_blk0 = t_lat // lc
    lam1 = lam_full.reshape(1).astype(F32)
    sw = subln_w.reshape(1, hw)
    common = dict(exp2_scale=DA_HEAD_DIM ** -0.5 * math.log2(math.e), out_scale=1.0 - lambda_init)
    smem = pl.BlockSpec(memory_space=pltpu.SMEM)
    y_lat = pl.pallas_call(
        functools.partial(_da_kernel, n_sub=n_sub, with_latent=True, **common),
        grid=(n_batch, DA_HEADS, nq),
        in_specs=[
            smem,
            pl.BlockSpec((bq, hw), lambda b, h, i: (b * nq + i, h)),
            pl.BlockSpec((seq, hw), lambda b, h, i: (b, DA_HEADS + h)),
            pl.BlockSpec((lc, hw), lambda b, h, i: (ctx_blk0 + b, DA_HEADS + h)),
            pl.BlockSpec((seq, hw), lambda b, h, i: (b, 2 * DA_HEADS + h)),
            pl.BlockSpec((lc, hw), lambda b, h, i: (ctx_blk0 + b, 2 * DA_HEADS + h)),
            pl.BlockSpec((1, hw), lambda b, h, i: (0, 0)),
        ],
        out_specs=pl.BlockSpec((bq, hw), lambda b, h, i: (b * nq + i, h)),
        out_shape=jax.ShapeDtypeStruct((t_lat, DA_HEADS * hw), BF16),
        compiler_params=_params(3),
        name="diff_attention",
    )(lam1, qkv, qkv, qkv, qkv, qkv, sw)
    if not ctx_out:
        return y_lat, None
    y_ctx = pl.pallas_call(
        functools.partial(_da_kernel, n_sub=1, with_latent=False, **common),
        grid=(n_batch, DA_HEADS),
        in_specs=[
            smem,
            pl.BlockSpec((lc, hw), lambda b, h: (ctx_blk0 + b, h)),
            pl.BlockSpec((lc, hw), lambda b, h: (ctx_blk0 + b, DA_HEADS + h)),
            pl.BlockSpec((lc, hw), lambda b, h: (ctx_blk0 + b, 2 * DA_HEADS + h)),
            pl.BlockSpec((1, hw), lambda b, h: (0, 0)),
        ],
        out_specs=pl.BlockSpec((lc, hw), lambda b, h: (b, h)),
        out_shape=jax.ShapeDtypeStruct((n_batch * lc, DA_HEADS * hw), BF16),
        compiler_params=_params(2),
        name="diff_attention_ctx",
    )(lam1, qkv, qkv, qkv, sw)
    return y_lat, y_ctx


def _conv_kernel(prev_ref, cur_ref, next_ref, w_ref, b_ref, o_ref, *, rb, lat_blocks, lat_blocks_per_seq, ctx_blocks_per_seq):
    i = pl.program_id(0)
    is_lat = i < lat_blocks
    pos = jnp.where(is_lat, i % lat_blocks_per_seq, (i - lat_blocks) % ctx_blocks_per_seq)
    per_seq = jnp.where(is_lat, lat_blocks_per_seq, ctx_blocks_per_seq)
    halo = prev_ref.shape[0]
    prev = jnp.where(pos == 0, 0.0, prev_ref[...].astype(F32))
    nxt = jnp.where(pos == per_seq - 1, 0.0, next_ref[...].astype(F32))
    cat = jnp.concatenate([prev, cur_ref[...].astype(F32), nxt], axis=0)
    n = rb + 2 * halo
    acc = b_ref[...]
    for k in range(SSM_CONV):
        shifted = cat if k == SSM_CONV // 2 else pltpu.roll(cat, (SSM_CONV // 2 - k) % n, 0)
        acc = acc + w_ref[k:k + 1, :] * shifted[halo:halo + rb]
    o_ref[...] = (acc * _sigmoid(acc)).astype(o_ref.dtype)


def _ssm_conv(proj, conv_w, conv_b, n_batch, seq, lc):
    t = proj.shape[0]
    rb, ct, halo = 256, 2048, SUBLANES_BF16
    assert seq % rb == 0 and lc % rb == 0 and SSM_D_INNER % ct == 0 and SSM_CONV_DIM % ct == 0
    col0 = SSM_D_INNER // ct
    hb = rb // halo
    n_halo_blocks = t // halo
    kern = functools.partial(_conv_kernel, rb=rb, lat_blocks=n_batch * seq // rb,
                             lat_blocks_per_seq=seq // rb, ctx_blocks_per_seq=lc // rb)
    return pl.pallas_call(
        kern,
        grid=(t // rb, SSM_CONV_DIM // ct),
        in_specs=[
            pl.BlockSpec((halo, ct), lambda i, j: (jnp.maximum(i * hb - 1, 0), col0 + j)),
            pl.BlockSpec((rb, ct), lambda i, j: (i, col0 + j)),
            pl.BlockSpec((halo, ct), lambda i, j: (jnp.minimum((i + 1) * hb, n_halo_blocks - 1), col0 + j)),
            pl.BlockSpec((SSM_CONV, ct), lambda i, j: (0, j)),
            pl.BlockSpec((1, ct), lambda i, j: (0, j)),
        ],
        out_specs=pl.BlockSpec((rb, ct), lambda i, j: (i, j)),
        out_shape=jax.ShapeDtypeStruct((t, SSM_CONV_DIM), BF16),
        compiler_params=_params(2),
        name="ssm_conv",
    )(proj, proj, proj, conv_w, conv_b.reshape(1, SSM_CONV_DIM))


def _ssd_kernel(xbc_ref, dt_ref, dtb_ref, alog_ref, tri_ref, half_ref, y_ref, state_ref):
    q = SSM_CHUNK
    nh = SSM_HEADS
    p2 = 2 * SSM_HEAD_DIM
    gw = SSM_GROUP_WIDTH
    log2e = math.log2(math.e)
    direction = pl.program_id(1)
    fwd = direction == 0

    @pl.when(pl.program_id(2) == 0)
    def _():
        state_ref[...] = jnp.zeros_like(state_ref)

    pre = dt_ref[...] + dtb_ref[...]
    dt_all = jnp.maximum(pre, 0.0) + jnp.log1p(jnp.exp(-jnp.abs(pre)))
    a_all = dt_all * (jnp.exp(alog_ref[...]) * -log2e)
    tri = tri_ref[0]
    acum_all = sum(_dot(tri, piece) for piece in _split3(a_all))
    acum = jnp.where(fwd, acum_all, pltpu.roll(acum_all, nh, 1))
    acum_t = acum.T
    dt_t = jnp.where(fwd, dt_all, pltpu.roll(dt_all, nh, 1)).T
    src_t = acum_t - jnp.log(dt_t) * log2e
    lane = lax.broadcasted_iota(jnp.int32, (q, q), 1)
    last = jnp.where(fwd, q - 1, 0)
    total = jnp.sum(jnp.where(lane == last, acum_t, 0.0), axis=1, keepdims=True)
    to_end_t = dt_t * jnp.exp2(total - acum_t)
    chunk_decay = jnp.broadcast_to(jnp.exp2(total), (q, q))
    seen = tri.astype(F32) > 0.5
    first_head = lane < SSM_HEAD_DIM
    c_off = SSM_D_INNER + SSM_GROUPS * SSM_STATE

    for g in range(SSM_GROUPS):
        b_g = xbc_ref[:, SSM_D_INNER + g * SSM_STATE:SSM_D_INNER + (g + 1) * SSM_STATE]
        c_g = xbc_ref[:, c_off + g * SSM_STATE:c_off + (g + 1) * SSM_STATE]
        cb = _dot_nt(c_g, b_g)
        b_t = b_g.astype(F32).T
        y_carried = _dot(c_g, state_ref[g].astype(BF16))
        for pr in range(SSM_HEADS_PER_GROUP // 2):
            ps = slice(pr * p2, (pr + 1) * p2)
            cols = slice(g * gw + pr * p2, g * gw + (pr + 1) * p2)
            x2 = xbc_ref[:, cols]
            x_bd = jnp.concatenate([x2 * half_ref[0], x2 * half_ref[1]], axis=0)
            intra, to_state, carry_scale, decay = [], [], [], []
            for r in (2 * pr, 2 * pr + 1):
                hh = g * SSM_HEADS_PER_GROUP + r
                col = jnp.sum(jnp.where(lane == hh, acum, 0.0), axis=1, keepdims=True)
                intra.append((cb * jnp.where(seen, jnp.exp2(col - src_t[hh:hh + 1, :]), 0.0)).astype(BF16))
                to_state.append((b_t * to_end_t[hh:hh + 1, :]).astype(BF16))
                carry_scale.append(jnp.exp2(col))
                decay.append(chunk_decay[hh:hh + 1, :])
            y_ref[0, :, cols] = (_dot(jnp.concatenate(intra, axis=1), x_bd)
                                 + jnp.where(first_head, carry_scale[0], carry_scale[1]) * y_carried[:, ps])
            state_ref[g, :, ps] = (state_ref[g, :, ps] * jnp.where(first_head[:1], decay[0], decay[1])
                                   + _dot(jnp.concatenate(to_state, axis=1), x_bd))


def _ssd_scan(xbc, dt_raw, dt_bias, a_log, n_batch, seq, lc):
    t = xbc.shape[0]
    q = SSM_CHUNK
    n_lat, n_ctx = seq // q, lc // q
    steps = n_lat + n_ctx
    ctx_blk0 = n_batch * n_lat

    def blk(b, d, s):
        ctx_chunk = jnp.where(d == 0, s, n_ctx - 1 - s)
        lat_chunk = jnp.where(d == 0, s - n_ctx, steps - 1 - s)
        return jnp.where(s < n_ctx, ctx_blk0 + b * n_ctx + ctx_chunk, b * n_lat + lat_chunk)

    idx = np.arange(q)
    tri = np.stack([idx[None, :] <= idx[:, None], idx[None, :] >= idx[:, None]]).astype(np.float32)
    first = np.broadcast_to(np.arange(2 * SSM_HEAD_DIM)[None, :] < SSM_HEAD_DIM, (q, 2 * SSM_HEAD_DIM))
    half = np.stack([first, ~first]).astype(np.float32)
    return pl.pallas_call(
        _ssd_kernel,
        grid=(n_batch, 2, steps),
        in_specs=[
            pl.BlockSpec((q, SSM_CONV_DIM), lambda b, d, s: (blk(b, d, s), 0)),
            pl.BlockSpec((q, 2 * SSM_HEADS), lambda b, d, s: (blk(b, d, s), 0)),
            pl.BlockSpec((1, 2 * SSM_HEADS), lambda b, d, s: (0, 0)),
            pl.BlockSpec((1, 2 * SSM_HEADS), lambda b, d, s: (0, 0)),
            pl.BlockSpec((1, q, q), lambda b, d, s: (d, 0, 0)),
            pl.BlockSpec((2, q, 2 * SSM_HEAD_DIM), lambda b, d, s: (0, 0, 0)),
        ],
        out_specs=pl.BlockSpec((1, q, SSM_D_INNER), lambda b, d, s: (d, blk(b, d, s), 0)),
        out_shape=jax.ShapeDtypeStruct((2, t, SSM_D_INNER), F32),
        scratch_shapes=[pltpu.VMEM((SSM_GROUPS, SSM_STATE, SSM_GROUP_WIDTH), F32)],
        compiler_params=_params(3),
        name="ssd_scan",
    )(xbc, dt_raw, dt_bias.reshape(1, -1), a_log.reshape(1, -1), jnp.asarray(tri, BF16), jnp.asarray(half, BF16))


def _ssm_finish_kernel(yf_ref, yb_ref, x_ref, z_ref, dsk_ref, nw_ref, o_ref):
    gw = SSM_GROUP_WIDTH
    z = z_ref[...].astype(F32)
    y = (yf_ref[0] + yb_ref[0] + x_ref[...].astype(F32) * dsk_ref[...]) * (z * _sigmoid(z))
    for g in range(SSM_GROUPS):
        cs = slice(g * gw, (g + 1) * gw)
        yg = y[:, cs]
        o_ref[:, cs] = (yg * lax.rsqrt(jnp.mean(yg * yg, axis=-1, keepdims=True) + EPS) * nw_ref[:, cs]).astype(o_ref.dtype)


def _ssm_finish(y2, xbc, proj, d_skip, norm_w):
    t = xbc.shape[0]
    di = SSM_D_INNER
    rb = 128
    dsk = jnp.repeat(d_skip.astype(F32), SSM_HEAD_DIM).reshape(1, di)
    return pl.pallas_call(
        _ssm_finish_kernel,
        grid=(t // rb,),
        in_specs=[
            pl.BlockSpec((1, rb, di), lambda i: (0, i, 0)),
            pl.BlockSpec((1, rb, di), lambda i: (1, i, 0)),
            pl.BlockSpec((rb, di), lambda i: (i, 0)),
            pl.BlockSpec((rb, di), lambda i: (i, 0)),
            pl.BlockSpec((1, di), lambda i: (0, 0)),
            pl.BlockSpec((1, di), lambda i: (0, 0)),
        ],
        out_specs=pl.BlockSpec((rb, di), lambda i: (i, 0)),
        out_shape=jax.ShapeDtypeStruct((t, di), BF16),
        compiler_params=_params(1),
        name="ssm_finish",
    )(y2, y2, xbc, proj, dsk, norm_w.reshape(1, di))


def _mamba2(xs, ssq, mod_l, geom, w_in_stack, layer, conv_w, conv_b, dt_bias, a_log, d_skip, norm_w, n_batch, seq, lc):
    t = xs.shape[0]
    n_main = SSM_D_INNER + SSM_CONV_DIM
    proj = _mm_pre(xs, ssq, mod_l, 0, w_in_stack, layer, t, geom, n_cols=n_main)
    dt_raw = _mm_pre(xs, ssq, mod_l, 0, w_in_stack, layer, t, geom, col_start=n_main, n_cols=2 * SSM_HEADS,
                     out_dtype=F32)
    xbc = _ssm_conv(proj, conv_w, conv_b, n_batch, seq, lc)
    y2 = _ssd_scan(xbc, dt_raw, dt_bias, a_log, n_batch, seq, lc)
    return _ssm_finish(y2, xbc, proj, d_skip, norm_w)


def kernel(x, c, ctx, c_ctx, ada_w, ada_b, norm_mix_w, norm_mlp_w, mlp_w1, mlp_w2, na_w_qkv, na_w_o, na_rpb,
           ssm_w_in, ssm_conv_w, ssm_conv_b, ssm_dt_bias, ssm_a_log, ssm_d, ssm_norm_w, ssm_w_out,
           da_w_qkv, da_w_o, da_lambda, da_subln_w, final_norm_w):
    n_batch, seq, d = x.shape
    lc = ctx.shape[1]
    t_lat, t_ctx = n_batch * seq, n_batch * lc
    t = t_lat + t_ctx
    geom = dict(t_lat=t_lat, seq=seq, n_batch=n_batch)
    assert d == D_MODEL and seq % GRID_W == 0 and n_batch < 8

    cond = jnp.concatenate([c, c_ctx[None, :], jnp.zeros((8 - n_batch - 1, d), F32)], axis=0)
    mod = _adaln_all(cond, ada_w, ada_b)
    bufs = _prep(x.reshape(t_lat, d), ctx.reshape(t_ctx, d), norm_mix_w[0], mod[0], geom)

    for i in range(DEPTH):
        last = i == DEPTH - 1
        mixer, j = i % N_MIXERS, i // N_MIXERS
        y_ctx = None
        mod_l = mod[i]
        rows_out = t_lat if last else t
        _, xs, ssq = bufs
        if mixer == 0:
            qkv = _mm_pre(xs, ssq, mod_l, 0, na_w_qkv, j, t, geom)
            y, y_ctx = _na_attention(qkv, na_rpb[j], n_batch, seq, lc, not last)
            w_o = na_w_o[j]
        elif mixer == 1:
            y = _mamba2(xs, ssq, mod_l, geom, ssm_w_in, j, ssm_conv_w[j], ssm_conv_b[j], ssm_dt_bias[j],
                        ssm_a_log[j], ssm_d[j], ssm_norm_w[j], n_batch, seq, lc)
            w_o = ssm_w_out[j]
        else:
            lambda_init = 0.8 - 0.6 * math.exp(-0.3 * i)
            qk_w = 2 * DA_HEADS * DA_HEAD_DIM
            cos_t, sin_t = _rope_tables(seq, DA_HEAD_DIM)
            qkv = _mm_pre(xs, ssq, mod_l, 0, da_w_qkv, j, t, geom, rope=(cos_t, sin_t, t_lat, 2 * qk_w))
            lam = da_lambda[j].astype(F32)
            lam_full = jnp.exp(jnp.sum(lam[0] * lam[1])) - jnp.exp(jnp.sum(lam[2] * lam[3])) + lambda_init
            y, y_ctx = _diff_attention(qkv, lam_full, da_subln_w[j], lambda_init, n_batch, seq, lc, not last)
            w_o = da_w_o[j]
        w_o = w_o.astype(BF16)
        mlp_norm = (norm_mlp_w[i], mod_l, 4)
        if y_ctx is None:
            bufs = _mm_res(y, w_o, bufs, mod_l, 2, rows_out, geom, next_norm=mlp_norm)
        else:
            bufs = _mm_res(y, w_o, bufs, mod_l, 2, t_lat, geom, next_norm=mlp_norm)
            bufs = _mm_res(y_ctx, w_o, bufs, mod_l, 2, t_ctx, geom, row_off=t_lat, next_norm=mlp_norm)
        hidden = _mm_pre(bufs[1], bufs[2], mod_l, 3, mlp_w1, i, rows_out, geom, act="relu2")
        mixer_norm = None if last else (norm_mix_w[i + 1], mod[i + 1], 1)
        bufs = _mm_res(hidden, mlp_w2[i].astype(BF16), bufs, mod_l, 5, rows_out, geom, next_norm=mixer_norm)

    return _final_norm(bufs[0], final_norm_w, t_lat).reshape(n_batch, seq, d)
```

```python
import functools
import math

import jax
import jax.numpy as jnp
import numpy as np
from jax import lax
from jax.experimental import pallas as pl
from jax.experimental.pallas import tpu as pltpu

F32 = jnp.float32
BF16 = jnp.bfloat16

D_MODEL = 2048
DEPTH = 4
GRID_W = 64
N_MIXERS = 3
EPS = 1e-6

NA_HEADS = 16
NA_HEAD_DIM = 128
NA_WIN_ROWS = 8
NA_WIN_COLS = 16
NA_GROUP_ROWS = 4
NA_KEY_ROWS = 3 * NA_GROUP_ROWS
NA_HEADS_PER_STEP = 4

SSM_D_INNER = 4096
SSM_HEAD_DIM = 64
SSM_HEADS = 64
SSM_GROUPS = 8
SSM_HEADS_PER_GROUP = 8
SSM_STATE = 128
SSM_CONV = 5
SSM_CHUNK = 128
SSM_CONV_DIM = SSM_D_INNER + 2 * SSM_GROUPS * SSM_STATE
SSM_GROUP_WIDTH = SSM_HEADS_PER_GROUP * SSM_HEAD_DIM

DA_HEADS = 8
DA_HEAD_DIM = 128
DA_SUBLN_EPS = 1e-5
DA_Q_BLOCK = 1024
DA_Q_SUBBLOCKS = 4
ROPE_BASE = 10000.0

MASK_NEG = -1e30
assert NA_HEAD_DIM == DA_HEAD_DIM
ATTN_EXP2_SCALE = NA_HEAD_DIM ** -0.5 * math.log2(math.e)

V7X_VMEM_BYTES = 64 * 1024 * 1024
VMEM_LIMIT = (V7X_VMEM_BYTES * 3) // 4
MM_INNER_COLS = 512
MM_RES_INNER_COLS = 256
VMEM_LIMIT_F32_WEIGHTS = (V7X_VMEM_BYTES * 29) // 32
SUBLANES_BF16 = 16
LANES = 128


def _params(n_grid_dims, vmem_limit=VMEM_LIMIT):
    return pltpu.CompilerParams(
        dimension_semantics=("arbitrary",) * n_grid_dims, vmem_limit_bytes=vmem_limit)


def _sigmoid(x):
    return 1.0 / (1.0 + jnp.exp(-x))


def _dot(a, b):
    return jnp.dot(a, b, preferred_element_type=F32)


def _dot_nt(a, b):
    return lax.dot_general(a, b, (((1,), (1,)), ((), ())), preferred_element_type=F32)


def _split3(v):
    hi = v.astype(BF16)
    r1 = v - hi.astype(F32)
    mid = r1.astype(BF16)
    lo = (r1 - mid.astype(F32)).astype(BF16)
    return hi, mid, lo


def _mod_row(row0, t_lat, seq, n_batch):
    return jnp.where(row0 < t_lat, row0 // seq, n_batch)


def _adaln_kernel(c_ref, w_ref, b_ref, o_ref):
    x = c_ref[...]
    x = (x * _sigmoid(x)).astype(BF16)
    o_ref[0] = _dot(x, w_ref[0].astype(BF16)) + b_ref[0]


def _adaln_all(cond, ada_w, ada_b):
    depth, d, n = ada_w.shape
    bn = 1024
    return pl.pallas_call(
        _adaln_kernel,
        grid=(depth, n // bn),
        in_specs=[
            pl.BlockSpec((8, d), lambda l, j: (0, 0)),
            pl.BlockSpec((1, d, bn), lambda l, j: (l, 0, j)),
            pl.BlockSpec((1, 1, bn), lambda l, j: (l, 0, j)),
        ],
        out_specs=pl.BlockSpec((1, 8, bn), lambda l, j: (l, 0, j)),
        out_shape=jax.ShapeDtypeStruct((depth, 8, n), F32),
        compiler_params=_params(2),
        name="adaln",
    )(cond, ada_w, ada_b.reshape(depth, 1, n))


def _final_norm_kernel(h_ref, nw_ref, o_ref):
    x = h_ref[...]
    o_ref[...] = x * lax.rsqrt(jnp.mean(x * x, axis=-1, keepdims=True) + EPS) * nw_ref[...]


def _final_norm(h, norm_w, m_rows):
    d = h.shape[1]
    bm = 256
    return pl.pallas_call(
        _final_norm_kernel,
        grid=(m_rows // bm,),
        in_specs=[pl.BlockSpec((bm, d), lambda i: (i, 0)), pl.BlockSpec((1, d), lambda i: (0, 0))],
        out_specs=pl.BlockSpec((bm, d), lambda i: (i, 0)),
        out_shape=jax.ShapeDtypeStruct((m_rows, d), F32),
        compiler_params=_params(1),
        name="final_norm",
    )(h, norm_w.reshape(1, d))


def _mm_blocks(m_rows, k, n):
    bm = 1024 if m_rows % 1024 == 0 else 256
    bn = 1024 if n % 1024 == 0 else (512 if n % 512 == 0 else 128)
    if k >= 8192:
        bm, bn = min(bm, 512), min(bn, 512)
    elif k >= 4096:
        bn = min(bn, 512)
    return bm, bn


def _accumulate_over_columns(ssq_ref, part, j):
    @pl.when(j == 0)
    def _():
        ssq_ref[...] = part

    @pl.when(j > 0)
    def _():
        ssq_ref[...] = ssq_ref[...] + part


def _next_norm_outputs(hb, j, midx, nw_ref, nscale_ref, xs_ref, ssq_ref):
    g = nw_ref[...] * (1.0 + nscale_ref[pl.ds(midx, 1), :])
    xs_ref[...] = (hb * g).astype(xs_ref.dtype)
    sq = hb * hb
    part = functools.reduce(jnp.add, [sq[:, c:c + LANES] for c in range(0, sq.shape[1], LANES)])
    _accumulate_over_columns(ssq_ref, part, j)


def _prep_kernel(x_ref, c_ref, nw_ref, nscale_ref, h_ref, xs_ref, ssq_ref, *, bm, lat_blocks, t_lat, seq, n_batch):
    i, j = pl.program_id(0), pl.program_id(1)
    midx = _mod_row(i * bm, t_lat, seq, n_batch)

    def emit(src_ref):
        hb = src_ref[...]
        h_ref[...] = hb
        _next_norm_outputs(hb, j, midx, nw_ref, nscale_ref, xs_ref, ssq_ref)

    pl.when(i < lat_blocks)(lambda: emit(x_ref))
    pl.when(i >= lat_blocks)(lambda: emit(c_ref))


def _prep(x2d, c2d, norm_w, mod_next, geom):
    t_lat, d = x2d.shape
    t_ctx = c2d.shape[0]
    t = t_lat + t_ctx
    bm = 512 if t_ctx % 512 == 0 else 256
    bn = 1024
    lat_blocks = t_lat // bm
    sc0 = d // bn
    return pl.pallas_call(
        functools.partial(_prep_kernel, bm=bm, lat_blocks=lat_blocks, **geom),
        grid=(t // bm, d // bn),
        in_specs=[
            pl.BlockSpec((bm, bn), lambda i, j: (jnp.minimum(i, lat_blocks - 1), j)),
            pl.BlockSpec((bm, bn), lambda i, j: (jnp.maximum(i - lat_blocks, 0), j)),
            pl.BlockSpec((1, bn), lambda i, j: (0, j)),
            pl.BlockSpec((8, bn), lambda i, j: (0, sc0 + j)),
        ],
        out_specs=[
            pl.BlockSpec((bm, bn), lambda i, j: (i, j)),
            pl.BlockSpec((bm, bn), lambda i, j: (i, j)),
            pl.BlockSpec((bm, LANES), lambda i, j: (i, 0)),
        ],
        out_shape=[
            jax.ShapeDtypeStruct((t, d), F32),
            jax.ShapeDtypeStruct((t, d), BF16),
            jax.ShapeDtypeStruct((t, LANES), F32),
        ],
        compiler_params=_params(2),
        name="prep",
    )(x2d, c2d, norm_w.reshape(1, d), mod_next)


def _mm_pre_kernel(*refs, bm, d_norm, shift_k, act, rope_row_blocks, rope_col_blocks, lead_blocks, lead_scale,
                   t_lat, seq, n_batch):
    xs_ref, ssq_ref, mod_ref, w_ref = refs[:4]
    pos = 4
    if rope_col_blocks:
        cos_ref, sin_ref = refs[pos:pos + 2]
        pos += 2
    o_ref, sw_ref = refs[pos], refs[pos + 1]
    i, j = pl.program_id(0), pl.program_id(1)
    bn = o_ref.shape[1]
    hn = min(bn, MM_INNER_COLS)

    @pl.when(i == 0)
    def _():
        shift = mod_ref[:, shift_k * d_norm:(shift_k + 1) * d_norm].astype(BF16)
        for c0 in range(0, bn, hn):
            sw_ref[j, :, c0:c0 + hn] = _dot(shift, w_ref[:, c0:c0 + hn].astype(BF16))

    midx = _mod_row(i * bm, t_lat, seq, n_batch)
    r = lax.rsqrt(jnp.sum(ssq_ref[...], axis=1, keepdims=True) * (1.0 / d_norm) + EPS)
    lead = jnp.where(j < lead_blocks, lead_scale, 1.0) if lead_blocks else 1.0
    r = r * lead
    do_rope = (i < rope_row_blocks) & (j < rope_col_blocks)
    for c0 in range(0, bn, hn):
        acc = _dot(xs_ref[...], w_ref[:, c0:c0 + hn].astype(BF16))
        acc = acc * r + sw_ref[j, pl.ds(midx, 1), c0:c0 + hn] * lead
        if act == "relu2":
            acc = jnp.maximum(acc, 0.0)
            acc = acc * acc
        if not rope_col_blocks:
            o_ref[:, c0:c0 + hn] = acc.astype(o_ref.dtype)
            continue

        @pl.when(do_rope)
        def _(acc=acc, c0=c0):
            hd = cos_ref.shape[1]
            lane = lax.broadcasted_iota(jnp.int32, (bm, hd), 1)
            first = (lane % (hd // 2)) < hd // 4
            for c in range(hn // hd):
                xh = acc[:, c * hd:(c + 1) * hd]
                partner = jnp.where(first, pltpu.roll(xh, hd - hd // 4, 1), pltpu.roll(xh, hd // 4, 1))
                o_ref[:, c0 + c * hd:c0 + (c + 1) * hd] = (xh * cos_ref[...] + partner * sin_ref[...]).astype(o_ref.dtype)

        @pl.when(jnp.logical_not(do_rope))
        def _(acc=acc, c0=c0):
            o_ref[:, c0:c0 + hn] = acc.astype(o_ref.dtype)


def _mm_pre(xs, ssq, mod_l, shift_k, w_stack, layer, m_rows, geom, act=None, rope=None, col_start=0, n_cols=None,
            out_dtype=BF16, lead=None):
    d = xs.shape[1]
    k = w_stack.shape[1]
    n = n_cols or w_stack.shape[2]
    bm, bn = _mm_blocks(m_rows, k, n)
    assert col_start % bn == 0
    cb0 = col_start // bn
    kern = functools.partial(
        _mm_pre_kernel, bm=bm, d_norm=d, shift_k=shift_k, act=act,
        lead_blocks=(lead[0] // bn if lead else 0), lead_scale=(lead[1] if lead else 1.0),
        rope_row_blocks=(rope[2] // bm if rope else 0), rope_col_blocks=(rope[3] // bn if rope else 0), **geom)
    in_specs = [
        pl.BlockSpec((bm, d), lambda i, j: (i, 0)),
        pl.BlockSpec((bm, LANES), lambda i, j: (i, 0)),
        pl.BlockSpec(mod_l.shape, lambda i, j: (0, 0)),
        pl.BlockSpec((None, k, bn), lambda i, j: (layer, 0, cb0 + j)),
    ]
    args = [xs, ssq, mod_l, w_stack]
    if rope:
        seq_blocks = rope[0].shape[0] // bm
        hd = rope[0].shape[1]
        in_specs += [pl.BlockSpec((bm, hd), lambda i, j: (i % seq_blocks, 0))] * 2
        args += [rope[0], rope[1]]
    return pl.pallas_call(
        kern,
        grid=(m_rows // bm, n // bn),
        in_specs=in_specs,
        out_specs=pl.BlockSpec((bm, bn), lambda i, j: (i, j)),
        out_shape=jax.ShapeDtypeStruct((m_rows, n), out_dtype),
        scratch_shapes=[pltpu.VMEM((n // bn, 8, bn), F32)],
        compiler_params=_params(2, VMEM_LIMIT_F32_WEIGHTS),
        name="mm_pre" + ("_" + act if act else "") + ("_rope" if rope else ""),
    )(*args)


def _mm_res_kernel(*refs, bm, n_k, row_off, emit_next, t_lat, seq, n_batch):
    x_ref, w_ref, res_ref, gate_ref = refs[:4]
    pos = 4
    if emit_next:
        nw_ref, nscale_ref = refs[pos:pos + 2]
        pos += 4
    o_ref = refs[pos]
    pos += 1
    if emit_next:
        xs_ref, ssq_ref = refs[pos:pos + 2]
        pos += 2
    acc_ref = refs[pos] if n_k > 1 else None
    i, j, kk = pl.program_id(0), pl.program_id(1), pl.program_id(2)
    midx = _mod_row(row_off + i * bm, t_lat, seq, n_batch)
    bn = o_ref.shape[1]
    slices = [slice(c0, c0 + MM_RES_INNER_COLS) for c0 in range(0, bn, MM_RES_INNER_COLS)]
    parts = [_dot(x_ref[...], w_ref[:, cs]) for cs in slices]

    def finish(accs):
        sums = None
        for cs, acc in zip(slices, accs):
            hb = res_ref[:, cs] + gate_ref[pl.ds(midx, 1), cs] * acc
            o_ref[:, cs] = hb
            if emit_next:
                g = nw_ref[:, cs] * (1.0 + nscale_ref[pl.ds(midx, 1), cs])
                xs_ref[:, cs] = (hb * g).astype(xs_ref.dtype)
                sq = hb * hb
                for c in range(0, sq.shape[1], LANES):
                    sums = sq[:, c:c + LANES] if sums is None else sums + sq[:, c:c + LANES]
        if emit_next:
            _accumulate_over_columns(ssq_ref, sums, j)

    if n_k == 1:
        finish(parts)
        return

    @pl.when(kk == 0)
    def _():
        for cs, part in zip(slices, parts):
            acc_ref[:, cs] = part

    @pl.when((kk > 0) & (kk < n_k - 1))
    def _():
        for cs, part in zip(slices, parts):
            acc_ref[:, cs] = acc_ref[:, cs] + part

    @pl.when(kk == n_k - 1)
    def _():
        finish([acc_ref[:, cs] + part for cs, part in zip(slices, parts)])


def _mm_res(x, w_stack, layer, bufs, mod_l, gate_k, m_rows, geom, row_off=0, next_norm=None):
    h, xs, ssq = bufs
    k, n = w_stack.shape[1:]
    n_k = 2 if k >= 8192 else 1
    kb = k // n_k
    bm = 1024 if m_rows % 1024 == 0 else 256
    bn = 512 if k >= 4096 else 1024
    assert row_off % bm == 0 and n % bn == 0
    blk_off = row_off // bm
    nb = n // bn
    emit_next = next_norm is not None
    in_specs = [
        pl.BlockSpec((bm, kb), lambda i, j, kk: (i, kk)),
        pl.BlockSpec((None, kb, bn), lambda i, j, kk: (layer, kk, j)),
        pl.BlockSpec((bm, bn), lambda i, j, kk: (blk_off + i, j)),
        pl.BlockSpec((8, bn), lambda i, j, kk: (0, gate_k * nb + j)),
    ]
    args = [x, w_stack, h, mod_l]
    out_specs = [pl.BlockSpec((bm, bn), lambda i, j, kk: (blk_off + i, j))]
    out_shape = [jax.ShapeDtypeStruct(h.shape, F32)]
    aliases = {2: 0}
    if emit_next:
        nw, mod_next, scale_slot = next_norm
        in_specs += [
            pl.BlockSpec((1, bn), lambda i, j, kk: (0, j)),
            pl.BlockSpec((8, bn), lambda i, j, kk: (0, scale_slot * nb + j)),
            pl.BlockSpec(memory_space=pl.ANY),
            pl.BlockSpec(memory_space=pl.ANY),
        ]
        args += [nw.reshape(1, n), mod_next, xs, ssq]
        out_specs += [
            pl.BlockSpec((bm, bn), lambda i, j, kk: (blk_off + i, j)),
            pl.BlockSpec((bm, LANES), lambda i, j, kk: (blk_off + i, 0)),
        ]
        out_shape += [jax.ShapeDtypeStruct(xs.shape, BF16), jax.ShapeDtypeStruct(ssq.shape, F32)]
        aliases.update({6: 1, 7: 2})
    outs = pl.pallas_call(
        functools.partial(_mm_res_kernel, bm=bm, n_k=n_k, row_off=row_off, emit_next=emit_next, **geom),
        grid=(m_rows // bm, nb, n_k),
        in_specs=in_specs,
        out_specs=out_specs,
        out_shape=out_shape,
        scratch_shapes=[pltpu.VMEM((bm, bn), F32)] if n_k > 1 else [],
        input_output_aliases=aliases,
        compiler_params=_params(3),
        name="mm_res",
    )(*args)
    return (outs[0], outs[1], outs[2]) if emit_next else (outs[0], xs, ssq)


def _na_pair_layout(rows):
    gq, gk = NA_GROUP_ROWS, NA_KEY_ROWS
    pairs = {}
    idx = np.zeros((3, gq, gk // 2), np.int32)
    for typ, (r0, ks) in enumerate(((0, 0), (gq, 0), (rows - gq, rows - gk))):
        for i in range(gq):
            r = r0 + i
            rs = min(max(r - NA_WIN_ROWS // 2, 0), rows - NA_WIN_ROWS)
            dr = [kr - r + NA_WIN_ROWS - 1 if rs <= kr < rs + NA_WIN_ROWS else -1 for kr in range(ks, ks + gk)]
            for p in range(gk // 2):
                idx[typ, i, p] = pairs.setdefault((dr[2 * p], dr[2 * p + 1]), len(pairs))
    return list(pairs), idx


def _na_pair_table(rpb, pairs, out_scale):
    heads = rpb.shape[0]
    col = np.arange(GRID_W)
    col_start = np.clip(col - NA_WIN_COLS // 2, 0, GRID_W - NA_WIN_COLS)
    col_mask = (col[None, :] >= col_start[:, None]) & (col[None, :] < col_start[:, None] + NA_WIN_COLS)
    dc_idx = np.clip(col[None, :] - col[:, None], -(NA_WIN_COLS - 1), NA_WIN_COLS - 1) + NA_WIN_COLS - 1
    onehot = (dc_idx[None] == np.arange(2 * NA_WIN_COLS - 1)[:, None, None]).astype(np.float32)
    per_dr = jnp.einsum("hrd,dqk->hrqk", rpb.astype(F32), onehot, precision=lax.Precision.HIGHEST)
    per_dr = jnp.where(col_mask[None, None], per_dr * out_scale, MASK_NEG)
    masked = jnp.full((heads, GRID_W, GRID_W), MASK_NEG, F32)
    block = lambda dr: masked if dr < 0 else per_dr[:, dr]
    return jnp.stack([jnp.concatenate([block(a), block(b)], axis=-1) for a, b in pairs], axis=1)


def _softmax_pv(scores, values, out_dtype):
    m = functools.reduce(jnp.maximum, [jnp.max(t, axis=1, keepdims=True) for t in scores])
    probs = [jnp.exp2(t - m) for t in scores]
    denom = functools.reduce(jnp.add, [jnp.sum(p, axis=1, keepdims=True) for p in probs])
    o = None
    for p, v in zip(probs, values):
        part = _dot(p.astype(BF16), v)
        o = part if o is None else o + part
    return (o * (1.0 / denom)).astype(out_dtype)


def _na_kernel(idx_ref, q_ref, k0_ref, k1_ref, k2_ref, v0_ref, v1_ref, v2_ref, kc_ref, vc_ref, pair_ref, o_ref,
               *, n_groups):
    dh = NA_HEAD_DIM
    gq, n_pairs = NA_GROUP_ROWS, NA_KEY_ROWS // 2
    g = pl.program_id(1)
    typ = jnp.where(g == 0, 0, jnp.where(g == n_groups - 1, 2, 1))
    for hh in range(NA_HEADS_PER_STEP):
        sl = slice(hh * dh, (hh + 1) * dh)
        q = q_ref[:, sl]
        bias = jnp.concatenate(
            [jnp.concatenate([pair_ref[hh, idx_ref[(typ * gq + i) * n_pairs + p]] for p in range(n_pairs)], axis=1)
             for i in range(gq)], axis=0)
        s_loc = jnp.concatenate([_dot_nt(q, kr[:, sl]) for kr in (k0_ref, k1_ref, k2_ref)], axis=1)
        t_loc = s_loc + bias
        t_ctx = _dot_nt(q, kc_ref[:, sl])
        kb = k0_ref.shape[0]
        scores = [t_loc[:, j * kb:(j + 1) * kb] for j in range(3)] + [t_ctx]
        values = [vr[:, sl] for vr in (v0_ref, v1_ref, v2_ref, vc_ref)]
        o_ref[:, sl] = _softmax_pv(scores, values, o_ref.dtype)


def _na_ctx_kernel(q_ref, kc_ref, vc_ref, o_ref):
    dh = NA_HEAD_DIM
    for hh in range(NA_HEADS_PER_STEP):
        sl = slice(hh * dh, (hh + 1) * dh)
        t_ctx = _dot_nt(q_ref[:, sl], kc_ref[:, sl])
        o_ref[:, sl] = _softmax_pv([t_ctx], [vc_ref[:, sl]], o_ref.dtype)


def _na_attention(qkv, rpb, n_batch, seq, lc, ctx_out):
    hd = NA_HEADS * NA_HEAD_DIM
    gtok = NA_GROUP_ROWS * GRID_W
    ng = seq // gtok
    assert lc == gtok and ng >= 4
    hw = NA_HEADS_PER_STEP * NA_HEAD_DIM
    nhq = NA_HEADS // NA_HEADS_PER_STEP
    t_lat = n_batch * seq
    ctx_blk0 = n_batch * ng
    pairs, idx = _na_pair_layout(seq // GRID_W)
    pair_tab = _na_pair_table(rpb, pairs, math.log2(math.e))

    def k_row(g, b, j):
        return b * ng + jnp.clip(g - 1, 0, ng - 3) + j

    blk = (gtok, hw)
    in_specs = [pl.BlockSpec(memory_space=pltpu.SMEM), pl.BlockSpec(blk, lambda h, g, b: (b * ng + g, h))]
    for col0 in (nhq, 2 * nhq):
        for j in range(3):
            in_specs.append(pl.BlockSpec(blk, lambda h, g, b, j=j, col0=col0: (k_row(g, b, j), col0 + h)))
    in_specs.append(pl.BlockSpec(blk, lambda h, g, b: (ctx_blk0 + b, nhq + h)))
    in_specs.append(pl.BlockSpec(blk, lambda h, g, b: (ctx_blk0 + b, 2 * nhq + h)))
    in_specs.append(pl.BlockSpec((NA_HEADS_PER_STEP, len(pairs), GRID_W, 2 * GRID_W), lambda h, g, b: (h, 0, 0, 0)))
    y_lat = pl.pallas_call(
        functools.partial(_na_kernel, n_groups=ng),
        grid=(nhq, ng, n_batch),
        in_specs=in_specs,
        out_specs=pl.BlockSpec(blk, lambda h, g, b: (b * ng + g, h)),
        out_shape=jax.ShapeDtypeStruct((t_lat, hd), BF16),
        compiler_params=_params(3),
        name="na_attention",
    )(jnp.asarray(idx.reshape(-1)), *([qkv] * 9), pair_tab)
    if not ctx_out:
        return y_lat, None
    y_ctx = pl.pallas_call(
        _na_ctx_kernel,
        grid=(nhq, n_batch),
        in_specs=[
            pl.BlockSpec(blk, lambda h, b: (ctx_blk0 + b, h)),
            pl.BlockSpec(blk, lambda h, b: (ctx_blk0 + b, nhq + h)),
            pl.BlockSpec(blk, lambda h, b: (ctx_blk0 + b, 2 * nhq + h)),
        ],
        out_specs=pl.BlockSpec(blk, lambda h, b: (b, h)),
        out_shape=jax.ShapeDtypeStruct((n_batch * lc, hd), BF16),
        compiler_params=_params(2),
        name="na_attention_ctx",
    )(qkv, qkv, qkv)
    return y_lat, y_ctx


def _rope_tables(seq, width):
    t = jnp.arange(seq)
    pos = jnp.stack([t // GRID_W, t % GRID_W], axis=-1).astype(F32)
    n_freq = DA_HEAD_DIM // 4
    inv_freq = ROPE_BASE ** (-jnp.arange(n_freq, dtype=F32) / n_freq)
    ang = pos[:, :, None] * inv_freq
    cos = jnp.broadcast_to(jnp.cos(ang)[:, :, None, :], (seq, 2, 2, n_freq)).reshape(seq, DA_HEAD_DIM)
    sign = jnp.array([-1.0, 1.0], F32)[None, None, :, None]
    sin = (jnp.broadcast_to(jnp.sin(ang)[:, :, None, :], (seq, 2, 2, n_freq)) * sign).reshape(seq, DA_HEAD_DIM)
    reps = width // DA_HEAD_DIM
    return jnp.tile(cos, (1, reps)), jnp.tile(sin, (1, reps))


def _da_kernel(lam_ref, q_ref, *refs, out_scale, n_sub, with_latent):
    n_kv = 2 if with_latent else 1
    k_refs, v_refs = refs[:n_kv], refs[n_kv:2 * n_kv]
    sw_ref, o_ref = refs[2 * n_kv], refs[2 * n_kv + 1]
    dh = DA_HEAD_DIM
    lam = lam_ref[0]
    rows = q_ref.shape[0] // n_sub
    for sub in range(n_sub):
        rs = slice(sub * rows, (sub + 1) * rows)
        probs, sums = [], []
        for comp in range(2):
            sl = slice(comp * dh, (comp + 1) * dh)
            q = q_ref[rs, sl]
            s = [_dot_nt(q, k_ref[:, sl]) for k_ref in k_refs]
            m = functools.reduce(jnp.maximum, [jnp.max(x, axis=1, keepdims=True) for x in s])
            p = [jnp.exp2(x - m) for x in s]
            probs.append(p)
            sums.append(functools.reduce(jnp.add, [jnp.sum(x, axis=1, keepdims=True) for x in p]))
        ratio = lam * sums[0] / sums[1]
        o = None
        for p1, p2, v_ref in zip(probs[0], probs[1], v_refs):
            part = _dot((p1 - ratio * p2).astype(BF16), v_ref[...])
            o = part if o is None else o + part
        o = o * (1.0 / sums[0])
        y = o * lax.rsqrt(jnp.mean(o * o, axis=-1, keepdims=True) + DA_SUBLN_EPS) * sw_ref[...]
        o_ref[rs, :] = (y * out_scale).astype(o_ref.dtype)


def _diff_attention(qkv, lam_full, subln_w, lambda_init, n_batch, seq, lc, ctx_out):
    hw = 2 * DA_HEAD_DIM
    bq, n_sub = DA_Q_BLOCK, DA_Q_SUBBLOCKS
    nq = seq // bq
    t_lat = n_batch * seq
    ctx_blk0 = t_lat // lc
    lam1 = lam_full.reshape(1).astype(F32)
    sw = subln_w.reshape(1, hw)
    common = dict(out_scale=1.0 - lambda_init)
    smem = pl.BlockSpec(memory_space=pltpu.SMEM)
    y_lat = pl.pallas_call(
        functools.partial(_da_kernel, n_sub=n_sub, with_latent=True, **common),
        grid=(n_batch, DA_HEADS, nq),
        in_specs=[
            smem,
            pl.BlockSpec((bq, hw), lambda b, h, i: (b * nq + i, h)),
            pl.BlockSpec((seq, hw), lambda b, h, i: (b, DA_HEADS + h)),
            pl.BlockSpec((lc, hw), lambda b, h, i: (ctx_blk0 + b, DA_HEADS + h)),
            pl.BlockSpec((seq, hw), lambda b, h, i: (b, 2 * DA_HEADS + h)),
            pl.BlockSpec((lc, hw), lambda b, h, i: (ctx_blk0 + b, 2 * DA_HEADS + h)),
            pl.BlockSpec((1, hw), lambda b, h, i: (0, 0)),
        ],
        out_specs=pl.BlockSpec((bq, hw), lambda b, h, i: (b * nq + i, h)),
        out_shape=jax.ShapeDtypeStruct((t_lat, DA_HEADS * hw), BF16),
        compiler_params=_params(3),
        name="diff_attention",
    )(lam1, qkv, qkv, qkv, qkv, qkv, sw)
    if not ctx_out:
        return y_lat, None
    y_ctx = pl.pallas_call(
        functools.partial(_da_kernel, n_sub=1, with_latent=False, **common),
        grid=(n_batch, DA_HEADS),
        in_specs=[
            smem,
            pl.BlockSpec((lc, hw), lambda b, h: (ctx_blk0 + b, h)),
            pl.BlockSpec((lc, hw), lambda b, h: (ctx_blk0 + b, DA_HEADS + h)),
            pl.BlockSpec((lc, hw), lambda b, h: (ctx_blk0 + b, 2 * DA_HEADS + h)),
            pl.BlockSpec((1, hw), lambda b, h: (0, 0)),
        ],
        out_specs=pl.BlockSpec((lc, hw), lambda b, h: (b, h)),
        out_shape=jax.ShapeDtypeStruct((n_batch * lc, DA_HEADS * hw), BF16),
        compiler_params=_params(2),
        name="diff_attention_ctx",
    )(lam1, qkv, qkv, qkv, sw)
    return y_lat, y_ctx


def _conv_kernel(prev_ref, cur_ref, next_ref, w_ref, b_ref, o_ref, *, rb, lat_blocks, lat_blocks_per_seq, ctx_blocks_per_seq):
    i = pl.program_id(0)
    is_lat = i < lat_blocks
    pos = jnp.where(is_lat, i % lat_blocks_per_seq, (i - lat_blocks) % ctx_blocks_per_seq)
    per_seq = jnp.where(is_lat, lat_blocks_per_seq, ctx_blocks_per_seq)
    halo = prev_ref.shape[0]
    prev = jnp.where(pos == 0, 0.0, prev_ref[...].astype(F32))
    nxt = jnp.where(pos == per_seq - 1, 0.0, next_ref[...].astype(F32))
    cat = jnp.concatenate([prev, cur_ref[...].astype(F32), nxt], axis=0)
    n = rb + 2 * halo
    acc = b_ref[...]
    for k in range(SSM_CONV):
        shifted = cat if k == SSM_CONV // 2 else pltpu.roll(cat, (SSM_CONV // 2 - k) % n, 0)
        acc = acc + w_ref[k:k + 1, :] * shifted[halo:halo + rb]
    o_ref[...] = (acc * _sigmoid(acc)).astype(o_ref.dtype)


def _ssm_conv(proj, conv_w, conv_b, n_batch, seq, lc):
    t = proj.shape[0]
    rb, ct, halo = 256, 2048, SUBLANES_BF16
    assert seq % rb == 0 and lc % rb == 0 and SSM_D_INNER % ct == 0 and SSM_CONV_DIM % ct == 0
    col0 = SSM_D_INNER // ct
    hb = rb // halo
    n_halo_blocks = t // halo
    kern = functools.partial(_conv_kernel, rb=rb, lat_blocks=n_batch * seq // rb,
                             lat_blocks_per_seq=seq // rb, ctx_blocks_per_seq=lc // rb)
    return pl.pallas_call(
        kern,
        grid=(t // rb, SSM_CONV_DIM // ct),
        in_specs=[
            pl.BlockSpec((halo, ct), lambda i, j: (jnp.maximum(i * hb - 1, 0), col0 + j)),
            pl.BlockSpec((rb, ct), lambda i, j: (i, col0 + j)),
            pl.BlockSpec((halo, ct), lambda i, j: (jnp.minimum((i + 1) * hb, n_halo_blocks - 1), col0 + j)),
            pl.BlockSpec((SSM_CONV, ct), lambda i, j: (0, j)),
            pl.BlockSpec((1, ct), lambda i, j: (0, j)),
        ],
        out_specs=pl.BlockSpec((rb, ct), lambda i, j: (i, j)),
        out_shape=jax.ShapeDtypeStruct((t, SSM_CONV_DIM), BF16),
        compiler_params=_params(2),
        name="ssm_conv",
    )(proj, proj, proj, conv_w, conv_b.reshape(1, SSM_CONV_DIM))


def _ssd_kernel(xbc_ref, dt_ref, dtb_ref, alog_ref, tri_ref, half_ref, y_ref, state_ref):
    q = SSM_CHUNK
    nh = SSM_HEADS
    p2 = 2 * SSM_HEAD_DIM
    gw = SSM_GROUP_WIDTH
    log2e = math.log2(math.e)
    direction = pl.program_id(1)
    fwd = direction == 0

    @pl.when(pl.program_id(2) == 0)
    def _():
        state_ref[...] = jnp.zeros_like(state_ref)

    pre = dt_ref[...] + dtb_ref[...]
    dt_all = jnp.maximum(pre, 0.0) + jnp.log1p(jnp.exp(-jnp.abs(pre)))
    a_all = dt_all * (jnp.exp(alog_ref[...]) * -log2e)
    tri = tri_ref[0]
    acum_all = sum(_dot(tri, piece) for piece in _split3(a_all))
    acum = jnp.where(fwd, acum_all, pltpu.roll(acum_all, nh, 1))
    acum_t = acum.T
    dt_t = jnp.where(fwd, dt_all, pltpu.roll(dt_all, nh, 1)).T
    src_t = acum_t - jnp.log(dt_t) * log2e
    lane = lax.broadcasted_iota(jnp.int32, (q, q), 1)
    last = jnp.where(fwd, q - 1, 0)
    total = jnp.sum(jnp.where(lane == last, acum_t, 0.0), axis=1, keepdims=True)
    to_end_t = dt_t * jnp.exp2(total - acum_t)
    chunk_decay = jnp.broadcast_to(jnp.exp2(total), (q, q))
    seen = tri.astype(F32) > 0.5
    first_head = lane < SSM_HEAD_DIM
    c_off = SSM_D_INNER + SSM_GROUPS * SSM_STATE

    for g in range(SSM_GROUPS):
        b_g = xbc_ref[:, SSM_D_INNER + g * SSM_STATE:SSM_D_INNER + (g + 1) * SSM_STATE]
        c_g = xbc_ref[:, c_off + g * SSM_STATE:c_off + (g + 1) * SSM_STATE]
        cb = _dot_nt(c_g, b_g)
        b_t = b_g.astype(F32).T
        y_carried = _dot(c_g, state_ref[g].astype(BF16))
        for pr in range(SSM_HEADS_PER_GROUP // 2):
            ps = slice(pr * p2, (pr + 1) * p2)
            cols = slice(g * gw + pr * p2, g * gw + (pr + 1) * p2)
            x2 = xbc_ref[:, cols]
            x_bd = jnp.concatenate([x2 * half_ref[0], x2 * half_ref[1]], axis=0)
            intra, to_state, carry_scale, decay = [], [], [], []
            for r in (2 * pr, 2 * pr + 1):
                hh = g * SSM_HEADS_PER_GROUP + r
                col = jnp.sum(jnp.where(lane == hh, acum, 0.0), axis=1, keepdims=True)
                intra.append((cb * jnp.where(seen, jnp.exp2(col - src_t[hh:hh + 1, :]), 0.0)).astype(BF16))
                to_state.append((b_t * to_end_t[hh:hh + 1, :]).astype(BF16))
                carry_scale.append(jnp.exp2(col))
                decay.append(chunk_decay[hh:hh + 1, :])
            y_ref[0, :, cols] = (_dot(jnp.concatenate(intra, axis=1), x_bd)
                                 + jnp.where(first_head, carry_scale[0], carry_scale[1]) * y_carried[:, ps]
                                 ).astype(y_ref.dtype)
            state_ref[g, :, ps] = (state_ref[g, :, ps] * jnp.where(first_head[:1], decay[0], decay[1])
                                   + _dot(jnp.concatenate(to_state, axis=1), x_bd))


def _ssd_scan(xbc, dt_raw, dt_bias, a_log, n_batch, seq, lc):
    t = xbc.shape[0]
    q = SSM_CHUNK
    n_lat, n_ctx = seq // q, lc // q
    steps = n_lat + n_ctx
    ctx_blk0 = n_batch * n_lat

    def blk(b, d, s):
        ctx_chunk = jnp.where(d == 0, s, n_ctx - 1 - s)
        lat_chunk = jnp.where(d == 0, s - n_ctx, steps - 1 - s)
        return jnp.where(s < n_ctx, ctx_blk0 + b * n_ctx + ctx_chunk, b * n_lat + lat_chunk)

    idx = np.arange(q)
    tri = np.stack([idx[None, :] <= idx[:, None], idx[None, :] >= idx[:, None]]).astype(np.float32)
    first = np.broadcast_to(np.arange(2 * SSM_HEAD_DIM)[None, :] < SSM_HEAD_DIM, (q, 2 * SSM_HEAD_DIM))
    half = np.stack([first, ~first]).astype(np.float32)
    return pl.pallas_call(
        _ssd_kernel,
        grid=(n_batch, 2, steps),
        in_specs=[
            pl.BlockSpec((q, SSM_CONV_DIM), lambda b, d, s: (blk(b, d, s), 0)),
            pl.BlockSpec((q, 2 * SSM_HEADS), lambda b, d, s: (blk(b, d, s), 0)),
            pl.BlockSpec((1, 2 * SSM_HEADS), lambda b, d, s: (0, 0)),
            pl.BlockSpec((1, 2 * SSM_HEADS), lambda b, d, s: (0, 0)),
            pl.BlockSpec((1, q, q), lambda b, d, s: (d, 0, 0)),
            pl.BlockSpec((2, q, 2 * SSM_HEAD_DIM), lambda b, d, s: (0, 0, 0)),
        ],
        out_specs=pl.BlockSpec((1, q, SSM_D_INNER), lambda b, d, s: (d, blk(b, d, s), 0)),
        out_shape=jax.ShapeDtypeStruct((2, t, SSM_D_INNER), BF16),
        scratch_shapes=[pltpu.VMEM((SSM_GROUPS, SSM_STATE, SSM_GROUP_WIDTH), F32)],
        compiler_params=_params(3),
        name="ssd_scan",
    )(xbc, dt_raw, dt_bias.reshape(1, -1), a_log.reshape(1, -1), jnp.asarray(tri, BF16), jnp.asarray(half, BF16))


def _ssm_finish_kernel(yf_ref, yb_ref, x_ref, z_ref, dsk_ref, nw_ref, o_ref):
    gw = SSM_GROUP_WIDTH
    z = z_ref[...].astype(F32)
    y = yf_ref[0].astype(F32) + yb_ref[0].astype(F32) + x_ref[...].astype(F32) * dsk_ref[...]
    y = y * (z * _sigmoid(z))
    for g in range(SSM_GROUPS):
        cs = slice(g * gw, (g + 1) * gw)
        yg = y[:, cs]
        o_ref[:, cs] = (yg * lax.rsqrt(jnp.mean(yg * yg, axis=-1, keepdims=True) + EPS) * nw_ref[:, cs]).astype(o_ref.dtype)


def _ssm_finish(y2, xbc, proj, d_skip, norm_w):
    t = xbc.shape[0]
    di = SSM_D_INNER
    rb = 256
    dsk = jnp.repeat(d_skip.astype(F32), SSM_HEAD_DIM).reshape(1, di)
    return pl.pallas_call(
        _ssm_finish_kernel,
        grid=(t // rb,),
        in_specs=[
            pl.BlockSpec((1, rb, di), lambda i: (0, i, 0)),
            pl.BlockSpec((1, rb, di), lambda i: (1, i, 0)),
            pl.BlockSpec((rb, di), lambda i: (i, 0)),
            pl.BlockSpec((rb, di), lambda i: (i, 0)),
            pl.BlockSpec((1, di), lambda i: (0, 0)),
            pl.BlockSpec((1, di), lambda i: (0, 0)),
        ],
        out_specs=pl.BlockSpec((rb, di), lambda i: (i, 0)),
        out_shape=jax.ShapeDtypeStruct((t, di), BF16),
        compiler_params=_params(1),
        name="ssm_finish",
    )(y2, y2, xbc, proj, dsk, norm_w.reshape(1, di))


def _mamba2(xs, ssq, mod_l, geom, w_in_stack, layer, conv_w, conv_b, dt_bias, a_log, d_skip, norm_w, n_batch, seq, lc):
    t = xs.shape[0]
    n_main = SSM_D_INNER + SSM_CONV_DIM
    proj = _mm_pre(xs, ssq, mod_l, 0, w_in_stack, layer, t, geom, n_cols=n_main)
    dt_raw = _mm_pre(xs, ssq, mod_l, 0, w_in_stack, layer, t, geom, col_start=n_main, n_cols=2 * SSM_HEADS,
                     out_dtype=F32)
    xbc = _ssm_conv(proj, conv_w, conv_b, n_batch, seq, lc)
    y2 = _ssd_scan(xbc, dt_raw, dt_bias, a_log, n_batch, seq, lc)
    return _ssm_finish(y2, xbc, proj, d_skip, norm_w)


def kernel(x, c, ctx, c_ctx, ada_w, ada_b, norm_mix_w, norm_mlp_w, mlp_w1, mlp_w2, na_w_qkv, na_w_o, na_rpb,
           ssm_w_in, ssm_conv_w, ssm_conv_b, ssm_dt_bias, ssm_a_log, ssm_d, ssm_norm_w, ssm_w_out,
           da_w_qkv, da_w_o, da_lambda, da_subln_w, final_norm_w):
    n_batch, seq, d = x.shape
    lc = ctx.shape[1]
    t_lat, t_ctx = n_batch * seq, n_batch * lc
    t = t_lat + t_ctx
    geom = dict(t_lat=t_lat, seq=seq, n_batch=n_batch)
    assert d == D_MODEL and seq % GRID_W == 0 and n_batch < 8

    cond = jnp.concatenate([c, c_ctx[None, :], jnp.zeros((8 - n_batch - 1, d), F32)], axis=0)
    mod = _adaln_all(cond, ada_w, ada_b)
    bufs = _prep(x.reshape(t_lat, d), ctx.reshape(t_ctx, d), norm_mix_w[0], mod[0], geom)
    mlp_w2_bf16 = mlp_w2.astype(BF16)

    for i in range(DEPTH):
        last = i == DEPTH - 1
        mixer, j = i % N_MIXERS, i // N_MIXERS
        y_ctx = None
        mod_l = mod[i]
        rows_out = t_lat if last else t
        _, xs, ssq = bufs
        if mixer == 0:
            qkv = _mm_pre(xs, ssq, mod_l, 0, na_w_qkv, j, t, geom, lead=(NA_HEADS * NA_HEAD_DIM, ATTN_EXP2_SCALE))
            y, y_ctx = _na_attention(qkv, na_rpb[j], n_batch, seq, lc, not last)
            w_o = na_w_o
        elif mixer == 1:
            y = _mamba2(xs, ssq, mod_l, geom, ssm_w_in, j, ssm_conv_w[j], ssm_conv_b[j], ssm_dt_bias[j],
                        ssm_a_log[j], ssm_d[j], ssm_norm_w[j], n_batch, seq, lc)
            w_o = ssm_w_out
        else:
            lambda_init = 0.8 - 0.6 * math.exp(-0.3 * i)
            qk_w = 2 * DA_HEADS * DA_HEAD_DIM
            cos_t, sin_t = _rope_tables(seq, DA_HEAD_DIM)
            qkv = _mm_pre(xs, ssq, mod_l, 0, da_w_qkv, j, t, geom, rope=(cos_t, sin_t, t_lat, 2 * qk_w),
                          lead=(qk_w, ATTN_EXP2_SCALE))
            lam = da_lambda[j].astype(F32)
            lam_full = jnp.exp(jnp.sum(lam[0] * lam[1])) - jnp.exp(jnp.sum(lam[2] * lam[3])) + lambda_init
            y, y_ctx = _diff_attention(qkv, lam_full, da_subln_w[j], lambda_init, n_batch, seq, lc, not last)
            w_o = da_w_o
        w_o = w_o.astype(BF16)
        mlp_norm = (norm_mlp_w[i], mod_l, 4)
        if y_ctx is None:
            bufs = _mm_res(y, w_o, j, bufs, mod_l, 2, rows_out, geom, next_norm=mlp_norm)
        else:
            bufs = _mm_res(y, w_o, j, bufs, mod_l, 2, t_lat, geom, next_norm=mlp_norm)
            bufs = _mm_res(y_ctx, w_o, j, bufs, mod_l, 2, t_ctx, geom, row_off=t_lat, next_norm=mlp_norm)
        hidden = _mm_pre(bufs[1], bufs[2], mod_l, 3, mlp_w1, i, rows_out, geom, act="relu2")
        mixer_norm = None if last else (norm_mix_w[i + 1], mod[i + 1], 1)
        bufs = _mm_res(hidden, mlp_w2_bf16, i, bufs, mod_l, 5, rows_out, geom, next_norm=mixer_norm)

    return _final_norm(bufs[0], final_norm_w, t_lat).reshape(n_batch, seq, d)
```

```python
import functools
import math

import jax
import jax.numpy as jnp
import numpy as np
from jax import lax
from jax.experimental import pallas as pl
from jax.experimental.pallas import tpu as pltpu

F32 = jnp.float32
BF16 = jnp.bfloat16

D_MODEL = 2048
DEPTH = 4
GRID_W = 64
N_MIXERS = 3
EPS = 1e-6

NA_HEADS = 16
NA_HEAD_DIM = 128
NA_WIN_ROWS = 8
NA_WIN_COLS = 16
NA_GROUP_ROWS = 4
NA_KEY_ROWS = 3 * NA_GROUP_ROWS
NA_HEADS_PER_STEP = 8

SSM_D_INNER = 4096
SSM_HEAD_DIM = 64
SSM_HEADS = 64
SSM_GROUPS = 8
SSM_HEADS_PER_GROUP = 8
SSM_STATE = 128
SSM_CONV = 5
SSM_CHUNK = 128
SSM_CONV_DIM = SSM_D_INNER + 2 * SSM_GROUPS * SSM_STATE
SSM_GROUP_WIDTH = SSM_HEADS_PER_GROUP * SSM_HEAD_DIM

DA_HEADS = 8
DA_HEAD_DIM = 128
DA_SUBLN_EPS = 1e-5
DA_Q_BLOCK = 1024
DA_Q_SUBBLOCKS = 4
ROPE_BASE = 10000.0

MASK_NEG = -1e30
assert NA_HEAD_DIM == DA_HEAD_DIM
ATTN_EXP2_SCALE = NA_HEAD_DIM ** -0.5 * math.log2(math.e)

V7X_VMEM_BYTES = 64 * 1024 * 1024
VMEM_LIMIT = (V7X_VMEM_BYTES * 3) // 4
MM_INNER_COLS = 512
MM_RES_INNER_COLS = 256
VMEM_LIMIT_F32_WEIGHTS = (V7X_VMEM_BYTES * 29) // 32
SUBLANES_BF16 = 16
LANES = 128


def _params(n_grid_dims, vmem_limit=VMEM_LIMIT):
    return pltpu.CompilerParams(
        dimension_semantics=("arbitrary",) * n_grid_dims, vmem_limit_bytes=vmem_limit)


def _sigmoid(x):
    return 1.0 / (1.0 + jnp.exp(-x))


def _dot(a, b):
    return jnp.dot(a, b, preferred_element_type=F32)


def _dot_nt(a, b):
    return lax.dot_general(a, b, (((1,), (1,)), ((), ())), preferred_element_type=F32)


def _split3(v):
    hi = v.astype(BF16)
    r1 = v - hi.astype(F32)
    mid = r1.astype(BF16)
    lo = (r1 - mid.astype(F32)).astype(BF16)
    return hi, mid, lo


def _mod_row(row0, t_lat, seq, n_batch):
    return jnp.where(row0 < t_lat, row0 // seq, n_batch)


def _adaln_kernel(c_ref, w_ref, b_ref, o_ref):
    x = c_ref[...]
    x = (x * _sigmoid(x)).astype(BF16)
    o_ref[0] = _dot(x, w_ref[0].astype(BF16)) + b_ref[0]


def _adaln_all(cond, ada_w, ada_b):
    depth, d, n = ada_w.shape
    bn = 1024
    return pl.pallas_call(
        _adaln_kernel,
        grid=(depth, n // bn),
        in_specs=[
            pl.BlockSpec((8, d), lambda l, j: (0, 0)),
            pl.BlockSpec((1, d, bn), lambda l, j: (l, 0, j)),
            pl.BlockSpec((1, 1, bn), lambda l, j: (l, 0, j)),
        ],
        out_specs=pl.BlockSpec((1, 8, bn), lambda l, j: (l, 0, j)),
        out_shape=jax.ShapeDtypeStruct((depth, 8, n), F32),
        compiler_params=_params(2),
        name="adaln",
    )(cond, ada_w, ada_b.reshape(depth, 1, n))


def _final_norm_kernel(h_ref, nw_ref, o_ref):
    x = h_ref[...]
    o_ref[...] = x * lax.rsqrt(jnp.mean(x * x, axis=-1, keepdims=True) + EPS) * nw_ref[...]


def _final_norm(h, norm_w, m_rows):
    d = h.shape[1]
    bm = 256
    return pl.pallas_call(
        _final_norm_kernel,
        grid=(m_rows // bm,),
        in_specs=[pl.BlockSpec((bm, d), lambda i: (i, 0)), pl.BlockSpec((1, d), lambda i: (0, 0))],
        out_specs=pl.BlockSpec((bm, d), lambda i: (i, 0)),
        out_shape=jax.ShapeDtypeStruct((m_rows, d), F32),
        compiler_params=_params(1),
        name="final_norm",
    )(h, norm_w.reshape(1, d))


def _mm_blocks(m_rows, k, n):
    bm = 1024 if m_rows % 1024 == 0 else 256
    bn = 1024 if n % 1024 == 0 else (512 if n % 512 == 0 else 128)
    if k >= 8192:
        bm, bn = min(bm, 512), min(bn, 512)
    elif k >= 4096:
        bn = min(bn, 512)
    return bm, bn


def _accumulate_over_columns(ssq_ref, part, j):
    @pl.when(j == 0)
    def _():
        ssq_ref[...] = part

    @pl.when(j > 0)
    def _():
        ssq_ref[...] = ssq_ref[...] + part


def _next_norm_outputs(hb, j, midx, nw_ref, nscale_ref, xs_ref, ssq_ref):
    g = nw_ref[...] * (1.0 + nscale_ref[pl.ds(midx, 1), :])
    xs_ref[...] = (hb * g).astype(xs_ref.dtype)
    sq = hb * hb
    part = functools.reduce(jnp.add, [sq[:, c:c + LANES] for c in range(0, sq.shape[1], LANES)])
    _accumulate_over_columns(ssq_ref, part, j)


def _prep_kernel(x_ref, c_ref, nw_ref, nscale_ref, h_ref, xs_ref, ssq_ref, *, bm, lat_blocks, t_lat, seq, n_batch):
    i, j = pl.program_id(0), pl.program_id(1)
    midx = _mod_row(i * bm, t_lat, seq, n_batch)

    def emit(src_ref):
        hb = src_ref[...]
        h_ref[...] = hb
        _next_norm_outputs(hb, j, midx, nw_ref, nscale_ref, xs_ref, ssq_ref)

    pl.when(i < lat_blocks)(lambda: emit(x_ref))
    pl.when(i >= lat_blocks)(lambda: emit(c_ref))


def _prep(x2d, c2d, norm_w, mod_next, geom):
    t_lat, d = x2d.shape
    t_ctx = c2d.shape[0]
    t = t_lat + t_ctx
    bm = 512 if t_ctx % 512 == 0 else 256
    bn = 1024
    lat_blocks = t_lat // bm
    sc0 = d // bn
    return pl.pallas_call(
        functools.partial(_prep_kernel, bm=bm, lat_blocks=lat_blocks, **geom),
        grid=(t // bm, d // bn),
        in_specs=[
            pl.BlockSpec((bm, bn), lambda i, j: (jnp.minimum(i, lat_blocks - 1), j)),
            pl.BlockSpec((bm, bn), lambda i, j: (jnp.maximum(i - lat_blocks, 0), j)),
            pl.BlockSpec((1, bn), lambda i, j: (0, j)),
            pl.BlockSpec((8, bn), lambda i, j: (0, sc0 + j)),
        ],
        out_specs=[
            pl.BlockSpec((bm, bn), lambda i, j: (i, j)),
            pl.BlockSpec((bm, bn), lambda i, j: (i, j)),
            pl.BlockSpec((bm, LANES), lambda i, j: (i, 0)),
        ],
        out_shape=[
            jax.ShapeDtypeStruct((t, d), F32),
            jax.ShapeDtypeStruct((t, d), BF16),
            jax.ShapeDtypeStruct((t, LANES), F32),
        ],
        compiler_params=_params(2),
        name="prep",
    )(x2d, c2d, norm_w.reshape(1, d), mod_next)


def _mm_pre_kernel(*refs, bm, d_norm, shift_k, act, rope_row_blocks, rope_col_blocks, lead_blocks, lead_scale,
                   t_lat, seq, n_batch):
    xs_ref, ssq_ref, mod_ref, w_ref = refs[:4]
    pos = 4
    if rope_col_blocks:
        cos_ref, sin_ref = refs[pos:pos + 2]
        pos += 2
    o_ref, sw_ref = refs[pos], refs[pos + 1]
    i, j = pl.program_id(0), pl.program_id(1)
    bn = o_ref.shape[1]
    hn = min(bn, MM_INNER_COLS)

    @pl.when(i == 0)
    def _():
        shift = mod_ref[:, shift_k * d_norm:(shift_k + 1) * d_norm].astype(BF16)
        for c0 in range(0, bn, hn):
            sw_ref[j, :, c0:c0 + hn] = _dot(shift, w_ref[:, c0:c0 + hn].astype(BF16))

    midx = _mod_row(i * bm, t_lat, seq, n_batch)
    r = lax.rsqrt(jnp.sum(ssq_ref[...], axis=1, keepdims=True) * (1.0 / d_norm) + EPS)
    lead = jnp.where(j < lead_blocks, lead_scale, 1.0) if lead_blocks else 1.0
    r = r * lead
    do_rope = (i < rope_row_blocks) & (j < rope_col_blocks)
    for c0 in range(0, bn, hn):
        acc = _dot(xs_ref[...], w_ref[:, c0:c0 + hn].astype(BF16))
        acc = acc * r + sw_ref[j, pl.ds(midx, 1), c0:c0 + hn] * lead
        if act == "relu2":
            acc = jnp.maximum(acc, 0.0)
            acc = acc * acc
        if not rope_col_blocks:
            o_ref[:, c0:c0 + hn] = acc.astype(o_ref.dtype)
            continue

        @pl.when(do_rope)
        def _(acc=acc, c0=c0):
            hd = cos_ref.shape[1]
            lane = lax.broadcasted_iota(jnp.int32, (bm, hd), 1)
            first = (lane % (hd // 2)) < hd // 4
            for c in range(hn // hd):
                xh = acc[:, c * hd:(c + 1) * hd]
                partner = jnp.where(first, pltpu.roll(xh, hd - hd // 4, 1), pltpu.roll(xh, hd // 4, 1))
                o_ref[:, c0 + c * hd:c0 + (c + 1) * hd] = (xh * cos_ref[...] + partner * sin_ref[...]).astype(o_ref.dtype)

        @pl.when(jnp.logical_not(do_rope))
        def _(acc=acc, c0=c0):
            o_ref[:, c0:c0 + hn] = acc.astype(o_ref.dtype)


def _mm_pre(xs, ssq, mod_l, shift_k, w_stack, layer, m_rows, geom, act=None, rope=None, col_start=0, n_cols=None,
            out_dtype=BF16, lead=None):
    d = xs.shape[1]
    k = w_stack.shape[1]
    n = n_cols or w_stack.shape[2]
    bm, bn = _mm_blocks(m_rows, k, n)
    assert col_start % bn == 0
    cb0 = col_start // bn
    kern = functools.partial(
        _mm_pre_kernel, bm=bm, d_norm=d, shift_k=shift_k, act=act,
        lead_blocks=(lead[0] // bn if lead else 0), lead_scale=(lead[1] if lead else 1.0),
        rope_row_blocks=(rope[2] // bm if rope else 0), rope_col_blocks=(rope[3] // bn if rope else 0), **geom)
    in_specs = [
        pl.BlockSpec((bm, d), lambda i, j: (i, 0)),
        pl.BlockSpec((bm, LANES), lambda i, j: (i, 0)),
        pl.BlockSpec(mod_l.shape, lambda i, j: (0, 0)),
        pl.BlockSpec((None, k, bn), lambda i, j: (layer, 0, cb0 + j)),
    ]
    args = [xs, ssq, mod_l, w_stack]
    if rope:
        seq_blocks = rope[0].shape[0] // bm
        hd = rope[0].shape[1]
        in_specs += [pl.BlockSpec((bm, hd), lambda i, j: (i % seq_blocks, 0))] * 2
        args += [rope[0], rope[1]]
    return pl.pallas_call(
        kern,
        grid=(m_rows // bm, n // bn),
        in_specs=in_specs,
        out_specs=pl.BlockSpec((bm, bn), lambda i, j: (i, j)),
        out_shape=jax.ShapeDtypeStruct((m_rows, n), out_dtype),
        scratch_shapes=[pltpu.VMEM((n // bn, 8, bn), F32)],
        compiler_params=_params(2, VMEM_LIMIT_F32_WEIGHTS),
        name="mm_pre" + ("_" + act if act else "") + ("_rope" if rope else ""),
    )(*args)


def _mm_res_kernel(*refs, bm, n_k, row_off, emit_next, t_lat, seq, n_batch):
    x_ref, w_ref, res_ref, gate_ref = refs[:4]
    pos = 4
    if emit_next:
        nw_ref, nscale_ref = refs[pos:pos + 2]
        pos += 4
    o_ref = refs[pos]
    pos += 1
    if emit_next:
        xs_ref, ssq_ref = refs[pos:pos + 2]
        pos += 2
    acc_ref = refs[pos].at[pl.program_id(2)] if n_k > 1 else None
    i, kk, j = pl.program_id(0), pl.program_id(1), pl.program_id(2)
    midx = _mod_row(row_off + i * bm, t_lat, seq, n_batch)
    bn = o_ref.shape[1]
    slices = [slice(c0, c0 + MM_RES_INNER_COLS) for c0 in range(0, bn, MM_RES_INNER_COLS)]
    parts = [_dot(x_ref[...], w_ref[:, cs]) for cs in slices]

    def finish(accs):
        sums = None
        for cs, acc in zip(slices, accs):
            hb = res_ref[:, cs] + gate_ref[pl.ds(midx, 1), cs] * acc
            o_ref[:, cs] = hb
            if emit_next:
                g = nw_ref[:, cs] * (1.0 + nscale_ref[pl.ds(midx, 1), cs])
                xs_ref[:, cs] = (hb * g).astype(xs_ref.dtype)
                sq = hb * hb
                for c in range(0, sq.shape[1], LANES):
                    sums = sq[:, c:c + LANES] if sums is None else sums + sq[:, c:c + LANES]
        if emit_next:
            _accumulate_over_columns(ssq_ref, sums, j)

    if n_k == 1:
        finish(parts)
        return

    @pl.when(kk == 0)
    def _():
        for cs, part in zip(slices, parts):
            acc_ref[:, cs] = part

    @pl.when((kk > 0) & (kk < n_k - 1))
    def _():
        for cs, part in zip(slices, parts):
            acc_ref[:, cs] = acc_ref[:, cs] + part

    @pl.when(kk == n_k - 1)
    def _():
        finish([acc_ref[:, cs] + part for cs, part in zip(slices, parts)])


def _mm_res(x, w_stack, layer, bufs, mod_l, gate_k, m_rows, geom, row_off=0, next_norm=None):
    h, xs, ssq = bufs
    k, n = w_stack.shape[1:]
    n_k = 2 if k >= 8192 else 1
    kb = k // n_k
    bm = 1024 if m_rows % 1024 == 0 else 256
    bn = 512 if k >= 4096 else 1024
    assert row_off % bm == 0 and n % bn == 0
    blk_off = row_off // bm
    nb = n // bn
    emit_next = next_norm is not None

    def out_col(kk, j):
        return jnp.where(kk == n_k - 1, j, 0)

    in_specs = [
        pl.BlockSpec((bm, kb), lambda i, kk, j: (i, kk)),
        pl.BlockSpec((None, kb, bn), lambda i, kk, j: (layer, kk, j)),
        pl.BlockSpec((bm, bn), lambda i, kk, j: (blk_off + i, out_col(kk, j))),
        pl.BlockSpec((8, bn), lambda i, kk, j: (0, gate_k * nb + j)),
    ]
    args = [x, w_stack, h, mod_l]
    out_specs = [pl.BlockSpec((bm, bn), lambda i, kk, j: (blk_off + i, out_col(kk, j)))]
    out_shape = [jax.ShapeDtypeStruct(h.shape, F32)]
    aliases = {2: 0}
    if emit_next:
        nw, mod_next, scale_slot = next_norm
        in_specs += [
            pl.BlockSpec((1, bn), lambda i, kk, j: (0, j)),
            pl.BlockSpec((8, bn), lambda i, kk, j: (0, scale_slot * nb + j)),
            pl.BlockSpec(memory_space=pl.ANY),
            pl.BlockSpec(memory_space=pl.ANY),
        ]
        args += [nw.reshape(1, n), mod_next, xs, ssq]
        out_specs += [
            pl.BlockSpec((bm, bn), lambda i, kk, j: (blk_off + i, out_col(kk, j))),
            pl.BlockSpec((bm, LANES), lambda i, kk, j: (blk_off + i, 0)),
        ]
        out_shape += [jax.ShapeDtypeStruct(xs.shape, BF16), jax.ShapeDtypeStruct(ssq.shape, F32)]
        aliases.update({6: 1, 7: 2})
    outs = pl.pallas_call(
        functools.partial(_mm_res_kernel, bm=bm, n_k=n_k, row_off=row_off, emit_next=emit_next, **geom),
        grid=(m_rows // bm, n_k, nb),
        in_specs=in_specs,
        out_specs=out_specs,
        out_shape=out_shape,
        scratch_shapes=[pltpu.VMEM((nb, bm, bn), F32)] if n_k > 1 else [],
        input_output_aliases=aliases,
        compiler_params=_params(3),
        name="mm_res",
    )(*args)
    return (outs[0], outs[1], outs[2]) if emit_next else (outs[0], xs, ssq)


def _na_pair_layout(rows):
    gq, gk = NA_GROUP_ROWS, NA_KEY_ROWS
    pairs = {}
    idx = np.zeros((3, gq, gk // 2), np.int32)
    for typ, (r0, ks) in enumerate(((0, 0), (gq, 0), (rows - gq, rows - gk))):
        for i in range(gq):
            r = r0 + i
            rs = min(max(r - NA_WIN_ROWS // 2, 0), rows - NA_WIN_ROWS)
            dr = [kr - r + NA_WIN_ROWS - 1 if rs <= kr < rs + NA_WIN_ROWS else -1 for kr in range(ks, ks + gk)]
            for p in range(gk // 2):
                idx[typ, i, p] = pairs.setdefault((dr[2 * p], dr[2 * p + 1]), len(pairs))
    return list(pairs), idx


def _na_pair_table(rpb, pairs, out_scale):
    heads = rpb.shape[0]
    col = np.arange(GRID_W)
    col_start = np.clip(col - NA_WIN_COLS // 2, 0, GRID_W - NA_WIN_COLS)
    col_mask = (col[None, :] >= col_start[:, None]) & (col[None, :] < col_start[:, None] + NA_WIN_COLS)
    dc_idx = np.clip(col[None, :] - col[:, None], -(NA_WIN_COLS - 1), NA_WIN_COLS - 1) + NA_WIN_COLS - 1
    onehot = (dc_idx[None] == np.arange(2 * NA_WIN_COLS - 1)[:, None, None]).astype(np.float32)
    per_dr = jnp.einsum("hrd,dqk->hrqk", rpb.astype(F32), onehot, precision=lax.Precision.HIGHEST)
    per_dr = jnp.where(col_mask[None, None], per_dr * out_scale, MASK_NEG)
    masked = jnp.full((heads, GRID_W, GRID_W), MASK_NEG, F32)
    block = lambda dr: masked if dr < 0 else per_dr[:, dr]
    return jnp.stack([jnp.concatenate([block(a), block(b)], axis=-1) for a, b in pairs], axis=1)


def _softmax_pv(scores, values, out_dtype):
    m = functools.reduce(jnp.maximum, [jnp.max(t, axis=1, keepdims=True) for t in scores])
    probs = [jnp.exp2(t - m) for t in scores]
    denom = functools.reduce(jnp.add, [jnp.sum(p, axis=1, keepdims=True) for p in probs])
    o = None
    for p, v in zip(probs, values):
        part = _dot(p.astype(BF16), v)
        o = part if o is None else o + part
    return (o * (1.0 / denom)).astype(out_dtype)


def _na_kernel(idx_ref, q_ref, k0_ref, k1_ref, k2_ref, v0_ref, v1_ref, v2_ref, kc_ref, vc_ref, pair_ref, o_ref,
               *, n_groups):
    dh = NA_HEAD_DIM
    gq, n_pairs = NA_GROUP_ROWS, NA_KEY_ROWS // 2
    g = pl.program_id(1)
    typ = jnp.where(g == 0, 0, jnp.where(g == n_groups - 1, 2, 1))
    for hh in range(NA_HEADS_PER_STEP):
        sl = slice(hh * dh, (hh + 1) * dh)
        q = q_ref[:, sl]
        bias = jnp.concatenate(
            [jnp.concatenate([pair_ref[hh, idx_ref[(typ * gq + i) * n_pairs + p]] for p in range(n_pairs)], axis=1)
             for i in range(gq)], axis=0)
        s_loc = jnp.concatenate([_dot_nt(q, kr[:, sl]) for kr in (k0_ref, k1_ref, k2_ref)], axis=1)
        t_loc = s_loc + bias
        t_ctx = _dot_nt(q, kc_ref[:, sl])
        kb = k0_ref.shape[0]
        scores = [t_loc[:, j * kb:(j + 1) * kb] for j in range(3)] + [t_ctx]
        values = [vr[:, sl] for vr in (v0_ref, v1_ref, v2_ref, vc_ref)]
        o_ref[:, sl] = _softmax_pv(scores, values, o_ref.dtype)


def _na_ctx_kernel(q_ref, kc_ref, vc_ref, o_ref):
    dh = NA_HEAD_DIM
    for hh in range(NA_HEADS_PER_STEP):
        sl = slice(hh * dh, (hh + 1) * dh)
        t_ctx = _dot_nt(q_ref[:, sl], kc_ref[:, sl])
        o_ref[:, sl] = _softmax_pv([t_ctx], [vc_ref[:, sl]], o_ref.dtype)


def _na_attention(qkv, rpb, n_batch, seq, lc, ctx_out):
    hd = NA_HEADS * NA_HEAD_DIM
    gtok = NA_GROUP_ROWS * GRID_W
    ng = seq // gtok
    assert lc == gtok and ng >= 4
    hw = NA_HEADS_PER_STEP * NA_HEAD_DIM
    nhq = NA_HEADS // NA_HEADS_PER_STEP
    t_lat = n_batch * seq
    ctx_blk0 = n_batch * ng
    pairs, idx = _na_pair_layout(seq // GRID_W)
    pair_tab = _na_pair_table(rpb, pairs, math.log2(math.e))

    def k_row(g, b, j):
        return b * ng + jnp.clip(g - 1, 0, ng - 3) + j

    blk = (gtok, hw)
    in_specs = [pl.BlockSpec(memory_space=pltpu.SMEM), pl.BlockSpec(blk, lambda h, g, b: (b * ng + g, h))]
    for col0 in (nhq, 2 * nhq):
        for j in range(3):
            in_specs.append(pl.BlockSpec(blk, lambda h, g, b, j=j, col0=col0: (k_row(g, b, j), col0 + h)))
    in_specs.append(pl.BlockSpec(blk, lambda h, g, b: (ctx_blk0 + b, nhq + h)))
    in_specs.append(pl.BlockSpec(blk, lambda h, g, b: (ctx_blk0 + b, 2 * nhq + h)))
    in_specs.append(pl.BlockSpec((NA_HEADS_PER_STEP, len(pairs), GRID_W, 2 * GRID_W), lambda h, g, b: (h, 0, 0, 0)))
    y_lat = pl.pallas_call(
        functools.partial(_na_kernel, n_groups=ng),
        grid=(nhq, ng, n_batch),
        in_specs=in_specs,
        out_specs=pl.BlockSpec(blk, lambda h, g, b: (b * ng + g, h)),
        out_shape=jax.ShapeDtypeStruct((t_lat, hd), BF16),
        compiler_params=_params(3),
        name="na_attention",
    )(jnp.asarray(idx.reshape(-1)), *([qkv] * 9), pair_tab)
    if not ctx_out:
        return y_lat, None
    y_ctx = pl.pallas_call(
        _na_ctx_kernel,
        grid=(nhq, n_batch),
        in_specs=[
            pl.BlockSpec(blk, lambda h, b: (ctx_blk0 + b, h)),
            pl.BlockSpec(blk, lambda h, b: (ctx_blk0 + b, nhq + h)),
            pl.BlockSpec(blk, lambda h, b: (ctx_blk0 + b, 2 * nhq + h)),
        ],
        out_specs=pl.BlockSpec(blk, lambda h, b: (b, h)),
        out_shape=jax.ShapeDtypeStruct((n_batch * lc, hd), BF16),
        compiler_params=_params(2),
        name="na_attention_ctx",
    )(qkv, qkv, qkv)
    return y_lat, y_ctx


def _rope_tables(seq, width):
    t = jnp.arange(seq)
    pos = jnp.stack([t // GRID_W, t % GRID_W], axis=-1).astype(F32)
    n_freq = DA_HEAD_DIM // 4
    inv_freq = ROPE_BASE ** (-jnp.arange(n_freq, dtype=F32) / n_freq)
    ang = pos[:, :, None] * inv_freq
    cos = jnp.broadcast_to(jnp.cos(ang)[:, :, None, :], (seq, 2, 2, n_freq)).reshape(seq, DA_HEAD_DIM)
    sign = jnp.array([-1.0, 1.0], F32)[None, None, :, None]
    sin = (jnp.broadcast_to(jnp.sin(ang)[:, :, None, :], (seq, 2, 2, n_freq)) * sign).reshape(seq, DA_HEAD_DIM)
    reps = width // DA_HEAD_DIM
    return jnp.tile(cos, (1, reps)), jnp.tile(sin, (1, reps))


def _da_kernel(lam_ref, q_ref, *refs, out_scale, n_sub, with_latent):
    n_kv = 2 if with_latent else 1
    k_refs, v_refs = refs[:n_kv], refs[n_kv:2 * n_kv]
    sw_ref, o_ref = refs[2 * n_kv], refs[2 * n_kv + 1]
    dh = DA_HEAD_DIM
    lam = lam_ref[0]
    rows = q_ref.shape[0] // n_sub
    for sub in range(n_sub):
        rs = slice(sub * rows, (sub + 1) * rows)
        probs, sums = [], []
        for comp in range(2):
            sl = slice(comp * dh, (comp + 1) * dh)
            q = q_ref[rs, sl]
            s = [_dot_nt(q, k_ref[:, sl]) for k_ref in k_refs]
            m = functools.reduce(jnp.maximum, [jnp.max(x, axis=1, keepdims=True) for x in s])
            p = [jnp.exp2(x - m) for x in s]
            probs.append(p)
            sums.append(functools.reduce(jnp.add, [jnp.sum(x, axis=1, keepdims=True) for x in p]))
        ratio = lam * sums[0] / sums[1]
        o = None
        for p1, p2, v_ref in zip(probs[0], probs[1], v_refs):
            part = _dot((p1 - ratio * p2).astype(BF16), v_ref[...])
            o = part if o is None else o + part
        o = o * (1.0 / sums[0])
        y = o * lax.rsqrt(jnp.mean(o * o, axis=-1, keepdims=True) + DA_SUBLN_EPS) * sw_ref[...]
        o_ref[rs, :] = (y * out_scale).astype(o_ref.dtype)


def _diff_attention(qkv, lam_full, subln_w, lambda_init, n_batch, seq, lc, ctx_out):
    hw = 2 * DA_HEAD_DIM
    bq, n_sub = DA_Q_BLOCK, DA_Q_SUBBLOCKS
    nq = seq // bq
    t_lat = n_batch * seq
    ctx_blk0 = t_lat // lc
    lam1 = lam_full.reshape(1).astype(F32)
    sw = subln_w.reshape(1, hw)
    common = dict(out_scale=1.0 - lambda_init)
    smem = pl.BlockSpec(memory_space=pltpu.SMEM)
    y_lat = pl.pallas_call(
        functools.partial(_da_kernel, n_sub=n_sub, with_latent=True, **common),
        grid=(n_batch, DA_HEADS, nq),
        in_specs=[
            smem,
            pl.BlockSpec((bq, hw), lambda b, h, i: (b * nq + i, h)),
            pl.BlockSpec((seq, hw), lambda b, h, i: (b, DA_HEADS + h)),
            pl.BlockSpec((lc, hw), lambda b, h, i: (ctx_blk0 + b, DA_HEADS + h)),
            pl.BlockSpec((seq, hw), lambda b, h, i: (b, 2 * DA_HEADS + h)),
            pl.BlockSpec((lc, hw), lambda b, h, i: (ctx_blk0 + b, 2 * DA_HEADS + h)),
            pl.BlockSpec((1, hw), lambda b, h, i: (0, 0)),
        ],
        out_specs=pl.BlockSpec((bq, hw), lambda b, h, i: (b * nq + i, h)),
        out_shape=jax.ShapeDtypeStruct((t_lat, DA_HEADS * hw), BF16),
        compiler_params=_params(3),
        name="diff_attention",
    )(lam1, qkv, qkv, qkv, qkv, qkv, sw)
    if not ctx_out:
        return y_lat, None
    y_ctx = pl.pallas_call(
        functools.partial(_da_kernel, n_sub=1, with_latent=False, **common),
        grid=(n_batch, DA_HEADS),
        in_specs=[
            smem,
            pl.BlockSpec((lc, hw), lambda b, h: (ctx_blk0 + b, h)),
            pl.BlockSpec((lc, hw), lambda b, h: (ctx_blk0 + b, DA_HEADS + h)),
            pl.BlockSpec((lc, hw), lambda b, h: (ctx_blk0 + b, 2 * DA_HEADS + h)),
            pl.BlockSpec((1, hw), lambda b, h: (0, 0)),
        ],
        out_specs=pl.BlockSpec((lc, hw), lambda b, h: (b, h)),
        out_shape=jax.ShapeDtypeStruct((n_batch * lc, DA_HEADS * hw), BF16),
        compiler_params=_params(2),
        name="diff_attention_ctx",
    )(lam1, qkv, qkv, qkv, sw)
    return y_lat, y_ctx


def _conv_kernel(prev_ref, cur_ref, next_ref, w_ref, b_ref, o_ref, *, rb, lat_blocks, lat_blocks_per_seq, ctx_blocks_per_seq):
    i = pl.program_id(0)
    is_lat = i < lat_blocks
    pos = jnp.where(is_lat, i % lat_blocks_per_seq, (i - lat_blocks) % ctx_blocks_per_seq)
    per_seq = jnp.where(is_lat, lat_blocks_per_seq, ctx_blocks_per_seq)
    halo = prev_ref.shape[0]
    prev = jnp.where(pos == 0, 0.0, prev_ref[...].astype(F32))
    nxt = jnp.where(pos == per_seq - 1, 0.0, next_ref[...].astype(F32))
    cat = jnp.concatenate([prev, cur_ref[...].astype(F32), nxt], axis=0)
    n = rb + 2 * halo
    acc = b_ref[...]
    for k in range(SSM_CONV):
        shifted = cat if k == SSM_CONV // 2 else pltpu.roll(cat, (SSM_CONV // 2 - k) % n, 0)
        acc = acc + w_ref[k:k + 1, :] * shifted[halo:halo + rb]
    o_ref[...] = (acc * _sigmoid(acc)).astype(o_ref.dtype)


def _ssm_conv(proj, conv_w, conv_b, n_batch, seq, lc):
    t = proj.shape[0]
    rb, ct, halo = 256, 2048, SUBLANES_BF16
    assert seq % rb == 0 and lc % rb == 0 and SSM_D_INNER % ct == 0 and SSM_CONV_DIM % ct == 0
    col0 = SSM_D_INNER // ct
    hb = rb // halo
    n_halo_blocks = t // halo
    kern = functools.partial(_conv_kernel, rb=rb, lat_blocks=n_batch * seq // rb,
                             lat_blocks_per_seq=seq // rb, ctx_blocks_per_seq=lc // rb)
    return pl.pallas_call(
        kern,
        grid=(t // rb, SSM_CONV_DIM // ct),
        in_specs=[
            pl.BlockSpec((halo, ct), lambda i, j: (jnp.maximum(i * hb - 1, 0), col0 + j)),
            pl.BlockSpec((rb, ct), lambda i, j: (i, col0 + j)),
            pl.BlockSpec((halo, ct), lambda i, j: (jnp.minimum((i + 1) * hb, n_halo_blocks - 1), col0 + j)),
            pl.BlockSpec((SSM_CONV, ct), lambda i, j: (0, j)),
            pl.BlockSpec((1, ct), lambda i, j: (0, j)),
        ],
        out_specs=pl.BlockSpec((rb, ct), lambda i, j: (i, j)),
        out_shape=jax.ShapeDtypeStruct((t, SSM_CONV_DIM), BF16),
        compiler_params=_params(2),
        name="ssm_conv",
    )(proj, proj, proj, conv_w, conv_b.reshape(1, SSM_CONV_DIM))


def _ssd_kernel(xbc_ref, dt_ref, dtb_ref, alog_ref, tri_ref, half_ref, y_ref, state_ref):
    q = SSM_CHUNK
    nh = SSM_HEADS
    p2 = 2 * SSM_HEAD_DIM
    gw = SSM_GROUP_WIDTH
    log2e = math.log2(math.e)
    direction = pl.program_id(1)
    fwd = direction == 0

    @pl.when(pl.program_id(2) == 0)
    def _():
        state_ref[...] = jnp.zeros_like(state_ref)

    pre = dt_ref[...] + dtb_ref[...]
    dt_all = jnp.maximum(pre, 0.0) + jnp.log1p(jnp.exp(-jnp.abs(pre)))
    a_all = dt_all * (jnp.exp(alog_ref[...]) * -log2e)
    tri = tri_ref[0]
    acum_all = sum(_dot(tri, piece) for piece in _split3(a_all))
    acum = jnp.where(fwd, acum_all, pltpu.roll(acum_all, nh, 1))
    acum_t = acum.T
    dt_t = jnp.where(fwd, dt_all, pltpu.roll(dt_all, nh, 1)).T
    src_t = acum_t - jnp.log(dt_t) * log2e
    lane = lax.broadcasted_iota(jnp.int32, (q, q), 1)
    last = jnp.where(fwd, q - 1, 0)
    total = jnp.sum(jnp.where(lane == last, acum_t, 0.0), axis=1, keepdims=True)
    to_end_t = dt_t * jnp.exp2(total - acum_t)
    chunk_decay = jnp.broadcast_to(jnp.exp2(total), (q, q))
    seen = tri.astype(F32) > 0.5
    first_head = lane < SSM_HEAD_DIM
    c_off = SSM_D_INNER + SSM_GROUPS * SSM_STATE

    for g in range(SSM_GROUPS):
        b_g = xbc_ref[:, SSM_D_INNER + g * SSM_STATE:SSM_D_INNER + (g + 1) * SSM_STATE]
        c_g = xbc_ref[:, c_off + g * SSM_STATE:c_off + (g + 1) * SSM_STATE]
        cb = _dot_nt(c_g, b_g)
        b_t = b_g.astype(F32).T
        y_carried = _dot(c_g, state_ref[g].astype(BF16))
        for pr in range(SSM_HEADS_PER_GROUP // 2):
            ps = slice(pr * p2, (pr + 1) * p2)
            cols = slice(g * gw + pr * p2, g * gw + (pr + 1) * p2)
            x2 = xbc_ref[:, cols]
            x_bd = jnp.concatenate([x2 * half_ref[0], x2 * half_ref[1]], axis=0)
            intra, to_state, carry_scale, decay = [], [], [], []
            for r in (2 * pr, 2 * pr + 1):
                hh = g * SSM_HEADS_PER_GROUP + r
                col = jnp.sum(jnp.where(lane == hh, acum, 0.0), axis=1, keepdims=True)
                intra.append((cb * jnp.where(seen, jnp.exp2(col - src_t[hh:hh + 1, :]), 0.0)).astype(BF16))
                to_state.append((b_t * to_end_t[hh:hh + 1, :]).astype(BF16))
                carry_scale.append(jnp.exp2(col))
                decay.append(chunk_decay[hh:hh + 1, :])
            y_ref[0, :, cols] = (_dot(jnp.concatenate(intra, axis=1), x_bd)
                                 + jnp.where(first_head, carry_scale[0], carry_scale[1]) * y_carried[:, ps]
                                 ).astype(y_ref.dtype)
            state_ref[g, :, ps] = (state_ref[g, :, ps] * jnp.where(first_head[:1], decay[0], decay[1])
                                   + _dot(jnp.concatenate(to_state, axis=1), x_bd))


def _ssd_scan(xbc, dt_raw, dt_bias, a_log, n_batch, seq, lc):
    t = xbc.shape[0]
    q = SSM_CHUNK
    n_lat, n_ctx = seq // q, lc // q
    steps = n_lat + n_ctx
    ctx_blk0 = n_batch * n_lat

    def blk(b, d, s):
        ctx_chunk = jnp.where(d == 0, s, n_ctx - 1 - s)
        lat_chunk = jnp.where(d == 0, s - n_ctx, steps - 1 - s)
        return jnp.where(s < n_ctx, ctx_blk0 + b * n_ctx + ctx_chunk, b * n_lat + lat_chunk)

    idx = np.arange(q)
    tri = np.stack([idx[None, :] <= idx[:, None], idx[None, :] >= idx[:, None]]).astype(np.float32)
    first = np.broadcast_to(np.arange(2 * SSM_HEAD_DIM)[None, :] < SSM_HEAD_DIM, (q, 2 * SSM_HEAD_DIM))
    half = np.stack([first, ~first]).astype(np.float32)
    return pl.pallas_call(
        _ssd_kernel,
        grid=(n_batch, 2, steps),
        in_specs=[
            pl.BlockSpec((q, SSM_CONV_DIM), lambda b, d, s: (blk(b, d, s), 0)),
            pl.BlockSpec((q, 2 * SSM_HEADS), lambda b, d, s: (blk(b, d, s), 0)),
            pl.BlockSpec((1, 2 * SSM_HEADS), lambda b, d, s: (0, 0)),
            pl.BlockSpec((1, 2 * SSM_HEADS), lambda b, d, s: (0, 0)),
            pl.BlockSpec((1, q, q), lambda b, d, s: (d, 0, 0)),
            pl.BlockSpec((2, q, 2 * SSM_HEAD_DIM), lambda b, d, s: (0, 0, 0)),
        ],
        out_specs=pl.BlockSpec((1, q, SSM_D_INNER), lambda b, d, s: (d, blk(b, d, s), 0)),
        out_shape=jax.ShapeDtypeStruct((2, t, SSM_D_INNER), BF16),
        scratch_shapes=[pltpu.VMEM((SSM_GROUPS, SSM_STATE, SSM_GROUP_WIDTH), F32)],
        compiler_params=_params(3),
        name="ssd_scan",
    )(xbc, dt_raw, dt_bias.reshape(1, -1), a_log.reshape(1, -1), jnp.asarray(tri, BF16), jnp.asarray(half, BF16))


def _ssm_finish_kernel(yf_ref, yb_ref, x_ref, z_ref, dsk_ref, nw_ref, o_ref):
    gw = SSM_GROUP_WIDTH
    z = z_ref[...].astype(F32)
    y = yf_ref[0].astype(F32) + yb_ref[0].astype(F32) + x_ref[...].astype(F32) * dsk_ref[...]
    y = y * (z * _sigmoid(z))
    for g in range(SSM_GROUPS):
        cs = slice(g * gw, (g + 1) * gw)
        yg = y[:, cs]
        o_ref[:, cs] = (yg * lax.rsqrt(jnp.mean(yg * yg, axis=-1, keepdims=True) + EPS) * nw_ref[:, cs]).astype(o_ref.dtype)


def _ssm_finish(y2, xbc, proj, d_skip, norm_w):
    t = xbc.shape[0]
    di = SSM_D_INNER
    rb = 256
    dsk = jnp.repeat(d_skip.astype(F32), SSM_HEAD_DIM).reshape(1, di)
    return pl.pallas_call(
        _ssm_finish_kernel,
        grid=(t // rb,),
        in_specs=[
            pl.BlockSpec((1, rb, di), lambda i: (0, i, 0)),
            pl.BlockSpec((1, rb, di), lambda i: (1, i, 0)),
            pl.BlockSpec((rb, di), lambda i: (i, 0)),
            pl.BlockSpec((rb, di), lambda i: (i, 0)),
            pl.BlockSpec((1, di), lambda i: (0, 0)),
            pl.BlockSpec((1, di), lambda i: (0, 0)),
        ],
        out_specs=pl.BlockSpec((rb, di), lambda i: (i, 0)),
        out_shape=jax.ShapeDtypeStruct((t, di), BF16),
        compiler_params=_params(1),
        name="ssm_finish",
    )(y2, y2, xbc, proj, dsk, norm_w.reshape(1, di))


def _mamba2(xs, ssq, mod_l, geom, w_in_stack, layer, conv_w, conv_b, dt_bias, a_log, d_skip, norm_w, n_batch, seq, lc):
    t = xs.shape[0]
    n_main = SSM_D_INNER + SSM_CONV_DIM
    proj = _mm_pre(xs, ssq, mod_l, 0, w_in_stack, layer, t, geom, n_cols=n_main)
    dt_raw = _mm_pre(xs, ssq, mod_l, 0, w_in_stack, layer, t, geom, col_start=n_main, n_cols=2 * SSM_HEADS,
                     out_dtype=F32)
    xbc = _ssm_conv(proj, conv_w, conv_b, n_batch, seq, lc)
    y2 = _ssd_scan(xbc, dt_raw, dt_bias, a_log, n_batch, seq, lc)
    return _ssm_finish(y2, xbc, proj, d_skip, norm_w)


def kernel(x, c, ctx, c_ctx, ada_w, ada_b, norm_mix_w, norm_mlp_w, mlp_w1, mlp_w2, na_w_qkv, na_w_o, na_rpb,
           ssm_w_in, ssm_conv_w, ssm_conv_b, ssm_dt_bias, ssm_a_log, ssm_d, ssm_norm_w, ssm_w_out,
           da_w_qkv, da_w_o, da_lambda, da_subln_w, final_norm_w):
    n_batch, seq, d = x.shape
    lc = ctx.shape[1]
    t_lat, t_ctx = n_batch * seq, n_batch * lc
    t = t_lat + t_ctx
    geom = dict(t_lat=t_lat, seq=seq, n_batch=n_batch)
    assert d == D_MODEL and seq % GRID_W == 0 and n_batch < 8

    cond = jnp.concatenate([c, c_ctx[None, :], jnp.zeros((8 - n_batch - 1, d), F32)], axis=0)
    mod = _adaln_all(cond, ada_w, ada_b)
    bufs = _prep(x.reshape(t_lat, d), ctx.reshape(t_ctx, d), norm_mix_w[0], mod[0], geom)
    mlp_w2_bf16 = mlp_w2.astype(BF16)

    for i in range(DEPTH):
        last = i == DEPTH - 1
        mixer, j = i % N_MIXERS, i // N_MIXERS
        y_ctx = None
        mod_l = mod[i]
        rows_out = t_lat if last else t
        _, xs, ssq = bufs
        if mixer == 0:
            qkv = _mm_pre(xs, ssq, mod_l, 0, na_w_qkv, j, t, geom, lead=(NA_HEADS * NA_HEAD_DIM, ATTN_EXP2_SCALE))
            y, y_ctx = _na_attention(qkv, na_rpb[j], n_batch, seq, lc, not last)
            w_o = na_w_o
        elif mixer == 1:
            y = _mamba2(xs, ssq, mod_l, geom, ssm_w_in, j, ssm_conv_w[j], ssm_conv_b[j], ssm_dt_bias[j],
                        ssm_a_log[j], ssm_d[j], ssm_norm_w[j], n_batch, seq, lc)
            w_o = ssm_w_out
        else:
            lambda_init = 0.8 - 0.6 * math.exp(-0.3 * i)
            qk_w = 2 * DA_HEADS * DA_HEAD_DIM
            cos_t, sin_t = _rope_tables(seq, DA_HEAD_DIM)
            qkv = _mm_pre(xs, ssq, mod_l, 0, da_w_qkv, j, t, geom, rope=(cos_t, sin_t, t_lat, 2 * qk_w),
                          lead=(qk_w, ATTN_EXP2_SCALE))
            lam = da_lambda[j].astype(F32)
            lam_full = jnp.exp(jnp.sum(lam[0] * lam[1])) - jnp.exp(jnp.sum(lam[2] * lam[3])) + lambda_init
            y, y_ctx = _diff_attention(qkv, lam_full, da_subln_w[j], lambda_init, n_batch, seq, lc, not last)
            w_o = da_w_o
        w_o = w_o.astype(BF16)
        mlp_norm = (norm_mlp_w[i], mod_l, 4)
        if y_ctx is None:
            bufs = _mm_res(y, w_o, j, bufs, mod_l, 2, rows_out, geom, next_norm=mlp_norm)
        else:
            bufs = _mm_res(y, w_o, j, bufs, mod_l, 2, t_lat, geom, next_norm=mlp_norm)
            bufs = _mm_res(y_ctx, w_o, j, bufs, mod_l, 2, t_ctx, geom, row_off=t_lat, next_norm=mlp_norm)
        hidden = _mm_pre(bufs[1], bufs[2], mod_l, 3, mlp_w1, i, rows_out, geom, act="relu2")
        mixer_norm = None if last else (norm_mix_w[i + 1], mod[i + 1], 1)
        bufs = _mm_res(hidden, mlp_w2_bf16, i, bufs, mod_l, 5, rows_out, geom, next_norm=mixer_norm)

    return _final_norm(bufs[0], final_norm_w, t_lat).reshape(n_batch, seq, d)
```

```python
import functools
import math

import jax
import jax.numpy as jnp
import numpy as np
from jax import lax
from jax.experimental import pallas as pl
from jax.experimental.pallas import tpu as pltpu

F32 = jnp.float32
BF16 = jnp.bfloat16

D_MODEL = 2048
DEPTH = 4
GRID_W = 64
N_MIXERS = 3
EPS = 1e-6

NA_HEADS = 16
NA_HEAD_DIM = 128
NA_WIN_ROWS = 8
NA_WIN_COLS = 16
NA_GROUP_ROWS = 4
NA_KEY_ROWS = 3 * NA_GROUP_ROWS
NA_HEADS_PER_STEP = 8

SSM_D_INNER = 4096
SSM_HEAD_DIM = 64
SSM_HEADS = 64
SSM_GROUPS = 8
SSM_HEADS_PER_GROUP = 8
SSM_STATE = 128
SSM_CONV = 5
SSM_CHUNK = 128
SSM_CONV_DIM = SSM_D_INNER + 2 * SSM_GROUPS * SSM_STATE
SSM_GROUP_WIDTH = SSM_HEADS_PER_GROUP * SSM_HEAD_DIM

DA_HEADS = 8
DA_HEAD_DIM = 128
DA_SUBLN_EPS = 1e-5
DA_Q_BLOCK = 1024
DA_Q_SUBBLOCKS = 4
ROPE_BASE = 10000.0

MASK_NEG = -1e30
assert NA_HEAD_DIM == DA_HEAD_DIM
ATTN_EXP2_SCALE = NA_HEAD_DIM ** -0.5 * math.log2(math.e)

V7X_VMEM_BYTES = 64 * 1024 * 1024
VMEM_LIMIT = (V7X_VMEM_BYTES * 3) // 4
MM_INNER_COLS = 512
MM_RES_INNER_COLS = 256
VMEM_LIMIT_F32_WEIGHTS = (V7X_VMEM_BYTES * 29) // 32
SUBLANES_BF16 = 16
LANES = 128


def _params(n_grid_dims, vmem_limit=VMEM_LIMIT):
    return pltpu.CompilerParams(
        dimension_semantics=("arbitrary",) * n_grid_dims, vmem_limit_bytes=vmem_limit)


def _sigmoid(x):
    return 1.0 / (1.0 + jnp.exp(-x))


def _dot(a, b):
    return jnp.dot(a, b, preferred_element_type=F32)


def _dot_nt(a, b):
    return lax.dot_general(a, b, (((1,), (1,)), ((), ())), preferred_element_type=F32)


def _split3(v):
    hi = v.astype(BF16)
    r1 = v - hi.astype(F32)
    mid = r1.astype(BF16)
    lo = (r1 - mid.astype(F32)).astype(BF16)
    return hi, mid, lo


def _mod_row(row0, t_lat, seq, n_batch):
    return jnp.where(row0 < t_lat, row0 // seq, n_batch)


def _adaln_kernel(c_ref, w_ref, b_ref, o_ref):
    x = c_ref[...]
    x = (x * _sigmoid(x)).astype(BF16)
    o_ref[0] = _dot(x, w_ref[0].astype(BF16)) + b_ref[0]


def _adaln_all(cond, ada_w, ada_b):
    depth, d, n = ada_w.shape
    bn = 1024
    return pl.pallas_call(
        _adaln_kernel,
        grid=(depth, n // bn),
        in_specs=[
            pl.BlockSpec((8, d), lambda l, j: (0, 0)),
            pl.BlockSpec((1, d, bn), lambda l, j: (l, 0, j)),
            pl.BlockSpec((1, 1, bn), lambda l, j: (l, 0, j)),
        ],
        out_specs=pl.BlockSpec((1, 8, bn), lambda l, j: (l, 0, j)),
        out_shape=jax.ShapeDtypeStruct((depth, 8, n), F32),
        compiler_params=_params(2),
        name="adaln",
    )(cond, ada_w, ada_b.reshape(depth, 1, n))


def _final_norm_kernel(h_ref, nw_ref, o_ref):
    x = h_ref[...]
    o_ref[...] = x * lax.rsqrt(jnp.mean(x * x, axis=-1, keepdims=True) + EPS) * nw_ref[...]


def _final_norm(h, norm_w, m_rows):
    d = h.shape[1]
    bm = 256
    return pl.pallas_call(
        _final_norm_kernel,
        grid=(m_rows // bm,),
        in_specs=[pl.BlockSpec((bm, d), lambda i: (i, 0)), pl.BlockSpec((1, d), lambda i: (0, 0))],
        out_specs=pl.BlockSpec((bm, d), lambda i: (i, 0)),
        out_shape=jax.ShapeDtypeStruct((m_rows, d), F32),
        compiler_params=_params(1),
        name="final_norm",
    )(h, norm_w.reshape(1, d))


def _mm_blocks(m_rows, k, n):
    bm = 1024 if m_rows % 1024 == 0 else 256
    bn = 1024 if n % 1024 == 0 else (512 if n % 512 == 0 else 128)
    if k >= 8192:
        bm, bn = min(bm, 512), min(bn, 512)
    elif k >= 4096:
        bn = min(bn, 512)
    return bm, bn


def _accumulate_over_columns(ssq_ref, part, j):
    @pl.when(j == 0)
    def _():
        ssq_ref[...] = part

    @pl.when(j > 0)
    def _():
        ssq_ref[...] = ssq_ref[...] + part


def _next_norm_outputs(hb, j, midx, nw_ref, nscale_ref, xs_ref, ssq_ref):
    g = nw_ref[...] * (1.0 + nscale_ref[pl.ds(midx, 1), :])
    xs_ref[...] = (hb * g).astype(xs_ref.dtype)
    sq = hb * hb
    part = functools.reduce(jnp.add, [sq[:, c:c + LANES] for c in range(0, sq.shape[1], LANES)])
    _accumulate_over_columns(ssq_ref, part, j)


def _prep_kernel(x_ref, c_ref, nw_ref, nscale_ref, h_ref, xs_ref, ssq_ref, *, bm, lat_blocks, t_lat, seq, n_batch):
    i, j = pl.program_id(0), pl.program_id(1)
    midx = _mod_row(i * bm, t_lat, seq, n_batch)

    def emit(src_ref):
        hb = src_ref[...]
        h_ref[...] = hb
        _next_norm_outputs(hb, j, midx, nw_ref, nscale_ref, xs_ref, ssq_ref)

    pl.when(i < lat_blocks)(lambda: emit(x_ref))
    pl.when(i >= lat_blocks)(lambda: emit(c_ref))


def _prep(x2d, c2d, norm_w, mod_next, geom):
    t_lat, d = x2d.shape
    t_ctx = c2d.shape[0]
    t = t_lat + t_ctx
    bm = 512 if t_ctx % 512 == 0 else 256
    bn = 1024
    lat_blocks = t_lat // bm
    sc0 = d // bn
    return pl.pallas_call(
        functools.partial(_prep_kernel, bm=bm, lat_blocks=lat_blocks, **geom),
        grid=(t // bm, d // bn),
        in_specs=[
            pl.BlockSpec((bm, bn), lambda i, j: (jnp.minimum(i, lat_blocks - 1), j)),
            pl.BlockSpec((bm, bn), lambda i, j: (jnp.maximum(i - lat_blocks, 0), j)),
            pl.BlockSpec((1, bn), lambda i, j: (0, j)),
            pl.BlockSpec((8, bn), lambda i, j: (0, sc0 + j)),
        ],
        out_specs=[
            pl.BlockSpec((bm, bn), lambda i, j: (i, j)),
            pl.BlockSpec((bm, bn), lambda i, j: (i, j)),
            pl.BlockSpec((bm, LANES), lambda i, j: (i, 0)),
        ],
        out_shape=[
            jax.ShapeDtypeStruct((t, d), F32),
            jax.ShapeDtypeStruct((t, d), BF16),
            jax.ShapeDtypeStruct((t, LANES), F32),
        ],
        compiler_params=_params(2),
        name="prep",
    )(x2d, c2d, norm_w.reshape(1, d), mod_next)


def _mm_pre_kernel(*refs, bm, d_norm, shift_k, act, rope_row_blocks, rope_col_blocks, lead_blocks, lead_scale,
                   t_lat, seq, n_batch):
    xs_ref, ssq_ref, mod_ref, w_ref = refs[:4]
    pos = 4
    if rope_col_blocks:
        cos_ref, sin_ref = refs[pos:pos + 2]
        pos += 2
    o_ref, sw_ref = refs[pos], refs[pos + 1]
    i, j = pl.program_id(0), pl.program_id(1)
    bn = o_ref.shape[1]
    hn = min(bn, MM_INNER_COLS)

    @pl.when(i == 0)
    def _():
        shift = mod_ref[:, shift_k * d_norm:(shift_k + 1) * d_norm].astype(BF16)
        for c0 in range(0, bn, hn):
            sw_ref[j, :, c0:c0 + hn] = _dot(shift, w_ref[:, c0:c0 + hn].astype(BF16))

    midx = _mod_row(i * bm, t_lat, seq, n_batch)
    r = lax.rsqrt(jnp.sum(ssq_ref[...], axis=1, keepdims=True) * (1.0 / d_norm) + EPS)
    lead = jnp.where(j < lead_blocks, lead_scale, 1.0) if lead_blocks else 1.0
    r = r * lead
    do_rope = (i < rope_row_blocks) & (j < rope_col_blocks)
    for c0 in range(0, bn, hn):
        acc = _dot(xs_ref[...], w_ref[:, c0:c0 + hn].astype(BF16))
        acc = acc * r + sw_ref[j, pl.ds(midx, 1), c0:c0 + hn] * lead
        if act == "relu2":
            acc = jnp.maximum(acc, 0.0)
            acc = acc * acc
        if not rope_col_blocks:
            o_ref[:, c0:c0 + hn] = acc.astype(o_ref.dtype)
            continue

        @pl.when(do_rope)
        def _(acc=acc, c0=c0):
            hd = cos_ref.shape[1]
            lane = lax.broadcasted_iota(jnp.int32, (bm, hd), 1)
            first = (lane % (hd // 2)) < hd // 4
            for c in range(hn // hd):
                xh = acc[:, c * hd:(c + 1) * hd]
                partner = jnp.where(first, pltpu.roll(xh, hd - hd // 4, 1), pltpu.roll(xh, hd // 4, 1))
                o_ref[:, c0 + c * hd:c0 + (c + 1) * hd] = (xh * cos_ref[...] + partner * sin_ref[...]).astype(o_ref.dtype)

        @pl.when(jnp.logical_not(do_rope))
        def _(acc=acc, c0=c0):
            o_ref[:, c0:c0 + hn] = acc.astype(o_ref.dtype)


def _mm_pre(xs, ssq, mod_l, shift_k, w_stack, layer, m_rows, geom, act=None, rope=None, col_start=0, n_cols=None,
            out_dtype=BF16, lead=None):
    d = xs.shape[1]
    k = w_stack.shape[1]
    n = n_cols or w_stack.shape[2]
    bm, bn = _mm_blocks(m_rows, k, n)
    assert col_start % bn == 0
    cb0 = col_start // bn
    kern = functools.partial(
        _mm_pre_kernel, bm=bm, d_norm=d, shift_k=shift_k, act=act,
        lead_blocks=(lead[0] // bn if lead else 0), lead_scale=(lead[1] if lead else 1.0),
        rope_row_blocks=(rope[2] // bm if rope else 0), rope_col_blocks=(rope[3] // bn if rope else 0), **geom)
    in_specs = [
        pl.BlockSpec((bm, d), lambda i, j: (i, 0)),
        pl.BlockSpec((bm, LANES), lambda i, j: (i, 0)),
        pl.BlockSpec(mod_l.shape, lambda i, j: (0, 0)),
        pl.BlockSpec((None, k, bn), lambda i, j: (layer, 0, cb0 + j)),
    ]
    args = [xs, ssq, mod_l, w_stack]
    if rope:
        seq_blocks = rope[0].shape[0] // bm
        hd = rope[0].shape[1]
        in_specs += [pl.BlockSpec((bm, hd), lambda i, j: (i % seq_blocks, 0))] * 2
        args += [rope[0], rope[1]]
    return pl.pallas_call(
        kern,
        grid=(m_rows // bm, n // bn),
        in_specs=in_specs,
        out_specs=pl.BlockSpec((bm, bn), lambda i, j: (i, j)),
        out_shape=jax.ShapeDtypeStruct((m_rows, n), out_dtype),
        scratch_shapes=[pltpu.VMEM((n // bn, 8, bn), F32)],
        compiler_params=_params(2, VMEM_LIMIT_F32_WEIGHTS),
        name="mm_pre" + ("_" + act if act else "") + ("_rope" if rope else ""),
    )(*args)


def _mm_res_kernel(*refs, bm, n_k, row_off, emit_next, t_lat, seq, n_batch):
    x_ref, w_ref, res_ref, gate_ref = refs[:4]
    pos = 4
    if emit_next:
        nw_ref, nscale_ref = refs[pos:pos + 2]
        pos += 4
    o_ref = refs[pos]
    pos += 1
    if emit_next:
        xs_ref, ssq_ref = refs[pos:pos + 2]
        pos += 2
    acc_ref = refs[pos].at[pl.program_id(2)] if n_k > 1 else None
    i, kk, j = pl.program_id(0), pl.program_id(1), pl.program_id(2)
    midx = _mod_row(row_off + i * bm, t_lat, seq, n_batch)
    bn = o_ref.shape[1]
    slices = [slice(c0, c0 + MM_RES_INNER_COLS) for c0 in range(0, bn, MM_RES_INNER_COLS)]
    parts = [_dot(x_ref[...], w_ref[:, cs]) for cs in slices]

    def finish(accs):
        sums = None
        for cs, acc in zip(slices, accs):
            hb = res_ref[:, cs] + gate_ref[pl.ds(midx, 1), cs] * acc
            o_ref[:, cs] = hb
            if emit_next:
                g = nw_ref[:, cs] * (1.0 + nscale_ref[pl.ds(midx, 1), cs])
                xs_ref[:, cs] = (hb * g).astype(xs_ref.dtype)
                sq = hb * hb
                for c in range(0, sq.shape[1], LANES):
                    sums = sq[:, c:c + LANES] if sums is None else sums + sq[:, c:c + LANES]
        if emit_next:
            _accumulate_over_columns(ssq_ref, sums, j)

    if n_k == 1:
        finish(parts)
        return

    @pl.when(kk == 0)
    def _():
        for cs, part in zip(slices, parts):
            acc_ref[:, cs] = part

    @pl.when((kk > 0) & (kk < n_k - 1))
    def _():
        for cs, part in zip(slices, parts):
            acc_ref[:, cs] = acc_ref[:, cs] + part

    @pl.when(kk == n_k - 1)
    def _():
        finish([acc_ref[:, cs] + part for cs, part in zip(slices, parts)])


def _mm_res(x, w_stack, layer, bufs, mod_l, gate_k, m_rows, geom, row_off=0, next_norm=None):
    h, xs, ssq = bufs
    k, n = w_stack.shape[1:]
    n_k = 2 if k >= 8192 else 1
    kb = k // n_k
    bm = 1024 if m_rows % 1024 == 0 else 256
    bn = 512 if k >= 4096 else 1024
    assert row_off % bm == 0 and n % bn == 0
    blk_off = row_off // bm
    nb = n // bn
    emit_next = next_norm is not None

    def out_col(kk, j):
        return jnp.where(kk == n_k - 1, j, 0)

    in_specs = [
        pl.BlockSpec((bm, kb), lambda i, kk, j: (i, kk)),
        pl.BlockSpec((None, kb, bn), lambda i, kk, j: (layer, kk, j)),
        pl.BlockSpec((bm, bn), lambda i, kk, j: (blk_off + i, out_col(kk, j))),
        pl.BlockSpec((8, bn), lambda i, kk, j: (0, gate_k * nb + j)),
    ]
    args = [x, w_stack, h, mod_l]
    out_specs = [pl.BlockSpec((bm, bn), lambda i, kk, j: (blk_off + i, out_col(kk, j)))]
    out_shape = [jax.ShapeDtypeStruct(h.shape, F32)]
    aliases = {2: 0}
    if emit_next:
        nw, mod_next, scale_slot = next_norm
        in_specs += [
            pl.BlockSpec((1, bn), lambda i, kk, j: (0, j)),
            pl.BlockSpec((8, bn), lambda i, kk, j: (0, scale_slot * nb + j)),
            pl.BlockSpec(memory_space=pl.ANY),
            pl.BlockSpec(memory_space=pl.ANY),
        ]
        args += [nw.reshape(1, n), mod_next, xs, ssq]
        out_specs += [
            pl.BlockSpec((bm, bn), lambda i, kk, j: (blk_off + i, out_col(kk, j))),
            pl.BlockSpec((bm, LANES), lambda i, kk, j: (blk_off + i, 0)),
        ]
        out_shape += [jax.ShapeDtypeStruct(xs.shape, BF16), jax.ShapeDtypeStruct(ssq.shape, F32)]
        aliases.update({6: 1, 7: 2})
    outs = pl.pallas_call(
        functools.partial(_mm_res_kernel, bm=bm, n_k=n_k, row_off=row_off, emit_next=emit_next, **geom),
        grid=(m_rows // bm, n_k, nb),
        in_specs=in_specs,
        out_specs=out_specs,
        out_shape=out_shape,
        scratch_shapes=[pltpu.VMEM((nb, bm, bn), F32)] if n_k > 1 else [],
        input_output_aliases=aliases,
        compiler_params=_params(3),
        name="mm_res",
    )(*args)
    return (outs[0], outs[1], outs[2]) if emit_next else (outs[0], xs, ssq)


def _na_pair_layout(rows):
    gq, gk = NA_GROUP_ROWS, NA_KEY_ROWS
    pairs = {}
    idx = np.zeros((3, gq, gk // 2), np.int32)
    for typ, (r0, ks) in enumerate(((0, 0), (gq, 0), (rows - gq, rows - gk))):
        for i in range(gq):
            r = r0 + i
            rs = min(max(r - NA_WIN_ROWS // 2, 0), rows - NA_WIN_ROWS)
            dr = [kr - r + NA_WIN_ROWS - 1 if rs <= kr < rs + NA_WIN_ROWS else -1 for kr in range(ks, ks + gk)]
            for p in range(gk // 2):
                idx[typ, i, p] = pairs.setdefault((dr[2 * p], dr[2 * p + 1]), len(pairs))
    return list(pairs), idx


def _na_pair_table(rpb, pairs, out_scale):
    heads = rpb.shape[0]
    col = np.arange(GRID_W)
    col_start = np.clip(col - NA_WIN_COLS // 2, 0, GRID_W - NA_WIN_COLS)
    col_mask = (col[None, :] >= col_start[:, None]) & (col[None, :] < col_start[:, None] + NA_WIN_COLS)
    dc_idx = np.clip(col[None, :] - col[:, None], -(NA_WIN_COLS - 1), NA_WIN_COLS - 1) + NA_WIN_COLS - 1
    onehot = (dc_idx[None] == np.arange(2 * NA_WIN_COLS - 1)[:, None, None]).astype(np.float32)
    per_dr = jnp.einsum("hrd,dqk->hrqk", rpb.astype(F32), onehot, precision=lax.Precision.HIGHEST)
    per_dr = jnp.where(col_mask[None, None], per_dr * out_scale, MASK_NEG)
    masked = jnp.full((heads, GRID_W, GRID_W), MASK_NEG, F32)
    block = lambda dr: masked if dr < 0 else per_dr[:, dr]
    return jnp.stack([jnp.concatenate([block(a), block(b)], axis=-1) for a, b in pairs], axis=1)


def _softmax_pv(scores, values, out_dtype):
    m = functools.reduce(jnp.maximum, [jnp.max(t, axis=1, keepdims=True) for t in scores])
    dh = values[0].shape[1]
    o = None
    for t, v in zip(scores, values):
        p = jnp.exp2(t - m).astype(BF16)
        part = _dot(p, jnp.concatenate([v, jnp.ones_like(v)], axis=1))
        o = part if o is None else o + part
    return (o[:, :dh] * (1.0 / o[:, dh:])).astype(out_dtype)


def _na_kernel(idx_ref, q_ref, k0_ref, k1_ref, k2_ref, v0_ref, v1_ref, v2_ref, kc_ref, vc_ref, pair_ref, o_ref,
               *, n_groups):
    dh = NA_HEAD_DIM
    gq, n_pairs = NA_GROUP_ROWS, NA_KEY_ROWS // 2
    g = pl.program_id(1)
    typ = jnp.where(g == 0, 0, jnp.where(g == n_groups - 1, 2, 1))
    for hh in range(NA_HEADS_PER_STEP):
        sl = slice(hh * dh, (hh + 1) * dh)
        q = q_ref[:, sl]
        bias = jnp.concatenate(
            [jnp.concatenate([pair_ref[hh, idx_ref[(typ * gq + i) * n_pairs + p]] for p in range(n_pairs)], axis=1)
             for i in range(gq)], axis=0)
        s_loc = jnp.concatenate([_dot_nt(q, kr[:, sl]) for kr in (k0_ref, k1_ref, k2_ref)], axis=1)
        t_loc = s_loc + bias
        t_ctx = _dot_nt(q, kc_ref[:, sl])
        kb = k0_ref.shape[0]
        scores = [t_loc[:, j * kb:(j + 1) * kb] for j in range(3)] + [t_ctx]
        values = [vr[:, sl] for vr in (v0_ref, v1_ref, v2_ref, vc_ref)]
        o_ref[:, sl] = _softmax_pv(scores, values, o_ref.dtype)


def _na_ctx_kernel(q_ref, kc_ref, vc_ref, o_ref):
    dh = NA_HEAD_DIM
    for hh in range(NA_HEADS_PER_STEP):
        sl = slice(hh * dh, (hh + 1) * dh)
        t_ctx = _dot_nt(q_ref[:, sl], kc_ref[:, sl])
        o_ref[:, sl] = _softmax_pv([t_ctx], [vc_ref[:, sl]], o_ref.dtype)


def _na_attention(qkv, rpb, n_batch, seq, lc, ctx_out):
    hd = NA_HEADS * NA_HEAD_DIM
    gtok = NA_GROUP_ROWS * GRID_W
    ng = seq // gtok
    assert lc == gtok and ng >= 4
    hw = NA_HEADS_PER_STEP * NA_HEAD_DIM
    nhq = NA_HEADS // NA_HEADS_PER_STEP
    t_lat = n_batch * seq
    ctx_blk0 = n_batch * ng
    pairs, idx = _na_pair_layout(seq // GRID_W)
    pair_tab = _na_pair_table(rpb, pairs, math.log2(math.e))

    def k_row(g, b, j):
        return b * ng + jnp.clip(g - 1, 0, ng - 3) + j

    blk = (gtok, hw)
    in_specs = [pl.BlockSpec(memory_space=pltpu.SMEM), pl.BlockSpec(blk, lambda h, g, b: (b * ng + g, h))]
    for col0 in (nhq, 2 * nhq):
        for j in range(3):
            in_specs.append(pl.BlockSpec(blk, lambda h, g, b, j=j, col0=col0: (k_row(g, b, j), col0 + h)))
    in_specs.append(pl.BlockSpec(blk, lambda h, g, b: (ctx_blk0 + b, nhq + h)))
    in_specs.append(pl.BlockSpec(blk, lambda h, g, b: (ctx_blk0 + b, 2 * nhq + h)))
    in_specs.append(pl.BlockSpec((NA_HEADS_PER_STEP, len(pairs), GRID_W, 2 * GRID_W), lambda h, g, b: (h, 0, 0, 0)))
    y_lat = pl.pallas_call(
        functools.partial(_na_kernel, n_groups=ng),
        grid=(nhq, ng, n_batch),
        in_specs=in_specs,
        out_specs=pl.BlockSpec(blk, lambda h, g, b: (b * ng + g, h)),
        out_shape=jax.ShapeDtypeStruct((t_lat, hd), BF16),
        compiler_params=_params(3),
        name="na_attention",
    )(jnp.asarray(idx.reshape(-1)), *([qkv] * 9), pair_tab)
    if not ctx_out:
        return y_lat, None
    y_ctx = pl.pallas_call(
        _na_ctx_kernel,
        grid=(nhq, n_batch),
        in_specs=[
            pl.BlockSpec(blk, lambda h, b: (ctx_blk0 + b, h)),
            pl.BlockSpec(blk, lambda h, b: (ctx_blk0 + b, nhq + h)),
            pl.BlockSpec(blk, lambda h, b: (ctx_blk0 + b, 2 * nhq + h)),
        ],
        out_specs=pl.BlockSpec(blk, lambda h, b: (b, h)),
        out_shape=jax.ShapeDtypeStruct((n_batch * lc, hd), BF16),
        compiler_params=_params(2),
        name="na_attention_ctx",
    )(qkv, qkv, qkv)
    return y_lat, y_ctx


def _rope_tables(seq, width):
    t = jnp.arange(seq)
    pos = jnp.stack([t // GRID_W, t % GRID_W], axis=-1).astype(F32)
    n_freq = DA_HEAD_DIM // 4
    inv_freq = ROPE_BASE ** (-jnp.arange(n_freq, dtype=F32) / n_freq)
    ang = pos[:, :, None] * inv_freq
    cos = jnp.broadcast_to(jnp.cos(ang)[:, :, None, :], (seq, 2, 2, n_freq)).reshape(seq, DA_HEAD_DIM)
    sign = jnp.array([-1.0, 1.0], F32)[None, None, :, None]
    sin = (jnp.broadcast_to(jnp.sin(ang)[:, :, None, :], (seq, 2, 2, n_freq)) * sign).reshape(seq, DA_HEAD_DIM)
    reps = width // DA_HEAD_DIM
    return jnp.tile(cos, (1, reps)), jnp.tile(sin, (1, reps))


def _da_kernel(lam_ref, q_ref, *refs, out_scale, n_sub, with_latent):
    n_kv = 2 if with_latent else 1
    k_refs, v_refs = refs[:n_kv], refs[n_kv:2 * n_kv]
    sw_ref, o_ref = refs[2 * n_kv], refs[2 * n_kv + 1]
    dh = DA_HEAD_DIM
    lam = lam_ref[0]
    rows = q_ref.shape[0] // n_sub
    for sub in range(n_sub):
        rs = slice(sub * rows, (sub + 1) * rows)
        probs, sums = [], []
        for comp in range(2):
            sl = slice(comp * dh, (comp + 1) * dh)
            q = q_ref[rs, sl]
            s = [_dot_nt(q, k_ref[:, sl]) for k_ref in k_refs]
            m = functools.reduce(jnp.maximum, [jnp.max(x, axis=1, keepdims=True) for x in s])
            p = [jnp.exp2(x - m) for x in s]
            probs.append(p)
            sums.append(functools.reduce(jnp.add, [jnp.sum(x, axis=1, keepdims=True) for x in p]))
        ratio = lam * sums[0] / sums[1]
        o = None
        for p1, p2, v_ref in zip(probs[0], probs[1], v_refs):
            part = _dot((p1 - ratio * p2).astype(BF16), v_ref[...])
            o = part if o is None else o + part
        o = o * (1.0 / sums[0])
        y = o * lax.rsqrt(jnp.mean(o * o, axis=-1, keepdims=True) + DA_SUBLN_EPS) * sw_ref[...]
        o_ref[rs, :] = (y * out_scale).astype(o_ref.dtype)


def _diff_attention(qkv, lam_full, subln_w, lambda_init, n_batch, seq, lc, ctx_out):
    hw = 2 * DA_HEAD_DIM
    bq, n_sub = DA_Q_BLOCK, DA_Q_SUBBLOCKS
    nq = seq // bq
    t_lat = n_batch * seq
    ctx_blk0 = t_lat // lc
    lam1 = lam_full.reshape(1).astype(F32)
    sw = subln_w.reshape(1, hw)
    common = dict(out_scale=1.0 - lambda_init)
    smem = pl.BlockSpec(memory_space=pltpu.SMEM)
    y_lat = pl.pallas_call(
        functools.partial(_da_kernel, n_sub=n_sub, with_latent=True, **common),
        grid=(n_batch, DA_HEADS, nq),
        in_specs=[
            smem,
            pl.BlockSpec((bq, hw), lambda b, h, i: (b * nq + i, h)),
            pl.BlockSpec((seq, hw), lambda b, h, i: (b, DA_HEADS + h)),
            pl.BlockSpec((lc, hw), lambda b, h, i: (ctx_blk0 + b, DA_HEADS + h)),
            pl.BlockSpec((seq, hw), lambda b, h, i: (b, 2 * DA_HEADS + h)),
            pl.BlockSpec((lc, hw), lambda b, h, i: (ctx_blk0 + b, 2 * DA_HEADS + h)),
            pl.BlockSpec((1, hw), lambda b, h, i: (0, 0)),
        ],
        out_specs=pl.BlockSpec((bq, hw), lambda b, h, i: (b * nq + i, h)),
        out_shape=jax.ShapeDtypeStruct((t_lat, DA_HEADS * hw), BF16),
        compiler_params=_params(3),
        name="diff_attention",
    )(lam1, qkv, qkv, qkv, qkv, qkv, sw)
    if not ctx_out:
        return y_lat, None
    y_ctx = pl.pallas_call(
        functools.partial(_da_kernel, n_sub=1, with_latent=False, **common),
        grid=(n_batch, DA_HEADS),
        in_specs=[
            smem,
            pl.BlockSpec((lc, hw), lambda b, h: (ctx_blk0 + b, h)),
            pl.BlockSpec((lc, hw), lambda b, h: (ctx_blk0 + b, DA_HEADS + h)),
            pl.BlockSpec((lc, hw), lambda b, h: (ctx_blk0 + b, 2 * DA_HEADS + h)),
            pl.BlockSpec((1, hw), lambda b, h: (0, 0)),
        ],
        out_specs=pl.BlockSpec((lc, hw), lambda b, h: (b, h)),
        out_shape=jax.ShapeDtypeStruct((n_batch * lc, DA_HEADS * hw), BF16),
        compiler_params=_params(2),
        name="diff_attention_ctx",
    )(lam1, qkv, qkv, qkv, sw)
    return y_lat, y_ctx


def _conv_kernel(prev_ref, cur_ref, next_ref, w_ref, b_ref, o_ref, *, rb, lat_blocks, lat_blocks_per_seq, ctx_blocks_per_seq):
    i = pl.program_id(0)
    is_lat = i < lat_blocks
    pos = jnp.where(is_lat, i % lat_blocks_per_seq, (i - lat_blocks) % ctx_blocks_per_seq)
    per_seq = jnp.where(is_lat, lat_blocks_per_seq, ctx_blocks_per_seq)
    halo = prev_ref.shape[0]
    prev = jnp.where(pos == 0, 0.0, prev_ref[...].astype(F32))
    nxt = jnp.where(pos == per_seq - 1, 0.0, next_ref[...].astype(F32))
    cat = jnp.concatenate([prev, cur_ref[...].astype(F32), nxt], axis=0)
    n = rb + 2 * halo
    acc = b_ref[...]
    for k in range(SSM_CONV):
        shifted = cat if k == SSM_CONV // 2 else pltpu.roll(cat, (SSM_CONV // 2 - k) % n, 0)
        acc = acc + w_ref[k:k + 1, :] * shifted[halo:halo + rb]
    o_ref[...] = (acc * _sigmoid(acc)).astype(o_ref.dtype)


def _ssm_conv(proj, conv_w, conv_b, n_batch, seq, lc):
    t = proj.shape[0]
    rb, ct, halo = 256, 2048, SUBLANES_BF16
    assert seq % rb == 0 and lc % rb == 0 and SSM_D_INNER % ct == 0 and SSM_CONV_DIM % ct == 0
    col0 = SSM_D_INNER // ct
    hb = rb // halo
    n_halo_blocks = t // halo
    kern = functools.partial(_conv_kernel, rb=rb, lat_blocks=n_batch * seq // rb,
                             lat_blocks_per_seq=seq // rb, ctx_blocks_per_seq=lc // rb)
    return pl.pallas_call(
        kern,
        grid=(t // rb, SSM_CONV_DIM // ct),
        in_specs=[
            pl.BlockSpec((halo, ct), lambda i, j: (jnp.maximum(i * hb - 1, 0), col0 + j)),
            pl.BlockSpec((rb, ct), lambda i, j: (i, col0 + j)),
            pl.BlockSpec((halo, ct), lambda i, j: (jnp.minimum((i + 1) * hb, n_halo_blocks - 1), col0 + j)),
            pl.BlockSpec((SSM_CONV, ct), lambda i, j: (0, j)),
            pl.BlockSpec((1, ct), lambda i, j: (0, j)),
        ],
        out_specs=pl.BlockSpec((rb, ct), lambda i, j: (i, j)),
        out_shape=jax.ShapeDtypeStruct((t, SSM_CONV_DIM), BF16),
        compiler_params=_params(2),
        name="ssm_conv",
    )(proj, proj, proj, conv_w, conv_b.reshape(1, SSM_CONV_DIM))


def _ssd_kernel(xbc_ref, dt_ref, dtb_ref, alog_ref, tri_ref, half_ref, y_ref, state_ref):
    q = SSM_CHUNK
    nh = SSM_HEADS
    p2 = 2 * SSM_HEAD_DIM
    gw = SSM_GROUP_WIDTH
    log2e = math.log2(math.e)
    direction = pl.program_id(1)
    fwd = direction == 0

    @pl.when(pl.program_id(2) == 0)
    def _():
        state_ref[...] = jnp.zeros_like(state_ref)

    pre = dt_ref[...] + dtb_ref[...]
    dt_all = jnp.maximum(pre, 0.0) + jnp.log1p(jnp.exp(-jnp.abs(pre)))
    a_all = dt_all * (jnp.exp(alog_ref[...]) * -log2e)
    tri = tri_ref[0]
    acum_all = sum(_dot(tri, piece) for piece in _split3(a_all))
    acum = jnp.where(fwd, acum_all, pltpu.roll(acum_all, nh, 1))
    acum_t = acum.T
    dt_t = jnp.where(fwd, dt_all, pltpu.roll(dt_all, nh, 1)).T
    src_t = acum_t - jnp.log(dt_t) * log2e
    lane = lax.broadcasted_iota(jnp.int32, (q, q), 1)
    last = jnp.where(fwd, q - 1, 0)
    total = jnp.sum(jnp.where(lane == last, acum_t, 0.0), axis=1, keepdims=True)
    to_end_t = dt_t * jnp.exp2(total - acum_t)
    chunk_decay = jnp.broadcast_to(jnp.exp2(total), (q, q))
    seen = tri.astype(F32) > 0.5
    first_head = lane < SSM_HEAD_DIM
    c_off = SSM_D_INNER + SSM_GROUPS * SSM_STATE

    for g in range(SSM_GROUPS):
        b_g = xbc_ref[:, SSM_D_INNER + g * SSM_STATE:SSM_D_INNER + (g + 1) * SSM_STATE]
        c_g = xbc_ref[:, c_off + g * SSM_STATE:c_off + (g + 1) * SSM_STATE]
        cb = _dot_nt(c_g, b_g)
        b_t = b_g.astype(F32).T
        y_carried = _dot(c_g, state_ref[g].astype(BF16))
        for pr in range(SSM_HEADS_PER_GROUP // 2):
            ps = slice(pr * p2, (pr + 1) * p2)
            cols = slice(g * gw + pr * p2, g * gw + (pr + 1) * p2)
            x2 = xbc_ref[:, cols]
            x_bd = jnp.concatenate([x2 * half_ref[0], x2 * half_ref[1]], axis=0)
            intra, to_state, carry_scale, decay = [], [], [], []
            for r in (2 * pr, 2 * pr + 1):
                hh = g * SSM_HEADS_PER_GROUP + r
                col = jnp.sum(jnp.where(lane == hh, acum, 0.0), axis=1, keepdims=True)
                intra.append((cb * jnp.where(seen, jnp.exp2(col - src_t[hh:hh + 1, :]), 0.0)).astype(BF16))
                to_state.append((b_t * to_end_t[hh:hh + 1, :]).astype(BF16))
                carry_scale.append(jnp.exp2(col))
                decay.append(chunk_decay[hh:hh + 1, :])
            y_ref[0, :, cols] = (_dot(jnp.concatenate(intra, axis=1), x_bd)
                                 + jnp.where(first_head, carry_scale[0], carry_scale[1]) * y_carried[:, ps]
                                 ).astype(y_ref.dtype)
            state_ref[g, :, ps] = (state_ref[g, :, ps] * jnp.where(first_head[:1], decay[0], decay[1])
                                   + _dot(jnp.concatenate(to_state, axis=1), x_bd))


def _ssd_scan(xbc, dt_raw, dt_bias, a_log, n_batch, seq, lc):
    t = xbc.shape[0]
    q = SSM_CHUNK
    n_lat, n_ctx = seq // q, lc // q
    steps = n_lat + n_ctx
    ctx_blk0 = n_batch * n_lat

    def blk(b, d, s):
        ctx_chunk = jnp.where(d == 0, s, n_ctx - 1 - s)
        lat_chunk = jnp.where(d == 0, s - n_ctx, steps - 1 - s)
        return jnp.where(s < n_ctx, ctx_blk0 + b * n_ctx + ctx_chunk, b * n_lat + lat_chunk)

    idx = np.arange(q)
    tri = np.stack([idx[None, :] <= idx[:, None], idx[None, :] >= idx[:, None]]).astype(np.float32)
    first = np.broadcast_to(np.arange(2 * SSM_HEAD_DIM)[None, :] < SSM_HEAD_DIM, (q, 2 * SSM_HEAD_DIM))
    half = np.stack([first, ~first]).astype(np.float32)
    return pl.pallas_call(
        _ssd_kernel,
        grid=(n_batch, 2, steps),
        in_specs=[
            pl.BlockSpec((q, SSM_CONV_DIM), lambda b, d, s: (blk(b, d, s), 0)),
            pl.BlockSpec((q, 2 * SSM_HEADS), lambda b, d, s: (blk(b, d, s), 0)),
            pl.BlockSpec((1, 2 * SSM_HEADS), lambda b, d, s: (0, 0)),
            pl.BlockSpec((1, 2 * SSM_HEADS), lambda b, d, s: (0, 0)),
            pl.BlockSpec((1, q, q), lambda b, d, s: (d, 0, 0)),
            pl.BlockSpec((2, q, 2 * SSM_HEAD_DIM), lambda b, d, s: (0, 0, 0)),
        ],
        out_specs=pl.BlockSpec((1, q, SSM_D_INNER), lambda b, d, s: (d, blk(b, d, s), 0)),
        out_shape=jax.ShapeDtypeStruct((2, t, SSM_D_INNER), BF16),
        scratch_shapes=[pltpu.VMEM((SSM_GROUPS, SSM_STATE, SSM_GROUP_WIDTH), F32)],
        compiler_params=_params(3),
        name="ssd_scan",
    )(xbc, dt_raw, dt_bias.reshape(1, -1), a_log.reshape(1, -1), jnp.asarray(tri, BF16), jnp.asarray(half, BF16))


def _ssm_finish_kernel(yf_ref, yb_ref, x_ref, z_ref, dsk_ref, nw_ref, o_ref):
    gw = SSM_GROUP_WIDTH
    z = z_ref[...].astype(F32)
    y = yf_ref[0].astype(F32) + yb_ref[0].astype(F32) + x_ref[...].astype(F32) * dsk_ref[...]
    y = y * (z * _sigmoid(z))
    for g in range(SSM_GROUPS):
        cs = slice(g * gw, (g + 1) * gw)
        yg = y[:, cs]
        o_ref[:, cs] = (yg * lax.rsqrt(jnp.mean(yg * yg, axis=-1, keepdims=True) + EPS) * nw_ref[:, cs]).astype(o_ref.dtype)


def _ssm_finish(y2, xbc, proj, d_skip, norm_w):
    t = xbc.shape[0]
    di = SSM_D_INNER
    rb = 256
    dsk = jnp.repeat(d_skip.astype(F32), SSM_HEAD_DIM).reshape(1, di)
    return pl.pallas_call(
        _ssm_finish_kernel,
        grid=(t // rb,),
        in_specs=[
            pl.BlockSpec((1, rb, di), lambda i: (0, i, 0)),
            pl.BlockSpec((1, rb, di), lambda i: (1, i, 0)),
            pl.BlockSpec((rb, di), lambda i: (i, 0)),
            pl.BlockSpec((rb, di), lambda i: (i, 0)),
            pl.BlockSpec((1, di), lambda i: (0, 0)),
            pl.BlockSpec((1, di), lambda i: (0, 0)),
        ],
        out_specs=pl.BlockSpec((rb, di), lambda i: (i, 0)),
        out_shape=jax.ShapeDtypeStruct((t, di), BF16),
        compiler_params=_params(1),
        name="ssm_finish",
    )(y2, y2, xbc, proj, dsk, norm_w.reshape(1, di))


def _mamba2(xs, ssq, mod_l, geom, w_in_stack, layer, conv_w, conv_b, dt_bias, a_log, d_skip, norm_w, n_batch, seq, lc):
    t = xs.shape[0]
    n_main = SSM_D_INNER + SSM_CONV_DIM
    proj = _mm_pre(xs, ssq, mod_l, 0, w_in_stack, layer, t, geom, n_cols=n_main)
    dt_raw = _mm_pre(xs, ssq, mod_l, 0, w_in_stack, layer, t, geom, col_start=n_main, n_cols=2 * SSM_HEADS,
                     out_dtype=F32)
    xbc = _ssm_conv(proj, conv_w, conv_b, n_batch, seq, lc)
    y2 = _ssd_scan(xbc, dt_raw, dt_bias, a_log, n_batch, seq, lc)
    return _ssm_finish(y2, xbc, proj, d_skip, norm_w)


def kernel(x, c, ctx, c_ctx, ada_w, ada_b, norm_mix_w, norm_mlp_w, mlp_w1, mlp_w2, na_w_qkv, na_w_o, na_rpb,
           ssm_w_in, ssm_conv_w, ssm_conv_b, ssm_dt_bias, ssm_a_log, ssm_d, ssm_norm_w, ssm_w_out,
           da_w_qkv, da_w_o, da_lambda, da_subln_w, final_norm_w):
    n_batch, seq, d = x.shape
    lc = ctx.shape[1]
    t_lat, t_ctx = n_batch * seq, n_batch * lc
    t = t_lat + t_ctx
    geom = dict(t_lat=t_lat, seq=seq, n_batch=n_batch)
    assert d == D_MODEL and seq % GRID_W == 0 and n_batch < 8

    cond = jnp.concatenate([c, c_ctx[None, :], jnp.zeros((8 - n_batch - 1, d), F32)], axis=0)
    mod = _adaln_all(cond, ada_w, ada_b)
    bufs = _prep(x.reshape(t_lat, d), ctx.reshape(t_ctx, d), norm_mix_w[0], mod[0], geom)
    mlp_w2_bf16 = mlp_w2.astype(BF16)

    for i in range(DEPTH):
        last = i == DEPTH - 1
        mixer, j = i % N_MIXERS, i // N_MIXERS
        y_ctx = None
        mod_l = mod[i]
        rows_out = t_lat if last else t
        _, xs, ssq = bufs
        if mixer == 0:
            qkv = _mm_pre(xs, ssq, mod_l, 0, na_w_qkv, j, t, geom, lead=(NA_HEADS * NA_HEAD_DIM, ATTN_EXP2_SCALE))
            y, y_ctx = _na_attention(qkv, na_rpb[j], n_batch, seq, lc, not last)
            w_o = na_w_o
        elif mixer == 1:
            y = _mamba2(xs, ssq, mod_l, geom, ssm_w_in, j, ssm_conv_w[j], ssm_conv_b[j], ssm_dt_bias[j],
                        ssm_a_log[j], ssm_d[j], ssm_norm_w[j], n_batch, seq, lc)
            w_o = ssm_w_out
        else:
            lambda_init = 0.8 - 0.6 * math.exp(-0.3 * i)
            qk_w = 2 * DA_HEADS * DA_HEAD_DIM
            cos_t, sin_t = _rope_tables(seq, DA_HEAD_DIM)
            qkv = _mm_pre(xs, ssq, mod_l, 0, da_w_qkv, j, t, geom, rope=(cos_t, sin_t, t_lat, 2 * qk_w),
                          lead=(qk_w, ATTN_EXP2_SCALE))
            lam = da_lambda[j].astype(F32)
            lam_full = jnp.exp(jnp.sum(lam[0] * lam[1])) - jnp.exp(jnp.sum(lam[2] * lam[3])) + lambda_init
            y, y_ctx = _diff_attention(qkv, lam_full, da_subln_w[j], lambda_init, n_batch, seq, lc, not last)
            w_o = da_w_o
        w_o = w_o.astype(BF16)
        mlp_norm = (norm_mlp_w[i], mod_l, 4)
        if y_ctx is None:
            bufs = _mm_res(y, w_o, j, bufs, mod_l, 2, rows_out, geom, next_norm=mlp_norm)
        else:
            bufs = _mm_res(y, w_o, j, bufs, mod_l, 2, t_lat, geom, next_norm=mlp_norm)
            bufs = _mm_res(y_ctx, w_o, j, bufs, mod_l, 2, t_ctx, geom, row_off=t_lat, next_norm=mlp_norm)
        hidden = _mm_pre(bufs[1], bufs[2], mod_l, 3, mlp_w1, i, rows_out, geom, act="relu2")
        mixer_norm = None if last else (norm_mix_w[i + 1], mod[i + 1], 1)
        bufs = _mm_res(hidden, mlp_w2_bf16, i, bufs, mod_l, 5, rows_out, geom, next_norm=mixer_norm)

    return _final_norm(bufs[0], final_norm_w, t_lat).reshape(n_batch, seq, d)
```

```python
import functools
import math

import jax
import jax.numpy as jnp
import numpy as np
from jax import lax
from jax.experimental import pallas as pl
from jax.experimental.pallas import tpu as pltpu

F32 = jnp.float32
BF16 = jnp.bfloat16

D_MODEL = 2048
DEPTH = 4
GRID_W = 64
N_MIXERS = 3
EPS = 1e-6

NA_HEADS = 16
NA_HEAD_DIM = 128
NA_WIN_ROWS = 8
NA_WIN_COLS = 16
NA_GROUP_ROWS = 4
NA_KEY_ROWS = 3 * NA_GROUP_ROWS
NA_HEADS_PER_STEP = 8

SSM_D_INNER = 4096
SSM_HEAD_DIM = 64
SSM_HEADS = 64
SSM_GROUPS = 8
SSM_HEADS_PER_GROUP = 8
SSM_STATE = 128
SSM_CONV = 5
SSM_CHUNK = 128
SSM_CONV_DIM = SSM_D_INNER + 2 * SSM_GROUPS * SSM_STATE
SSM_GROUP_WIDTH = SSM_HEADS_PER_GROUP * SSM_HEAD_DIM

DA_HEADS = 8
DA_HEAD_DIM = 128
DA_SUBLN_EPS = 1e-5
DA_Q_BLOCK = 2048
DA_Q_SUBBLOCKS = 8
ROPE_BASE = 10000.0

MASK_NEG = -1e30
assert NA_HEAD_DIM == DA_HEAD_DIM
ATTN_EXP2_SCALE = NA_HEAD_DIM ** -0.5 * math.log2(math.e)

V7X_VMEM_BYTES = 64 * 1024 * 1024
VMEM_LIMIT = (V7X_VMEM_BYTES * 3) // 4
MM_INNER_COLS = 512
MM_RES_INNER_COLS = 256
VMEM_LIMIT_F32_WEIGHTS = (V7X_VMEM_BYTES * 29) // 32
SUBLANES_BF16 = 16
LANES = 128


def _params(n_grid_dims, vmem_limit=VMEM_LIMIT):
    return pltpu.CompilerParams(
        dimension_semantics=("arbitrary",) * n_grid_dims, vmem_limit_bytes=vmem_limit)


def _sigmoid(x):
    return 1.0 / (1.0 + jnp.exp(-x))


def _dot(a, b):
    return jnp.dot(a, b, preferred_element_type=F32)


def _dot_nt(a, b):
    return lax.dot_general(a, b, (((1,), (1,)), ((), ())), preferred_element_type=F32)


def _split3(v):
    hi = v.astype(BF16)
    r1 = v - hi.astype(F32)
    mid = r1.astype(BF16)
    lo = (r1 - mid.astype(F32)).astype(BF16)
    return hi, mid, lo


def _mod_row(row0, t_lat, seq, n_batch):
    return jnp.where(row0 < t_lat, row0 // seq, n_batch)


def _adaln_kernel(c_ref, w_ref, b_ref, o_ref):
    x = c_ref[...]
    x = (x * _sigmoid(x)).astype(BF16)
    o_ref[0] = _dot(x, w_ref[0].astype(BF16)) + b_ref[0]


def _adaln_all(cond, ada_w, ada_b):
    depth, d, n = ada_w.shape
    bn = 1024
    return pl.pallas_call(
        _adaln_kernel,
        grid=(depth, n // bn),
        in_specs=[
            pl.BlockSpec((8, d), lambda l, j: (0, 0)),
            pl.BlockSpec((1, d, bn), lambda l, j: (l, 0, j)),
            pl.BlockSpec((1, 1, bn), lambda l, j: (l, 0, j)),
        ],
        out_specs=pl.BlockSpec((1, 8, bn), lambda l, j: (l, 0, j)),
        out_shape=jax.ShapeDtypeStruct((depth, 8, n), F32),
        compiler_params=_params(2),
        name="adaln",
    )(cond, ada_w, ada_b.reshape(depth, 1, n))


def _final_norm_kernel(h_ref, nw_ref, o_ref):
    x = h_ref[...]
    o_ref[...] = x * lax.rsqrt(jnp.mean(x * x, axis=-1, keepdims=True) + EPS) * nw_ref[...]


def _final_norm(h, norm_w, m_rows):
    d = h.shape[1]
    bm = 256
    return pl.pallas_call(
        _final_norm_kernel,
        grid=(m_rows // bm,),
        in_specs=[pl.BlockSpec((bm, d), lambda i: (i, 0)), pl.BlockSpec((1, d), lambda i: (0, 0))],
        out_specs=pl.BlockSpec((bm, d), lambda i: (i, 0)),
        out_shape=jax.ShapeDtypeStruct((m_rows, d), F32),
        compiler_params=_params(1),
        name="final_norm",
    )(h, norm_w.reshape(1, d))


def _mm_blocks(m_rows, n):
    bm = 1024 if m_rows % 1024 == 0 else 256
    bn = 1024 if n % 1024 == 0 else (512 if n % 512 == 0 else LANES)
    return bm, bn


def _accumulate_over_columns(ssq_ref, part, j):
    @pl.when(j == 0)
    def _():
        ssq_ref[...] = part

    @pl.when(j > 0)
    def _():
        ssq_ref[...] = ssq_ref[...] + part


def _next_norm_outputs(hb, j, midx, nw_ref, nscale_ref, xs_ref, ssq_ref):
    g = nw_ref[...] * (1.0 + nscale_ref[pl.ds(midx, 1), :])
    xs_ref[...] = (hb * g).astype(xs_ref.dtype)
    sq = hb * hb
    part = functools.reduce(jnp.add, [sq[:, c:c + LANES] for c in range(0, sq.shape[1], LANES)])
    _accumulate_over_columns(ssq_ref, part, j)


def _prep_kernel(x_ref, c_ref, nw_ref, nscale_ref, h_ref, xs_ref, ssq_ref, *, bm, lat_blocks, t_lat, seq, n_batch):
    i, j = pl.program_id(0), pl.program_id(1)
    midx = _mod_row(i * bm, t_lat, seq, n_batch)

    def emit(src_ref):
        hb = src_ref[...]
        h_ref[...] = hb
        _next_norm_outputs(hb, j, midx, nw_ref, nscale_ref, xs_ref, ssq_ref)

    pl.when(i < lat_blocks)(lambda: emit(x_ref))
    pl.when(i >= lat_blocks)(lambda: emit(c_ref))


def _prep(x2d, c2d, norm_w, mod_next, geom):
    t_lat, d = x2d.shape
    t_ctx = c2d.shape[0]
    t = t_lat + t_ctx
    bm = 512 if t_ctx % 512 == 0 else 256
    bn = 1024
    lat_blocks = t_lat // bm
    sc0 = d // bn
    return pl.pallas_call(
        functools.partial(_prep_kernel, bm=bm, lat_blocks=lat_blocks, **geom),
        grid=(t // bm, d // bn),
        in_specs=[
            pl.BlockSpec((bm, bn), lambda i, j: (jnp.minimum(i, lat_blocks - 1), j)),
            pl.BlockSpec((bm, bn), lambda i, j: (jnp.maximum(i - lat_blocks, 0), j)),
            pl.BlockSpec((1, bn), lambda i, j: (0, j)),
            pl.BlockSpec((8, bn), lambda i, j: (0, sc0 + j)),
        ],
        out_specs=[
            pl.BlockSpec((bm, bn), lambda i, j: (i, j)),
            pl.BlockSpec((bm, bn), lambda i, j: (i, j)),
            pl.BlockSpec((bm, LANES), lambda i, j: (i, 0)),
        ],
        out_shape=[
            jax.ShapeDtypeStruct((t, d), F32),
            jax.ShapeDtypeStruct((t, d), BF16),
            jax.ShapeDtypeStruct((t, LANES), F32),
        ],
        compiler_params=_params(2),
        name="prep",
    )(x2d, c2d, norm_w.reshape(1, d), mod_next)


def _mm_pre_kernel(*refs, bm, d_norm, shift_k, act, rope_row_blocks, rope_col_blocks, lead_blocks, lead_scale,
                   t_lat, seq, n_batch):
    xs_ref, ssq_ref, mod_ref, w_ref = refs[:4]
    pos = 4
    if rope_col_blocks:
        cos_ref, sin_ref = refs[pos:pos + 2]
        pos += 2
    o_ref, sw_ref = refs[pos], refs[pos + 1]
    i, j = pl.program_id(0), pl.program_id(1)
    bn = o_ref.shape[1]
    hn = min(bn, MM_INNER_COLS)

    @pl.when(i == 0)
    def _():
        shift = mod_ref[:, shift_k * d_norm:(shift_k + 1) * d_norm].astype(BF16)
        for c0 in range(0, bn, hn):
            sw_ref[j, :, c0:c0 + hn] = _dot(shift, w_ref[:, c0:c0 + hn].astype(BF16))

    midx = _mod_row(i * bm, t_lat, seq, n_batch)
    r = lax.rsqrt(jnp.sum(ssq_ref[...], axis=1, keepdims=True) * (1.0 / d_norm) + EPS)
    lead = jnp.where(j < lead_blocks, lead_scale, 1.0) if lead_blocks else 1.0
    r = r * lead
    do_rope = (i < rope_row_blocks) & (j < rope_col_blocks)
    for c0 in range(0, bn, hn):
        acc = _dot(xs_ref[...], w_ref[:, c0:c0 + hn].astype(BF16))
        acc = acc * r + sw_ref[j, pl.ds(midx, 1), c0:c0 + hn] * lead
        if act == "relu2":
            acc = jnp.maximum(acc, 0.0)
            acc = acc * acc
        if not rope_col_blocks:
            o_ref[:, c0:c0 + hn] = acc.astype(o_ref.dtype)
            continue

        @pl.when(do_rope)
        def _(acc=acc, c0=c0):
            hd = cos_ref.shape[1]
            lane = lax.broadcasted_iota(jnp.int32, (bm, hd), 1)
            first = (lane % (hd // 2)) < hd // 4
            for c in range(hn // hd):
                xh = acc[:, c * hd:(c + 1) * hd]
                partner = jnp.where(first, pltpu.roll(xh, hd - hd // 4, 1), pltpu.roll(xh, hd // 4, 1))
                o_ref[:, c0 + c * hd:c0 + (c + 1) * hd] = (xh * cos_ref[...] + partner * sin_ref[...]).astype(o_ref.dtype)

        @pl.when(jnp.logical_not(do_rope))
        def _(acc=acc, c0=c0):
            o_ref[:, c0:c0 + hn] = acc.astype(o_ref.dtype)


def _mm_pre(xs, ssq, mod_l, shift_k, w_stack, layer, m_rows, geom, act=None, rope=None, col_start=0, n_cols=None,
            out_dtype=BF16, lead=None):
    d = xs.shape[1]
    k = w_stack.shape[1]
    n = n_cols or w_stack.shape[2]
    bm, bn = _mm_blocks(m_rows, n)
    assert col_start % bn == 0
    cb0 = col_start // bn
    kern = functools.partial(
        _mm_pre_kernel, bm=bm, d_norm=d, shift_k=shift_k, act=act,
        lead_blocks=(lead[0] // bn if lead else 0), lead_scale=(lead[1] if lead else 1.0),
        rope_row_blocks=(rope[2] // bm if rope else 0), rope_col_blocks=(rope[3] // bn if rope else 0), **geom)
    in_specs = [
        pl.BlockSpec((bm, d), lambda i, j: (i, 0)),
        pl.BlockSpec((bm, LANES), lambda i, j: (i, 0)),
        pl.BlockSpec(mod_l.shape, lambda i, j: (0, 0)),
        pl.BlockSpec((None, k, bn), lambda i, j: (layer, 0, cb0 + j)),
    ]
    args = [xs, ssq, mod_l, w_stack]
    if rope:
        seq_blocks = rope[0].shape[0] // bm
        hd = rope[0].shape[1]
        in_specs += [pl.BlockSpec((bm, hd), lambda i, j: (i % seq_blocks, 0))] * 2
        args += [rope[0], rope[1]]
    return pl.pallas_call(
        kern,
        grid=(m_rows // bm, n // bn),
        in_specs=in_specs,
        out_specs=pl.BlockSpec((bm, bn), lambda i, j: (i, j)),
        out_shape=jax.ShapeDtypeStruct((m_rows, n), out_dtype),
        scratch_shapes=[pltpu.VMEM((n // bn, 8, bn), F32)],
        compiler_params=_params(2, VMEM_LIMIT_F32_WEIGHTS),
        name="mm_pre" + ("_" + act if act else "") + ("_rope" if rope else ""),
    )(*args)


def _mm_res_kernel(*refs, bm, n_k, row_off, emit_next, t_lat, seq, n_batch):
    x_ref, w_ref, res_ref, gate_ref = refs[:4]
    pos = 4
    if emit_next:
        nw_ref, nscale_ref = refs[pos:pos + 2]
        pos += 4
    o_ref = refs[pos]
    pos += 1
    if emit_next:
        xs_ref, ssq_ref = refs[pos:pos + 2]
        pos += 2
    acc_ref = refs[pos].at[pl.program_id(2)] if n_k > 1 else None
    i, kk, j = pl.program_id(0), pl.program_id(1), pl.program_id(2)
    midx = _mod_row(row_off + i * bm, t_lat, seq, n_batch)
    bn = o_ref.shape[1]
    slices = [slice(c0, c0 + MM_RES_INNER_COLS) for c0 in range(0, bn, MM_RES_INNER_COLS)]
    parts = [_dot(x_ref[...], w_ref[:, cs]) for cs in slices]

    def finish(accs):
        sums = None
        for cs, acc in zip(slices, accs):
            hb = res_ref[:, cs] + gate_ref[pl.ds(midx, 1), cs] * acc
            o_ref[:, cs] = hb
            if emit_next:
                g = nw_ref[:, cs] * (1.0 + nscale_ref[pl.ds(midx, 1), cs])
                xs_ref[:, cs] = (hb * g).astype(xs_ref.dtype)
                sq = hb * hb
                for c in range(0, sq.shape[1], LANES):
                    sums = sq[:, c:c + LANES] if sums is None else sums + sq[:, c:c + LANES]
        if emit_next:
            _accumulate_over_columns(ssq_ref, sums, j)

    if n_k == 1:
        finish(parts)
        return

    @pl.when(kk == 0)
    def _():
        for cs, part in zip(slices, parts):
            acc_ref[:, cs] = part

    @pl.when((kk > 0) & (kk < n_k - 1))
    def _():
        for cs, part in zip(slices, parts):
            acc_ref[:, cs] = acc_ref[:, cs] + part

    @pl.when(kk == n_k - 1)
    def _():
        finish([acc_ref[:, cs] + part for cs, part in zip(slices, parts)])


def _mm_res(x, w_stack, layer, bufs, mod_l, gate_k, m_rows, geom, row_off=0, next_norm=None):
    h, xs, ssq = bufs
    k, n = w_stack.shape[1:]
    n_k = 2 if k >= 8192 else 1
    kb = k // n_k
    bm = 1024 if m_rows % 1024 == 0 else 256
    bn = 512 if k >= 4096 else 1024
    assert row_off % bm == 0 and n % bn == 0
    blk_off = row_off // bm
    nb = n // bn
    emit_next = next_norm is not None

    def out_col(kk, j):
        return jnp.where(kk == n_k - 1, j, 0)

    in_specs = [
        pl.BlockSpec((bm, kb), lambda i, kk, j: (i, kk)),
        pl.BlockSpec((None, kb, bn), lambda i, kk, j: (layer, kk, j)),
        pl.BlockSpec((bm, bn), lambda i, kk, j: (blk_off + i, out_col(kk, j))),
        pl.BlockSpec((8, bn), lambda i, kk, j: (0, gate_k * nb + j)),
    ]
    args = [x, w_stack, h, mod_l]
    out_specs = [pl.BlockSpec((bm, bn), lambda i, kk, j: (blk_off + i, out_col(kk, j)))]
    out_shape = [jax.ShapeDtypeStruct(h.shape, F32)]
    aliases = {2: 0}
    if emit_next:
        nw, mod_next, scale_slot = next_norm
        in_specs += [
            pl.BlockSpec((1, bn), lambda i, kk, j: (0, j)),
            pl.BlockSpec((8, bn), lambda i, kk, j: (0, scale_slot * nb + j)),
            pl.BlockSpec(memory_space=pl.ANY),
            pl.BlockSpec(memory_space=pl.ANY),
        ]
        args += [nw.reshape(1, n), mod_next, xs, ssq]
        out_specs += [
            pl.BlockSpec((bm, bn), lambda i, kk, j: (blk_off + i, out_col(kk, j))),
            pl.BlockSpec((bm, LANES), lambda i, kk, j: (blk_off + i, 0)),
        ]
        out_shape += [jax.ShapeDtypeStruct(xs.shape, BF16), jax.ShapeDtypeStruct(ssq.shape, F32)]
        aliases.update({6: 1, 7: 2})
    outs = pl.pallas_call(
        functools.partial(_mm_res_kernel, bm=bm, n_k=n_k, row_off=row_off, emit_next=emit_next, **geom),
        grid=(m_rows // bm, n_k, nb),
        in_specs=in_specs,
        out_specs=out_specs,
        out_shape=out_shape,
        scratch_shapes=[pltpu.VMEM((nb, bm, bn), F32)] if n_k > 1 else [],
        input_output_aliases=aliases,
        compiler_params=_params(3),
        name="mm_res",
    )(*args)
    return (outs[0], outs[1], outs[2]) if emit_next else (outs[0], xs, ssq)


def _na_pair_layout(rows):
    gq, gk = NA_GROUP_ROWS, NA_KEY_ROWS
    pairs = {}
    idx = np.zeros((3, gq, gk // 2), np.int32)
    for typ, (r0, ks) in enumerate(((0, 0), (gq, 0), (rows - gq, rows - gk))):
        for i in range(gq):
            r = r0 + i
            rs = min(max(r - NA_WIN_ROWS // 2, 0), rows - NA_WIN_ROWS)
            dr = [kr - r + NA_WIN_ROWS - 1 if rs <= kr < rs + NA_WIN_ROWS else -1 for kr in range(ks, ks + gk)]
            for p in range(gk // 2):
                idx[typ, i, p] = pairs.setdefault((dr[2 * p], dr[2 * p + 1]), len(pairs))
    return list(pairs), idx


def _na_pair_table(rpb, pairs, out_scale):
    heads = rpb.shape[0]
    col = np.arange(GRID_W)
    col_start = np.clip(col - NA_WIN_COLS // 2, 0, GRID_W - NA_WIN_COLS)
    col_mask = (col[None, :] >= col_start[:, None]) & (col[None, :] < col_start[:, None] + NA_WIN_COLS)
    dc_idx = np.clip(col[None, :] - col[:, None], -(NA_WIN_COLS - 1), NA_WIN_COLS - 1) + NA_WIN_COLS - 1
    onehot = (dc_idx[None] == np.arange(2 * NA_WIN_COLS - 1)[:, None, None]).astype(np.float32)
    per_dr = jnp.einsum("hrd,dqk->hrqk", rpb.astype(F32), onehot, precision=lax.Precision.HIGHEST)
    per_dr = jnp.where(col_mask[None, None], per_dr * out_scale, MASK_NEG)
    masked = jnp.full((heads, GRID_W, GRID_W), MASK_NEG, F32)
    block = lambda dr: masked if dr < 0 else per_dr[:, dr]
    return jnp.stack([jnp.concatenate([block(a), block(b)], axis=-1) for a, b in pairs], axis=1)


def _softmax_pv(scores, values, out_dtype):
    m = functools.reduce(jnp.maximum, [jnp.max(t, axis=1, keepdims=True) for t in scores])
    dh = values[0].shape[1]
    o = None
    for t, v in zip(scores, values):
        p = jnp.exp2(t - m).astype(BF16)
        part = _dot(p, jnp.concatenate([v, jnp.ones_like(v)], axis=1))
        o = part if o is None else o + part
    return (o[:, :dh] * (1.0 / o[:, dh:])).astype(out_dtype)


def _na_kernel(idx_ref, q_ref, k0_ref, k1_ref, k2_ref, v0_ref, v1_ref, v2_ref, kc_ref, vc_ref, pair_ref, o_ref,
               *, n_groups):
    dh = NA_HEAD_DIM
    gq, n_pairs = NA_GROUP_ROWS, NA_KEY_ROWS // 2
    g = pl.program_id(1)
    typ = jnp.where(g == 0, 0, jnp.where(g == n_groups - 1, 2, 1))
    for hh in range(NA_HEADS_PER_STEP):
        sl = slice(hh * dh, (hh + 1) * dh)
        q = q_ref[:, sl]
        bias = jnp.concatenate(
            [jnp.concatenate([pair_ref[hh, idx_ref[(typ * gq + i) * n_pairs + p]] for p in range(n_pairs)], axis=1)
             for i in range(gq)], axis=0)
        s_loc = jnp.concatenate([_dot_nt(q, kr[:, sl]) for kr in (k0_ref, k1_ref, k2_ref)], axis=1)
        t_loc = s_loc + bias
        t_ctx = _dot_nt(q, kc_ref[:, sl])
        kb = k0_ref.shape[0]
        scores = [t_loc[:, j * kb:(j + 1) * kb] for j in range(3)] + [t_ctx]
        values = [vr[:, sl] for vr in (v0_ref, v1_ref, v2_ref, vc_ref)]
        o_ref[:, sl] = _softmax_pv(scores, values, o_ref.dtype)


def _na_ctx_kernel(q_ref, kc_ref, vc_ref, o_ref):
    dh = NA_HEAD_DIM
    for hh in range(NA_HEADS_PER_STEP):
        sl = slice(hh * dh, (hh + 1) * dh)
        t_ctx = _dot_nt(q_ref[:, sl], kc_ref[:, sl])
        o_ref[:, sl] = _softmax_pv([t_ctx], [vc_ref[:, sl]], o_ref.dtype)


def _na_attention(qkv, rpb, n_batch, seq, lc, ctx_out):
    hd = NA_HEADS * NA_HEAD_DIM
    gtok = NA_GROUP_ROWS * GRID_W
    ng = seq // gtok
    assert lc == gtok and ng >= 4
    hw = NA_HEADS_PER_STEP * NA_HEAD_DIM
    nhq = NA_HEADS // NA_HEADS_PER_STEP
    t_lat = n_batch * seq
    ctx_blk0 = n_batch * ng
    pairs, idx = _na_pair_layout(seq // GRID_W)
    pair_tab = _na_pair_table(rpb, pairs, math.log2(math.e))

    def k_row(g, b, j):
        return b * ng + jnp.clip(g - 1, 0, ng - 3) + j

    blk = (gtok, hw)
    in_specs = [pl.BlockSpec(memory_space=pltpu.SMEM), pl.BlockSpec(blk, lambda h, g, b: (b * ng + g, h))]
    for col0 in (nhq, 2 * nhq):
        for j in range(3):
            in_specs.append(pl.BlockSpec(blk, lambda h, g, b, j=j, col0=col0: (k_row(g, b, j), col0 + h)))
    in_specs.append(pl.BlockSpec(blk, lambda h, g, b: (ctx_blk0 + b, nhq + h)))
    in_specs.append(pl.BlockSpec(blk, lambda h, g, b: (ctx_blk0 + b, 2 * nhq + h)))
    in_specs.append(pl.BlockSpec((NA_HEADS_PER_STEP, len(pairs), GRID_W, 2 * GRID_W), lambda h, g, b: (h, 0, 0, 0)))
    y_lat = pl.pallas_call(
        functools.partial(_na_kernel, n_groups=ng),
        grid=(nhq, ng, n_batch),
        in_specs=in_specs,
        out_specs=pl.BlockSpec(blk, lambda h, g, b: (b * ng + g, h)),
        out_shape=jax.ShapeDtypeStruct((t_lat, hd), BF16),
        compiler_params=_params(3),
        name="na_attention",
    )(jnp.asarray(idx.reshape(-1)), *([qkv] * 9), pair_tab)
    if not ctx_out:
        return y_lat, None
    y_ctx = pl.pallas_call(
        _na_ctx_kernel,
        grid=(nhq, n_batch),
        in_specs=[
            pl.BlockSpec(blk, lambda h, b: (ctx_blk0 + b, h)),
            pl.BlockSpec(blk, lambda h, b: (ctx_blk0 + b, nhq + h)),
            pl.BlockSpec(blk, lambda h, b: (ctx_blk0 + b, 2 * nhq + h)),
        ],
        out_specs=pl.BlockSpec(blk, lambda h, b: (b, h)),
        out_shape=jax.ShapeDtypeStruct((n_batch * lc, hd), BF16),
        compiler_params=_params(2),
        name="na_attention_ctx",
    )(qkv, qkv, qkv)
    return y_lat, y_ctx


def _rope_tables(seq, width):
    t = jnp.arange(seq)
    pos = jnp.stack([t // GRID_W, t % GRID_W], axis=-1).astype(F32)
    n_freq = DA_HEAD_DIM // 4
    inv_freq = ROPE_BASE ** (-jnp.arange(n_freq, dtype=F32) / n_freq)
    ang = pos[:, :, None] * inv_freq
    cos = jnp.broadcast_to(jnp.cos(ang)[:, :, None, :], (seq, 2, 2, n_freq)).reshape(seq, DA_HEAD_DIM)
    sign = jnp.array([-1.0, 1.0], F32)[None, None, :, None]
    sin = (jnp.broadcast_to(jnp.sin(ang)[:, :, None, :], (seq, 2, 2, n_freq)) * sign).reshape(seq, DA_HEAD_DIM)
    reps = width // DA_HEAD_DIM
    return jnp.tile(cos, (1, reps)), jnp.tile(sin, (1, reps))


def _da_kernel(lam_ref, q_ref, *refs, out_scale, n_sub, with_latent):
    n_kv = 2 if with_latent else 1
    k_refs, v_refs = refs[:n_kv], refs[n_kv:2 * n_kv]
    sw_ref, o_ref = refs[2 * n_kv], refs[2 * n_kv + 1]
    dh = DA_HEAD_DIM
    lam = lam_ref[0]
    rows = q_ref.shape[0] // n_sub
    for sub in range(n_sub):
        rs = slice(sub * rows, (sub + 1) * rows)
        probs, sums = [], []
        for comp in range(2):
            sl = slice(comp * dh, (comp + 1) * dh)
            q = q_ref[rs, sl]
            s = [_dot_nt(q, k_ref[:, sl]) for k_ref in k_refs]
            m = functools.reduce(jnp.maximum, [jnp.max(x, axis=1, keepdims=True) for x in s])
            p = [jnp.exp2(x - m) for x in s]
            probs.append(p)
            sums.append(functools.reduce(jnp.add, [jnp.sum(x, axis=1, keepdims=True) for x in p]))
        ratio = lam * sums[0] / sums[1]
        o = None
        for p1, p2, v_ref in zip(probs[0], probs[1], v_refs):
            part = _dot((p1 - ratio * p2).astype(BF16), v_ref[...])
            o = part if o is None else o + part
        o = o * (1.0 / sums[0])
        y = o * lax.rsqrt(jnp.mean(o * o, axis=-1, keepdims=True) + DA_SUBLN_EPS) * sw_ref[...]
        o_ref[rs, :] = (y * out_scale).astype(o_ref.dtype)


def _diff_attention(qkv, lam_full, subln_w, lambda_init, n_batch, seq, lc, ctx_out):
    hw = 2 * DA_HEAD_DIM
    bq, n_sub = DA_Q_BLOCK, DA_Q_SUBBLOCKS
    nq = seq // bq
    t_lat = n_batch * seq
    ctx_blk0 = t_lat // lc
    lam1 = lam_full.reshape(1).astype(F32)
    sw = subln_w.reshape(1, hw)
    common = dict(out_scale=1.0 - lambda_init)
    smem = pl.BlockSpec(memory_space=pltpu.SMEM)
    y_lat = pl.pallas_call(
        functools.partial(_da_kernel, n_sub=n_sub, with_latent=True, **common),
        grid=(n_batch, DA_HEADS, nq),
        in_specs=[
            smem,
            pl.BlockSpec((bq, hw), lambda b, h, i: (b * nq + i, h)),
            pl.BlockSpec((seq, hw), lambda b, h, i: (b, DA_HEADS + h)),
            pl.BlockSpec((lc, hw), lambda b, h, i: (ctx_blk0 + b, DA_HEADS + h)),
            pl.BlockSpec((seq, hw), lambda b, h, i: (b, 2 * DA_HEADS + h)),
            pl.BlockSpec((lc, hw), lambda b, h, i: (ctx_blk0 + b, 2 * DA_HEADS + h)),
            pl.BlockSpec((1, hw), lambda b, h, i: (0, 0)),
        ],
        out_specs=pl.BlockSpec((bq, hw), lambda b, h, i: (b * nq + i, h)),
        out_shape=jax.ShapeDtypeStruct((t_lat, DA_HEADS * hw), BF16),
        compiler_params=_params(3, VMEM_LIMIT_F32_WEIGHTS),
        name="diff_attention",
    )(lam1, qkv, qkv, qkv, qkv, qkv, sw)
    if not ctx_out:
        return y_lat, None
    y_ctx = pl.pallas_call(
        functools.partial(_da_kernel, n_sub=1, with_latent=False, **common),
        grid=(n_batch, DA_HEADS),
        in_specs=[
            smem,
            pl.BlockSpec((lc, hw), lambda b, h: (ctx_blk0 + b, h)),
            pl.BlockSpec((lc, hw), lambda b, h: (ctx_blk0 + b, DA_HEADS + h)),
            pl.BlockSpec((lc, hw), lambda b, h: (ctx_blk0 + b, 2 * DA_HEADS + h)),
            pl.BlockSpec((1, hw), lambda b, h: (0, 0)),
        ],
        out_specs=pl.BlockSpec((lc, hw), lambda b, h: (b, h)),
        out_shape=jax.ShapeDtypeStruct((n_batch * lc, DA_HEADS * hw), BF16),
        compiler_params=_params(2),
        name="diff_attention_ctx",
    )(lam1, qkv, qkv, qkv, sw)
    return y_lat, y_ctx


def _conv_kernel(prev_ref, cur_ref, next_ref, w_ref, b_ref, o_ref, *, rb, lat_blocks, lat_blocks_per_seq, ctx_blocks_per_seq):
    i = pl.program_id(0)
    is_lat = i < lat_blocks
    pos = jnp.where(is_lat, i % lat_blocks_per_seq, (i - lat_blocks) % ctx_blocks_per_seq)
    per_seq = jnp.where(is_lat, lat_blocks_per_seq, ctx_blocks_per_seq)
    halo = prev_ref.shape[0]
    prev = jnp.where(pos == 0, 0.0, prev_ref[...].astype(F32))
    nxt = jnp.where(pos == per_seq - 1, 0.0, next_ref[...].astype(F32))
    cat = jnp.concatenate([prev, cur_ref[...].astype(F32), nxt], axis=0)
    n = rb + 2 * halo
    acc = b_ref[...]
    for k in range(SSM_CONV):
        shifted = cat if k == SSM_CONV // 2 else pltpu.roll(cat, (SSM_CONV // 2 - k) % n, 0)
        acc = acc + w_ref[k:k + 1, :] * shifted[halo:halo + rb]
    o_ref[...] = (acc * _sigmoid(acc)).astype(o_ref.dtype)


def _ssm_conv(proj, conv_w, conv_b, n_batch, seq, lc):
    t = proj.shape[0]
    rb, ct, halo = 256, 2048, SUBLANES_BF16
    assert seq % rb == 0 and lc % rb == 0 and SSM_D_INNER % ct == 0 and SSM_CONV_DIM % ct == 0
    col0 = SSM_D_INNER // ct
    hb = rb // halo
    n_halo_blocks = t // halo
    kern = functools.partial(_conv_kernel, rb=rb, lat_blocks=n_batch * seq // rb,
                             lat_blocks_per_seq=seq // rb, ctx_blocks_per_seq=lc // rb)
    return pl.pallas_call(
        kern,
        grid=(t // rb, SSM_CONV_DIM // ct),
        in_specs=[
            pl.BlockSpec((halo, ct), lambda i, j: (jnp.maximum(i * hb - 1, 0), col0 + j)),
            pl.BlockSpec((rb, ct), lambda i, j: (i, col0 + j)),
            pl.BlockSpec((halo, ct), lambda i, j: (jnp.minimum((i + 1) * hb, n_halo_blocks - 1), col0 + j)),
            pl.BlockSpec((SSM_CONV, ct), lambda i, j: (0, j)),
            pl.BlockSpec((1, ct), lambda i, j: (0, j)),
        ],
        out_specs=pl.BlockSpec((rb, ct), lambda i, j: (i, j)),
        out_shape=jax.ShapeDtypeStruct((t, SSM_CONV_DIM), BF16),
        compiler_params=_params(2),
        name="ssm_conv",
    )(proj, proj, proj, conv_w, conv_b.reshape(1, SSM_CONV_DIM))


def _ssd_kernel(xbc_ref, dt_ref, dtb_ref, alog_ref, tri_ref, half_ref, y_ref, state_ref):
    q = SSM_CHUNK
    nh = SSM_HEADS
    p2 = 2 * SSM_HEAD_DIM
    gw = SSM_GROUP_WIDTH
    log2e = math.log2(math.e)
    direction = pl.program_id(1)
    fwd = direction == 0

    @pl.when(pl.program_id(2) == 0)
    def _():
        state_ref[...] = jnp.zeros_like(state_ref)

    pre = dt_ref[...] + dtb_ref[...]
    dt_all = jnp.maximum(pre, 0.0) + jnp.log1p(jnp.exp(-jnp.abs(pre)))
    a_all = dt_all * (jnp.exp(alog_ref[...]) * -log2e)
    tri = tri_ref[0]
    acum_all = sum(_dot(tri, piece) for piece in _split3(a_all))
    acum = jnp.where(fwd, acum_all, pltpu.roll(acum_all, nh, 1))
    acum_t = acum.T
    dt_t = jnp.where(fwd, dt_all, pltpu.roll(dt_all, nh, 1)).T
    src_t = acum_t - jnp.log(dt_t) * log2e
    lane = lax.broadcasted_iota(jnp.int32, (q, q), 1)
    last = jnp.where(fwd, q - 1, 0)
    total = jnp.sum(jnp.where(lane == last, acum_t, 0.0), axis=1, keepdims=True)
    to_end_t = dt_t * jnp.exp2(total - acum_t)
    chunk_decay = jnp.broadcast_to(jnp.exp2(total), (q, q))
    seen = tri.astype(F32) > 0.5
    first_head = lane < SSM_HEAD_DIM
    c_off = SSM_D_INNER + SSM_GROUPS * SSM_STATE

    for g in range(SSM_GROUPS):
        b_g = xbc_ref[:, SSM_D_INNER + g * SSM_STATE:SSM_D_INNER + (g + 1) * SSM_STATE]
        c_g = xbc_ref[:, c_off + g * SSM_STATE:c_off + (g + 1) * SSM_STATE]
        cb = _dot_nt(c_g, b_g)
        b_t = b_g.astype(F32).T
        y_carried = _dot(c_g, state_ref[g].astype(BF16))
        for pr in range(SSM_HEADS_PER_GROUP // 2):
            ps = slice(pr * p2, (pr + 1) * p2)
            cols = slice(g * gw + pr * p2, g * gw + (pr + 1) * p2)
            x2 = xbc_ref[:, cols]
            x_bd = jnp.concatenate([x2 * half_ref[0], x2 * half_ref[1]], axis=0)
            intra, to_state, carry_scale, decay = [], [], [], []
            for r in (2 * pr, 2 * pr + 1):
                hh = g * SSM_HEADS_PER_GROUP + r
                col = jnp.sum(jnp.where(lane == hh, acum, 0.0), axis=1, keepdims=True)
                intra.append((cb * jnp.where(seen, jnp.exp2(col - src_t[hh:hh + 1, :]), 0.0)).astype(BF16))
                to_state.append((b_t * to_end_t[hh:hh + 1, :]).astype(BF16))
                carry_scale.append(jnp.exp2(col))
                decay.append(chunk_decay[hh:hh + 1, :])
            y_ref[0, :, cols] = (_dot(jnp.concatenate(intra, axis=1), x_bd)
                                 + jnp.where(first_head, carry_scale[0], carry_scale[1]) * y_carried[:, ps]
                                 ).astype(y_ref.dtype)
            state_ref[g, :, ps] = (state_ref[g, :, ps] * jnp.where(first_head[:1], decay[0], decay[1])
                                   + _dot(jnp.concatenate(to_state, axis=1), x_bd))


def _ssd_scan(xbc, dt_raw, dt_bias, a_log, n_batch, seq, lc):
    t = xbc.shape[0]
    q = SSM_CHUNK
    n_lat, n_ctx = seq // q, lc // q
    steps = n_lat + n_ctx
    ctx_blk0 = n_batch * n_lat

    def blk(b, d, s):
        ctx_chunk = jnp.where(d == 0, s, n_ctx - 1 - s)
        lat_chunk = jnp.where(d == 0, s - n_ctx, steps - 1 - s)
        return jnp.where(s < n_ctx, ctx_blk0 + b * n_ctx + ctx_chunk, b * n_lat + lat_chunk)

    idx = np.arange(q)
    tri = np.stack([idx[None, :] <= idx[:, None], idx[None, :] >= idx[:, None]]).astype(np.float32)
    first = np.broadcast_to(np.arange(2 * SSM_HEAD_DIM)[None, :] < SSM_HEAD_DIM, (q, 2 * SSM_HEAD_DIM))
    half = np.stack([first, ~first]).astype(np.float32)
    return pl.pallas_call(
        _ssd_kernel,
        grid=(n_batch, 2, steps),
        in_specs=[
            pl.BlockSpec((q, SSM_CONV_DIM), lambda b, d, s: (blk(b, d, s), 0)),
            pl.BlockSpec((q, 2 * SSM_HEADS), lambda b, d, s: (blk(b, d, s), 0)),
            pl.BlockSpec((1, 2 * SSM_HEADS), lambda b, d, s: (0, 0)),
            pl.BlockSpec((1, 2 * SSM_HEADS), lambda b, d, s: (0, 0)),
            pl.BlockSpec((1, q, q), lambda b, d, s: (d, 0, 0)),
            pl.BlockSpec((2, q, 2 * SSM_HEAD_DIM), lambda b, d, s: (0, 0, 0)),
        ],
        out_specs=pl.BlockSpec((1, q, SSM_D_INNER), lambda b, d, s: (d, blk(b, d, s), 0)),
        out_shape=jax.ShapeDtypeStruct((2, t, SSM_D_INNER), BF16),
        scratch_shapes=[pltpu.VMEM((SSM_GROUPS, SSM_STATE, SSM_GROUP_WIDTH), F32)],
        compiler_params=_params(3),
        name="ssd_scan",
    )(xbc, dt_raw, dt_bias.reshape(1, -1), a_log.reshape(1, -1), jnp.asarray(tri, BF16), jnp.asarray(half, BF16))


def _ssm_finish_kernel(yf_ref, yb_ref, x_ref, z_ref, dsk_ref, nw_ref, o_ref):
    gw = SSM_GROUP_WIDTH
    z = z_ref[...].astype(F32)
    y = yf_ref[0].astype(F32) + yb_ref[0].astype(F32) + x_ref[...].astype(F32) * dsk_ref[...]
    y = y * (z * _sigmoid(z))
    for g in range(SSM_GROUPS):
        cs = slice(g * gw, (g + 1) * gw)
        yg = y[:, cs]
        o_ref[:, cs] = (yg * lax.rsqrt(jnp.mean(yg * yg, axis=-1, keepdims=True) + EPS) * nw_ref[:, cs]).astype(o_ref.dtype)


def _ssm_finish(y2, xbc, proj, d_skip, norm_w):
    t = xbc.shape[0]
    di = SSM_D_INNER
    rb = 256
    dsk = jnp.repeat(d_skip.astype(F32), SSM_HEAD_DIM).reshape(1, di)
    return pl.pallas_call(
        _ssm_finish_kernel,
        grid=(t // rb,),
        in_specs=[
            pl.BlockSpec((1, rb, di), lambda i: (0, i, 0)),
            pl.BlockSpec((1, rb, di), lambda i: (1, i, 0)),
            pl.BlockSpec((rb, di), lambda i: (i, 0)),
            pl.BlockSpec((rb, di), lambda i: (i, 0)),
            pl.BlockSpec((1, di), lambda i: (0, 0)),
            pl.BlockSpec((1, di), lambda i: (0, 0)),
        ],
        out_specs=pl.BlockSpec((rb, di), lambda i: (i, 0)),
        out_shape=jax.ShapeDtypeStruct((t, di), BF16),
        compiler_params=_params(1),
        name="ssm_finish",
    )(y2, y2, xbc, proj, dsk, norm_w.reshape(1, di))


def _mamba2(xs, ssq, mod_l, geom, w_in_stack, layer, conv_w, conv_b, dt_bias, a_log, d_skip, norm_w, n_batch, seq, lc):
    t = xs.shape[0]
    n_main = SSM_D_INNER + SSM_CONV_DIM
    proj = _mm_pre(xs, ssq, mod_l, 0, w_in_stack, layer, t, geom, n_cols=n_main)
    dt_raw = _mm_pre(xs, ssq, mod_l, 0, w_in_stack, layer, t, geom, col_start=n_main, n_cols=2 * SSM_HEADS,
                     out_dtype=F32)
    xbc = _ssm_conv(proj, conv_w, conv_b, n_batch, seq, lc)
    y2 = _ssd_scan(xbc, dt_raw, dt_bias, a_log, n_batch, seq, lc)
    return _ssm_finish(y2, xbc, proj, d_skip, norm_w)


def kernel(x, c, ctx, c_ctx, ada_w, ada_b, norm_mix_w, norm_mlp_w, mlp_w1, mlp_w2, na_w_qkv, na_w_o, na_rpb,
           ssm_w_in, ssm_conv_w, ssm_conv_b, ssm_dt_bias, ssm_a_log, ssm_d, ssm_norm_w, ssm_w_out,
           da_w_qkv, da_w_o, da_lambda, da_subln_w, final_norm_w):
    n_batch, seq, d = x.shape
    lc = ctx.shape[1]
    t_lat, t_ctx = n_batch * seq, n_batch * lc
    t = t_lat + t_ctx
    geom = dict(t_lat=t_lat, seq=seq, n_batch=n_batch)
    assert d == D_MODEL and seq % GRID_W == 0 and n_batch < 8

    cond = jnp.concatenate([c, c_ctx[None, :], jnp.zeros((8 - n_batch - 1, d), F32)], axis=0)
    mod = _adaln_all(cond, ada_w, ada_b)
    bufs = _prep(x.reshape(t_lat, d), ctx.reshape(t_ctx, d), norm_mix_w[0], mod[0], geom)
    mlp_w2_bf16 = mlp_w2.astype(BF16)

    for i in range(DEPTH):
        last = i == DEPTH - 1
        mixer, j = i % N_MIXERS, i // N_MIXERS
        y_ctx = None
        mod_l = mod[i]
        rows_out = t_lat if last else t
        _, xs, ssq = bufs
        if mixer == 0:
            qkv = _mm_pre(xs, ssq, mod_l, 0, na_w_qkv, j, t, geom, lead=(NA_HEADS * NA_HEAD_DIM, ATTN_EXP2_SCALE))
            y, y_ctx = _na_attention(qkv, na_rpb[j], n_batch, seq, lc, not last)
            w_o = na_w_o
        elif mixer == 1:
            y = _mamba2(xs, ssq, mod_l, geom, ssm_w_in, j, ssm_conv_w[j], ssm_conv_b[j], ssm_dt_bias[j],
                        ssm_a_log[j], ssm_d[j], ssm_norm_w[j], n_batch, seq, lc)
            w_o = ssm_w_out
        else:
            lambda_init = 0.8 - 0.6 * math.exp(-0.3 * i)
            qk_w = 2 * DA_HEADS * DA_HEAD_DIM
            cos_t, sin_t = _rope_tables(seq, DA_HEAD_DIM)
            qkv = _mm_pre(xs, ssq, mod_l, 0, da_w_qkv, j, t, geom, rope=(cos_t, sin_t, t_lat, 2 * qk_w),
                          lead=(qk_w, ATTN_EXP2_SCALE))
            lam = da_lambda[j].astype(F32)
            lam_full = jnp.exp(jnp.sum(lam[0] * lam[1])) - jnp.exp(jnp.sum(lam[2] * lam[3])) + lambda_init
            y, y_ctx = _diff_attention(qkv, lam_full, da_subln_w[j], lambda_init, n_batch, seq, lc, not last)
            w_o = da_w_o
        w_o = w_o.astype(BF16)
        mlp_norm = (norm_mlp_w[i], mod_l, 4)
        if y_ctx is None:
            bufs = _mm_res(y, w_o, j, bufs, mod_l, 2, rows_out, geom, next_norm=mlp_norm)
        else:
            bufs = _mm_res(y, w_o, j, bufs, mod_l, 2, t_lat, geom, next_norm=mlp_norm)
            bufs = _mm_res(y_ctx, w_o, j, bufs, mod_l, 2, t_ctx, geom, row_off=t_lat, next_norm=mlp_norm)
        hidden = _mm_pre(bufs[1], bufs[2], mod_l, 3, mlp_w1, i, rows_out, geom, act="relu2")
        mixer_norm = None if last else (norm_mix_w[i + 1], mod[i + 1], 1)
        bufs = _mm_res(hidden, mlp_w2_bf16, i, bufs, mod_l, 5, rows_out, geom, next_norm=mixer_norm)

    return _final_norm(bufs[0], final_norm_w, t_lat).reshape(n_batch, seq, d)
```

```python
import functools
import math

import jax
import jax.numpy as jnp
import numpy as np
from jax import lax
from jax.experimental import pallas as pl
from jax.experimental.pallas import tpu as pltpu

F32 = jnp.float32
BF16 = jnp.bfloat16

D_MODEL = 2048
DEPTH = 4
GRID_W = 64
N_MIXERS = 3
EPS = 1e-6

NA_HEADS = 16
NA_HEAD_DIM = 128
NA_WIN_ROWS = 8
NA_WIN_COLS = 16
NA_GROUP_ROWS = 4
NA_KEY_ROWS = 3 * NA_GROUP_ROWS
NA_HEADS_PER_STEP = 8

SSM_D_INNER = 4096
SSM_HEAD_DIM = 64
SSM_HEADS = 64
SSM_GROUPS = 8
SSM_HEADS_PER_GROUP = 8
SSM_STATE = 128
SSM_CONV = 5
SSM_CHUNK = 128
SSM_CONV_DIM = SSM_D_INNER + 2 * SSM_GROUPS * SSM_STATE
SSM_GROUP_WIDTH = SSM_HEADS_PER_GROUP * SSM_HEAD_DIM

DA_HEADS = 8
DA_HEAD_DIM = 128
DA_SUBLN_EPS = 1e-5
DA_Q_BLOCK = 2048
DA_Q_SUBBLOCKS = 8
ROPE_BASE = 10000.0

MASK_NEG = -1e30
assert NA_HEAD_DIM == DA_HEAD_DIM
ATTN_EXP2_SCALE = NA_HEAD_DIM ** -0.5 * math.log2(math.e)

V7X_VMEM_BYTES = 64 * 1024 * 1024
VMEM_LIMIT = (V7X_VMEM_BYTES * 3) // 4
MM_INNER_COLS = 1024
MM_RES_INNER_COLS = 256
VMEM_LIMIT_F32_WEIGHTS = (V7X_VMEM_BYTES * 29) // 32
SUBLANES_BF16 = 16
LANES = 128


def _params(n_grid_dims, vmem_limit=VMEM_LIMIT):
    return pltpu.CompilerParams(
        dimension_semantics=("arbitrary",) * n_grid_dims, vmem_limit_bytes=vmem_limit)


def _sigmoid(x):
    return 1.0 / (1.0 + jnp.exp(-x))


def _dot(a, b):
    return jnp.dot(a, b, preferred_element_type=F32)


def _dot_nt(a, b):
    return lax.dot_general(a, b, (((1,), (1,)), ((), ())), preferred_element_type=F32)


def _split3(v):
    hi = v.astype(BF16)
    r1 = v - hi.astype(F32)
    mid = r1.astype(BF16)
    lo = (r1 - mid.astype(F32)).astype(BF16)
    return hi, mid, lo


def _mod_row(row0, t_lat, seq, n_batch):
    return jnp.where(row0 < t_lat, row0 // seq, n_batch)


def _adaln_kernel(c_ref, w_ref, b_ref, o_ref):
    x = c_ref[...]
    x = (x * _sigmoid(x)).astype(BF16)
    o_ref[0] = _dot(x, w_ref[0].astype(BF16)) + b_ref[0]


def _adaln_all(cond, ada_w, ada_b):
    depth, d, n = ada_w.shape
    bn = 1024
    return pl.pallas_call(
        _adaln_kernel,
        grid=(depth, n // bn),
        in_specs=[
            pl.BlockSpec((8, d), lambda l, j: (0, 0)),
            pl.BlockSpec((1, d, bn), lambda l, j: (l, 0, j)),
            pl.BlockSpec((1, 1, bn), lambda l, j: (l, 0, j)),
        ],
        out_specs=pl.BlockSpec((1, 8, bn), lambda l, j: (l, 0, j)),
        out_shape=jax.ShapeDtypeStruct((depth, 8, n), F32),
        compiler_params=_params(2),
        name="adaln",
    )(cond, ada_w, ada_b.reshape(depth, 1, n))


def _final_norm_kernel(h_ref, nw_ref, o_ref):
    x = h_ref[...]
    o_ref[...] = x * lax.rsqrt(jnp.mean(x * x, axis=-1, keepdims=True) + EPS) * nw_ref[...]


def _final_norm(h, norm_w, m_rows):
    d = h.shape[1]
    bm = 256
    return pl.pallas_call(
        _final_norm_kernel,
        grid=(m_rows // bm,),
        in_specs=[pl.BlockSpec((bm, d), lambda i: (i, 0)), pl.BlockSpec((1, d), lambda i: (0, 0))],
        out_specs=pl.BlockSpec((bm, d), lambda i: (i, 0)),
        out_shape=jax.ShapeDtypeStruct((m_rows, d), F32),
        compiler_params=_params(1),
        name="final_norm",
    )(h, norm_w.reshape(1, d))


def _mm_blocks(m_rows, n):
    bm = 1024 if m_rows % 1024 == 0 else 256
    bn = 1024 if n % 1024 == 0 else (512 if n % 512 == 0 else LANES)
    return bm, bn


def _accumulate_over_columns(ssq_ref, part, j):
    @pl.when(j == 0)
    def _():
        ssq_ref[...] = part

    @pl.when(j > 0)
    def _():
        ssq_ref[...] = ssq_ref[...] + part


def _next_norm_outputs(hb, j, midx, nw_ref, nscale_ref, xs_ref, ssq_ref):
    g = nw_ref[...] * (1.0 + nscale_ref[pl.ds(midx, 1), :])
    xs_ref[...] = (hb * g).astype(xs_ref.dtype)
    sq = hb * hb
    part = functools.reduce(jnp.add, [sq[:, c:c + LANES] for c in range(0, sq.shape[1], LANES)])
    _accumulate_over_columns(ssq_ref, part, j)


def _prep_kernel(x_ref, c_ref, nw_ref, nscale_ref, h_ref, xs_ref, ssq_ref, *, bm, lat_blocks, t_lat, seq, n_batch):
    i, j = pl.program_id(0), pl.program_id(1)
    midx = _mod_row(i * bm, t_lat, seq, n_batch)

    def emit(src_ref):
        hb = src_ref[...]
        h_ref[...] = hb
        _next_norm_outputs(hb, j, midx, nw_ref, nscale_ref, xs_ref, ssq_ref)

    pl.when(i < lat_blocks)(lambda: emit(x_ref))
    pl.when(i >= lat_blocks)(lambda: emit(c_ref))


def _prep(x2d, c2d, norm_w, mod_next, geom):
    t_lat, d = x2d.shape
    t_ctx = c2d.shape[0]
    t = t_lat + t_ctx
    bm = 512 if t_ctx % 512 == 0 else 256
    bn = 1024
    lat_blocks = t_lat // bm
    sc0 = d // bn
    return pl.pallas_call(
        functools.partial(_prep_kernel, bm=bm, lat_blocks=lat_blocks, **geom),
        grid=(t // bm, d // bn),
        in_specs=[
            pl.BlockSpec((bm, bn), lambda i, j: (jnp.minimum(i, lat_blocks - 1), j)),
            pl.BlockSpec((bm, bn), lambda i, j: (jnp.maximum(i - lat_blocks, 0), j)),
            pl.BlockSpec((1, bn), lambda i, j: (0, j)),
            pl.BlockSpec((8, bn), lambda i, j: (0, sc0 + j)),
        ],
        out_specs=[
            pl.BlockSpec((bm, bn), lambda i, j: (i, j)),
            pl.BlockSpec((bm, bn), lambda i, j: (i, j)),
            pl.BlockSpec((bm, LANES), lambda i, j: (i, 0)),
        ],
        out_shape=[
            jax.ShapeDtypeStruct((t, d), F32),
            jax.ShapeDtypeStruct((t, d), BF16),
            jax.ShapeDtypeStruct((t, LANES), F32),
        ],
        compiler_params=_params(2),
        name="prep",
    )(x2d, c2d, norm_w.reshape(1, d), mod_next)


def _mm_pre_kernel(*refs, bm, d_norm, shift_k, act, rope_row_blocks, rope_col_blocks, lead_blocks, lead_scale,
                   t_lat, seq, n_batch):
    xs_ref, ssq_ref, mod_ref, w_ref = refs[:4]
    pos = 4
    if rope_col_blocks:
        cos_ref, sin_ref = refs[pos:pos + 2]
        pos += 2
    o_ref, sw_ref = refs[pos], refs[pos + 1]
    i, j = pl.program_id(0), pl.program_id(1)
    bn = o_ref.shape[1]
    hn = min(bn, MM_INNER_COLS)

    @pl.when(i == 0)
    def _():
        shift = mod_ref[:, shift_k * d_norm:(shift_k + 1) * d_norm].astype(BF16)
        for c0 in range(0, bn, hn):
            sw_ref[j, :, c0:c0 + hn] = _dot(shift, w_ref[:, c0:c0 + hn].astype(BF16))

    midx = _mod_row(i * bm, t_lat, seq, n_batch)
    r = lax.rsqrt(jnp.sum(ssq_ref[...], axis=1, keepdims=True) * (1.0 / d_norm) + EPS)
    lead = jnp.where(j < lead_blocks, lead_scale, 1.0) if lead_blocks else 1.0
    r = r * lead
    do_rope = (i < rope_row_blocks) & (j < rope_col_blocks)
    for c0 in range(0, bn, hn):
        acc = _dot(xs_ref[...], w_ref[:, c0:c0 + hn].astype(BF16))
        acc = acc * r + sw_ref[j, pl.ds(midx, 1), c0:c0 + hn] * lead
        if act == "relu2":
            acc = jnp.maximum(acc, 0.0)
            acc = acc * acc
        if not rope_col_blocks:
            o_ref[:, c0:c0 + hn] = acc.astype(o_ref.dtype)
            continue

        @pl.when(do_rope)
        def _(acc=acc, c0=c0):
            hd = cos_ref.shape[1]
            lane = lax.broadcasted_iota(jnp.int32, (bm, hd), 1)
            first = (lane % (hd // 2)) < hd // 4
            for c in range(hn // hd):
                xh = acc[:, c * hd:(c + 1) * hd]
                partner = jnp.where(first, pltpu.roll(xh, hd - hd // 4, 1), pltpu.roll(xh, hd // 4, 1))
                o_ref[:, c0 + c * hd:c0 + (c + 1) * hd] = (xh * cos_ref[...] + partner * sin_ref[...]).astype(o_ref.dtype)

        @pl.when(jnp.logical_not(do_rope))
        def _(acc=acc, c0=c0):
            o_ref[:, c0:c0 + hn] = acc.astype(o_ref.dtype)


def _mm_pre(xs, ssq, mod_l, shift_k, w_stack, layer, m_rows, geom, act=None, rope=None, col_start=0, n_cols=None,
            out_dtype=BF16, lead=None):
    d = xs.shape[1]
    k = w_stack.shape[1]
    n = n_cols or w_stack.shape[2]
    bm, bn = _mm_blocks(m_rows, n)
    assert col_start % bn == 0
    cb0 = col_start // bn
    kern = functools.partial(
        _mm_pre_kernel, bm=bm, d_norm=d, shift_k=shift_k, act=act,
        lead_blocks=(lead[0] // bn if lead else 0), lead_scale=(lead[1] if lead else 1.0),
        rope_row_blocks=(rope[2] // bm if rope else 0), rope_col_blocks=(rope[3] // bn if rope else 0), **geom)
    in_specs = [
        pl.BlockSpec((bm, d), lambda i, j: (i, 0)),
        pl.BlockSpec((bm, LANES), lambda i, j: (i, 0)),
        pl.BlockSpec(mod_l.shape, lambda i, j: (0, 0)),
        pl.BlockSpec((None, k, bn), lambda i, j: (layer, 0, cb0 + j)),
    ]
    args = [xs, ssq, mod_l, w_stack]
    if rope:
        seq_blocks = rope[0].shape[0] // bm
        hd = rope[0].shape[1]
        in_specs += [pl.BlockSpec((bm, hd), lambda i, j: (i % seq_blocks, 0))] * 2
        args += [rope[0], rope[1]]
    return pl.pallas_call(
        kern,
        grid=(m_rows // bm, n // bn),
        in_specs=in_specs,
        out_specs=pl.BlockSpec((bm, bn), lambda i, j: (i, j)),
        out_shape=jax.ShapeDtypeStruct((m_rows, n), out_dtype),
        scratch_shapes=[pltpu.VMEM((n // bn, 8, bn), F32)],
        compiler_params=_params(2, VMEM_LIMIT_F32_WEIGHTS),
        name="mm_pre" + ("_" + act if act else "") + ("_rope" if rope else ""),
    )(*args)


def _mm_res_kernel(*refs, bm, n_k, row_off, emit_next, t_lat, seq, n_batch):
    x_ref, w_ref, res_ref, gate_ref = refs[:4]
    pos = 4
    if emit_next:
        nw_ref, nscale_ref = refs[pos:pos + 2]
        pos += 4
    o_ref = refs[pos]
    pos += 1
    if emit_next:
        xs_ref, ssq_ref = refs[pos:pos + 2]
        pos += 2
    acc_ref = refs[pos].at[pl.program_id(2)] if n_k > 1 else None
    i, kk, j = pl.program_id(0), pl.program_id(1), pl.program_id(2)
    midx = _mod_row(row_off + i * bm, t_lat, seq, n_batch)
    bn = o_ref.shape[1]
    slices = [slice(c0, c0 + MM_RES_INNER_COLS) for c0 in range(0, bn, MM_RES_INNER_COLS)]
    parts = [_dot(x_ref[...], w_ref[:, cs]) for cs in slices]

    def finish(accs):
        sums = None
        for cs, acc in zip(slices, accs):
            hb = res_ref[:, cs] + gate_ref[pl.ds(midx, 1), cs] * acc
            o_ref[:, cs] = hb
            if emit_next:
                g = nw_ref[:, cs] * (1.0 + nscale_ref[pl.ds(midx, 1), cs])
                xs_ref[:, cs] = (hb * g).astype(xs_ref.dtype)
                sq = hb * hb
                for c in range(0, sq.shape[1], LANES):
                    sums = sq[:, c:c + LANES] if sums is None else sums + sq[:, c:c + LANES]
        if emit_next:
            _accumulate_over_columns(ssq_ref, sums, j)

    if n_k == 1:
        finish(parts)
        return

    @pl.when(kk == 0)
    def _():
        for cs, part in zip(slices, parts):
            acc_ref[:, cs] = part

    @pl.when((kk > 0) & (kk < n_k - 1))
    def _():
        for cs, part in zip(slices, parts):
            acc_ref[:, cs] = acc_ref[:, cs] + part

    @pl.when(kk == n_k - 1)
    def _():
        finish([acc_ref[:, cs] + part for cs, part in zip(slices, parts)])


def _mm_res(x, w_stack, layer, bufs, mod_l, gate_k, m_rows, geom, row_off=0, next_norm=None):
    h, xs, ssq = bufs
    k, n = w_stack.shape[1:]
    n_k = 2 if k >= 8192 else 1
    kb = k // n_k
    bm = 1024 if m_rows % 1024 == 0 else 256
    bn = 512 if k >= 4096 else 1024
    assert row_off % bm == 0 and n % bn == 0
    blk_off = row_off // bm
    nb = n // bn
    emit_next = next_norm is not None

    def out_col(kk, j):
        return jnp.where(kk == n_k - 1, j, 0)

    in_specs = [
        pl.BlockSpec((bm, kb), lambda i, kk, j: (i, kk)),
        pl.BlockSpec((None, kb, bn), lambda i, kk, j: (layer, kk, j)),
        pl.BlockSpec((bm, bn), lambda i, kk, j: (blk_off + i, out_col(kk, j))),
        pl.BlockSpec((8, bn), lambda i, kk, j: (0, gate_k * nb + j)),
    ]
    args = [x, w_stack, h, mod_l]
    out_specs = [pl.BlockSpec((bm, bn), lambda i, kk, j: (blk_off + i, out_col(kk, j)))]
    out_shape = [jax.ShapeDtypeStruct(h.shape, F32)]
    aliases = {2: 0}
    if emit_next:
        nw, mod_next, scale_slot = next_norm
        in_specs += [
            pl.BlockSpec((1, bn), lambda i, kk, j: (0, j)),
            pl.BlockSpec((8, bn), lambda i, kk, j: (0, scale_slot * nb + j)),
            pl.BlockSpec(memory_space=pl.ANY),
            pl.BlockSpec(memory_space=pl.ANY),
        ]
        args += [nw.reshape(1, n), mod_next, xs, ssq]
        out_specs += [
            pl.BlockSpec((bm, bn), lambda i, kk, j: (blk_off + i, out_col(kk, j))),
            pl.BlockSpec((bm, LANES), lambda i, kk, j: (blk_off + i, 0)),
        ]
        out_shape += [jax.ShapeDtypeStruct(xs.shape, BF16), jax.ShapeDtypeStruct(ssq.shape, F32)]
        aliases.update({6: 1, 7: 2})
    outs = pl.pallas_call(
        functools.partial(_mm_res_kernel, bm=bm, n_k=n_k, row_off=row_off, emit_next=emit_next, **geom),
        grid=(m_rows // bm, n_k, nb),
        in_specs=in_specs,
        out_specs=out_specs,
        out_shape=out_shape,
        scratch_shapes=[pltpu.VMEM((nb, bm, bn), F32)] if n_k > 1 else [],
        input_output_aliases=aliases,
        compiler_params=_params(3),
        name="mm_res",
    )(*args)
    return (outs[0], outs[1], outs[2]) if emit_next else (outs[0], xs, ssq)


def _na_pair_layout(rows):
    gq, gk = NA_GROUP_ROWS, NA_KEY_ROWS
    pairs = {}
    idx = np.zeros((3, gq, gk // 2), np.int32)
    for typ, (r0, ks) in enumerate(((0, 0), (gq, 0), (rows - gq, rows - gk))):
        for i in range(gq):
            r = r0 + i
            rs = min(max(r - NA_WIN_ROWS // 2, 0), rows - NA_WIN_ROWS)
            dr = [kr - r + NA_WIN_ROWS - 1 if rs <= kr < rs + NA_WIN_ROWS else -1 for kr in range(ks, ks + gk)]
            for p in range(gk // 2):
                idx[typ, i, p] = pairs.setdefault((dr[2 * p], dr[2 * p + 1]), len(pairs))
    return list(pairs), idx


def _na_pair_table(rpb, pairs, out_scale):
    heads = rpb.shape[0]
    col = np.arange(GRID_W)
    col_start = np.clip(col - NA_WIN_COLS // 2, 0, GRID_W - NA_WIN_COLS)
    col_mask = (col[None, :] >= col_start[:, None]) & (col[None, :] < col_start[:, None] + NA_WIN_COLS)
    dc_idx = np.clip(col[None, :] - col[:, None], -(NA_WIN_COLS - 1), NA_WIN_COLS - 1) + NA_WIN_COLS - 1
    onehot = (dc_idx[None] == np.arange(2 * NA_WIN_COLS - 1)[:, None, None]).astype(np.float32)
    per_dr = jnp.einsum("hrd,dqk->hrqk", rpb.astype(F32), onehot, precision=lax.Precision.HIGHEST)
    per_dr = jnp.where(col_mask[None, None], per_dr * out_scale, MASK_NEG)
    masked = jnp.full((heads, GRID_W, GRID_W), MASK_NEG, F32)
    block = lambda dr: masked if dr < 0 else per_dr[:, dr]
    return jnp.stack([jnp.concatenate([block(a), block(b)], axis=-1) for a, b in pairs], axis=1)


def _softmax_pv(scores, values, out_dtype):
    m = functools.reduce(jnp.maximum, [jnp.max(t, axis=1, keepdims=True) for t in scores])
    dh = values[0].shape[1]
    o = None
    for t, v in zip(scores, values):
        p = jnp.exp2(t - m).astype(BF16)
        part = _dot(p, jnp.concatenate([v, jnp.ones_like(v)], axis=1))
        o = part if o is None else o + part
    return (o[:, :dh] * (1.0 / o[:, dh:])).astype(out_dtype)


def _na_kernel(idx_ref, q_ref, k0_ref, k1_ref, k2_ref, v0_ref, v1_ref, v2_ref, kc_ref, vc_ref, pair_ref, o_ref,
               *, n_groups):
    dh = NA_HEAD_DIM
    gq, n_pairs = NA_GROUP_ROWS, NA_KEY_ROWS // 2
    g = pl.program_id(1)
    typ = jnp.where(g == 0, 0, jnp.where(g == n_groups - 1, 2, 1))
    for hh in range(NA_HEADS_PER_STEP):
        sl = slice(hh * dh, (hh + 1) * dh)
        q = q_ref[:, sl]
        bias = jnp.concatenate(
            [jnp.concatenate([pair_ref[hh, idx_ref[(typ * gq + i) * n_pairs + p]] for p in range(n_pairs)], axis=1)
             for i in range(gq)], axis=0)
        s_loc = jnp.concatenate([_dot_nt(q, kr[:, sl]) for kr in (k0_ref, k1_ref, k2_ref)], axis=1)
        t_loc = s_loc + bias
        t_ctx = _dot_nt(q, kc_ref[:, sl])
        kb = k0_ref.shape[0]
        scores = [t_loc[:, j * kb:(j + 1) * kb] for j in range(3)] + [t_ctx]
        values = [vr[:, sl] for vr in (v0_ref, v1_ref, v2_ref, vc_ref)]
        o_ref[:, sl] = _softmax_pv(scores, values, o_ref.dtype)


def _na_ctx_kernel(q_ref, kc_ref, vc_ref, o_ref):
    dh = NA_HEAD_DIM
    for hh in range(NA_HEADS_PER_STEP):
        sl = slice(hh * dh, (hh + 1) * dh)
        t_ctx = _dot_nt(q_ref[:, sl], kc_ref[:, sl])
        o_ref[:, sl] = _softmax_pv([t_ctx], [vc_ref[:, sl]], o_ref.dtype)


def _na_attention(qkv, rpb, n_batch, seq, lc, ctx_out):
    hd = NA_HEADS * NA_HEAD_DIM
    gtok = NA_GROUP_ROWS * GRID_W
    ng = seq // gtok
    assert lc == gtok and ng >= 4
    hw = NA_HEADS_PER_STEP * NA_HEAD_DIM
    nhq = NA_HEADS // NA_HEADS_PER_STEP
    t_lat = n_batch * seq
    ctx_blk0 = n_batch * ng
    pairs, idx = _na_pair_layout(seq // GRID_W)
    pair_tab = _na_pair_table(rpb, pairs, math.log2(math.e))

    def k_row(g, b, j):
        return b * ng + jnp.clip(g - 1, 0, ng - 3) + j

    blk = (gtok, hw)
    in_specs = [pl.BlockSpec(memory_space=pltpu.SMEM), pl.BlockSpec(blk, lambda h, g, b: (b * ng + g, h))]
    for col0 in (nhq, 2 * nhq):
        for j in range(3):
            in_specs.append(pl.BlockSpec(blk, lambda h, g, b, j=j, col0=col0: (k_row(g, b, j), col0 + h)))
    in_specs.append(pl.BlockSpec(blk, lambda h, g, b: (ctx_blk0 + b, nhq + h)))
    in_specs.append(pl.BlockSpec(blk, lambda h, g, b: (ctx_blk0 + b, 2 * nhq + h)))
    in_specs.append(pl.BlockSpec((NA_HEADS_PER_STEP, len(pairs), GRID_W, 2 * GRID_W), lambda h, g, b: (h, 0, 0, 0)))
    y_lat = pl.pallas_call(
        functools.partial(_na_kernel, n_groups=ng),
        grid=(nhq, ng, n_batch),
        in_specs=in_specs,
        out_specs=pl.BlockSpec(blk, lambda h, g, b: (b * ng + g, h)),
        out_shape=jax.ShapeDtypeStruct((t_lat, hd), BF16),
        compiler_params=_params(3),
        name="na_attention",
    )(jnp.asarray(idx.reshape(-1)), *([qkv] * 9), pair_tab)
    if not ctx_out:
        return y_lat, None
    y_ctx = pl.pallas_call(
        _na_ctx_kernel,
        grid=(nhq, n_batch),
        in_specs=[
            pl.BlockSpec(blk, lambda h, b: (ctx_blk0 + b, h)),
            pl.BlockSpec(blk, lambda h, b: (ctx_blk0 + b, nhq + h)),
            pl.BlockSpec(blk, lambda h, b: (ctx_blk0 + b, 2 * nhq + h)),
        ],
        out_specs=pl.BlockSpec(blk, lambda h, b: (b, h)),
        out_shape=jax.ShapeDtypeStruct((n_batch * lc, hd), BF16),
        compiler_params=_params(2),
        name="na_attention_ctx",
    )(qkv, qkv, qkv)
    return y_lat, y_ctx


def _rope_tables(seq, width):
    t = jnp.arange(seq)
    pos = jnp.stack([t // GRID_W, t % GRID_W], axis=-1).astype(F32)
    n_freq = DA_HEAD_DIM // 4
    inv_freq = ROPE_BASE ** (-jnp.arange(n_freq, dtype=F32) / n_freq)
    ang = pos[:, :, None] * inv_freq
    cos = jnp.broadcast_to(jnp.cos(ang)[:, :, None, :], (seq, 2, 2, n_freq)).reshape(seq, DA_HEAD_DIM)
    sign = jnp.array([-1.0, 1.0], F32)[None, None, :, None]
    sin = (jnp.broadcast_to(jnp.sin(ang)[:, :, None, :], (seq, 2, 2, n_freq)) * sign).reshape(seq, DA_HEAD_DIM)
    reps = width // DA_HEAD_DIM
    return jnp.tile(cos, (1, reps)), jnp.tile(sin, (1, reps))


def _da_kernel(lam_ref, q_ref, *refs, out_scale, n_sub, with_latent):
    n_kv = 2 if with_latent else 1
    k_refs, v_refs = refs[:n_kv], refs[n_kv:2 * n_kv]
    sw_ref, o_ref = refs[2 * n_kv], refs[2 * n_kv + 1]
    dh = DA_HEAD_DIM
    lam = lam_ref[0]
    rows = q_ref.shape[0] // n_sub
    for sub in range(n_sub):
        rs = slice(sub * rows, (sub + 1) * rows)
        probs, sums = [], []
        for comp in range(2):
            sl = slice(comp * dh, (comp + 1) * dh)
            q = q_ref[rs, sl]
            s = [_dot_nt(q, k_ref[:, sl]) for k_ref in k_refs]
            m = functools.reduce(jnp.maximum, [jnp.max(x, axis=1, keepdims=True) for x in s])
            p = [jnp.exp2(x - m) for x in s]
            probs.append(p)
            sums.append(functools.reduce(jnp.add, [jnp.sum(x, axis=1, keepdims=True) for x in p]))
        ratio = lam * sums[0] / sums[1]
        o = None
        for p1, p2, v_ref in zip(probs[0], probs[1], v_refs):
            part = _dot((p1 - ratio * p2).astype(BF16), v_ref[...])
            o = part if o is None else o + part
        o = o * (1.0 / sums[0])
        y = o * lax.rsqrt(jnp.mean(o * o, axis=-1, keepdims=True) + DA_SUBLN_EPS) * sw_ref[...]
        o_ref[rs, :] = (y * out_scale).astype(o_ref.dtype)


def _diff_attention(qkv, lam_full, subln_w, lambda_init, n_batch, seq, lc, ctx_out):
    hw = 2 * DA_HEAD_DIM
    bq, n_sub = DA_Q_BLOCK, DA_Q_SUBBLOCKS
    nq = seq // bq
    t_lat = n_batch * seq
    ctx_blk0 = t_lat // lc
    lam1 = lam_full.reshape(1).astype(F32)
    sw = subln_w.reshape(1, hw)
    common = dict(out_scale=1.0 - lambda_init)
    smem = pl.BlockSpec(memory_space=pltpu.SMEM)
    y_lat = pl.pallas_call(
        functools.partial(_da_kernel, n_sub=n_sub, with_latent=True, **common),
        grid=(n_batch, DA_HEADS, nq),
        in_specs=[
            smem,
            pl.BlockSpec((bq, hw), lambda b, h, i: (b * nq + i, h)),
            pl.BlockSpec((seq, hw), lambda b, h, i: (b, DA_HEADS + h)),
            pl.BlockSpec((lc, hw), lambda b, h, i: (ctx_blk0 + b, DA_HEADS + h)),
            pl.BlockSpec((seq, hw), lambda b, h, i: (b, 2 * DA_HEADS + h)),
            pl.BlockSpec((lc, hw), lambda b, h, i: (ctx_blk0 + b, 2 * DA_HEADS + h)),
            pl.BlockSpec((1, hw), lambda b, h, i: (0, 0)),
        ],
        out_specs=pl.BlockSpec((bq, hw), lambda b, h, i: (b * nq + i, h)),
        out_shape=jax.ShapeDtypeStruct((t_lat, DA_HEADS * hw), BF16),
        compiler_params=_params(3, VMEM_LIMIT_F32_WEIGHTS),
        name="diff_attention",
    )(lam1, qkv, qkv, qkv, qkv, qkv, sw)
    if not ctx_out:
        return y_lat, None
    y_ctx = pl.pallas_call(
        functools.partial(_da_kernel, n_sub=1, with_latent=False, **common),
        grid=(n_batch, DA_HEADS),
        in_specs=[
            smem,
            pl.BlockSpec((lc, hw), lambda b, h: (ctx_blk0 + b, h)),
            pl.BlockSpec((lc, hw), lambda b, h: (ctx_blk0 + b, DA_HEADS + h)),
            pl.BlockSpec((lc, hw), lambda b, h: (ctx_blk0 + b, 2 * DA_HEADS + h)),
            pl.BlockSpec((1, hw), lambda b, h: (0, 0)),
        ],
        out_specs=pl.BlockSpec((lc, hw), lambda b, h: (b, h)),
        out_shape=jax.ShapeDtypeStruct((n_batch * lc, DA_HEADS * hw), BF16),
        compiler_params=_params(2),
        name="diff_attention_ctx",
    )(lam1, qkv, qkv, qkv, sw)
    return y_lat, y_ctx


def _conv_kernel(prev_ref, cur_ref, next_ref, w_ref, b_ref, o_ref, *, rb, lat_blocks, lat_blocks_per_seq, ctx_blocks_per_seq):
    i = pl.program_id(0)
    is_lat = i < lat_blocks
    pos = jnp.where(is_lat, i % lat_blocks_per_seq, (i - lat_blocks) % ctx_blocks_per_seq)
    per_seq = jnp.where(is_lat, lat_blocks_per_seq, ctx_blocks_per_seq)
    halo = prev_ref.shape[0]
    prev = jnp.where(pos == 0, 0.0, prev_ref[...].astype(F32))
    nxt = jnp.where(pos == per_seq - 1, 0.0, next_ref[...].astype(F32))
    cat = jnp.concatenate([prev, cur_ref[...].astype(F32), nxt], axis=0)
    n = rb + 2 * halo
    acc = b_ref[...]
    for k in range(SSM_CONV):
        shifted = cat if k == SSM_CONV // 2 else pltpu.roll(cat, (SSM_CONV // 2 - k) % n, 0)
        acc = acc + w_ref[k:k + 1, :] * shifted[halo:halo + rb]
    o_ref[...] = (acc * _sigmoid(acc)).astype(o_ref.dtype)


def _ssm_conv(proj, conv_w, conv_b, n_batch, seq, lc):
    t = proj.shape[0]
    rb, ct, halo = 256, 2048, SUBLANES_BF16
    assert seq % rb == 0 and lc % rb == 0 and SSM_D_INNER % ct == 0 and SSM_CONV_DIM % ct == 0
    col0 = SSM_D_INNER // ct
    hb = rb // halo
    n_halo_blocks = t // halo
    kern = functools.partial(_conv_kernel, rb=rb, lat_blocks=n_batch * seq // rb,
                             lat_blocks_per_seq=seq // rb, ctx_blocks_per_seq=lc // rb)
    return pl.pallas_call(
        kern,
        grid=(t // rb, SSM_CONV_DIM // ct),
        in_specs=[
            pl.BlockSpec((halo, ct), lambda i, j: (jnp.maximum(i * hb - 1, 0), col0 + j)),
            pl.BlockSpec((rb, ct), lambda i, j: (i, col0 + j)),
            pl.BlockSpec((halo, ct), lambda i, j: (jnp.minimum((i + 1) * hb, n_halo_blocks - 1), col0 + j)),
            pl.BlockSpec((SSM_CONV, ct), lambda i, j: (0, j)),
            pl.BlockSpec((1, ct), lambda i, j: (0, j)),
        ],
        out_specs=pl.BlockSpec((rb, ct), lambda i, j: (i, j)),
        out_shape=jax.ShapeDtypeStruct((t, SSM_CONV_DIM), BF16),
        compiler_params=_params(2),
        name="ssm_conv",
    )(proj, proj, proj, conv_w, conv_b.reshape(1, SSM_CONV_DIM))


def _ssd_kernel(xbc_ref, dt_ref, dtb_ref, alog_ref, tri_ref, half_ref, y_ref, state_ref):
    q = SSM_CHUNK
    nh = SSM_HEADS
    p2 = 2 * SSM_HEAD_DIM
    gw = SSM_GROUP_WIDTH
    log2e = math.log2(math.e)
    direction = pl.program_id(1)
    fwd = direction == 0

    @pl.when(pl.program_id(2) == 0)
    def _():
        state_ref[...] = jnp.zeros_like(state_ref)

    pre = dt_ref[...] + dtb_ref[...]
    dt_all = jnp.maximum(pre, 0.0) + jnp.log1p(jnp.exp(-jnp.abs(pre)))
    a_all = dt_all * (jnp.exp(alog_ref[...]) * -log2e)
    tri = tri_ref[0]
    acum_all = sum(_dot(tri, piece) for piece in _split3(a_all))
    acum = jnp.where(fwd, acum_all, pltpu.roll(acum_all, nh, 1))
    acum_t = acum.T
    dt_t = jnp.where(fwd, dt_all, pltpu.roll(dt_all, nh, 1)).T
    src_t = acum_t - jnp.log(dt_t) * log2e
    lane = lax.broadcasted_iota(jnp.int32, (q, q), 1)
    last = jnp.where(fwd, q - 1, 0)
    total = jnp.sum(jnp.where(lane == last, acum_t, 0.0), axis=1, keepdims=True)
    to_end_t = dt_t * jnp.exp2(total - acum_t)
    chunk_decay = jnp.broadcast_to(jnp.exp2(total), (q, q))
    seen = tri.astype(F32) > 0.5
    first_head = lane < SSM_HEAD_DIM
    c_off = SSM_D_INNER + SSM_GROUPS * SSM_STATE

    for g in range(SSM_GROUPS):
        b_g = xbc_ref[:, SSM_D_INNER + g * SSM_STATE:SSM_D_INNER + (g + 1) * SSM_STATE]
        c_g = xbc_ref[:, c_off + g * SSM_STATE:c_off + (g + 1) * SSM_STATE]
        cb = _dot_nt(c_g, b_g)
        b_t = b_g.astype(F32).T
        y_carried = _dot(c_g, state_ref[g].astype(BF16))
        for pr in range(SSM_HEADS_PER_GROUP // 2):
            ps = slice(pr * p2, (pr + 1) * p2)
            cols = slice(g * gw + pr * p2, g * gw + (pr + 1) * p2)
            x2 = xbc_ref[:, cols]
            x_bd = jnp.concatenate([x2 * half_ref[0], x2 * half_ref[1]], axis=0)
            intra, to_state, carry_scale, decay = [], [], [], []
            for r in (2 * pr, 2 * pr + 1):
                hh = g * SSM_HEADS_PER_GROUP + r
                col = jnp.sum(jnp.where(lane == hh, acum, 0.0), axis=1, keepdims=True)
                intra.append((cb * jnp.where(seen, jnp.exp2(col - src_t[hh:hh + 1, :]), 0.0)).astype(BF16))
                to_state.append((b_t * to_end_t[hh:hh + 1, :]).astype(BF16))
                carry_scale.append(jnp.exp2(col))
                decay.append(chunk_decay[hh:hh + 1, :])
            y_ref[0, :, cols] = (_dot(jnp.concatenate(intra, axis=1), x_bd)
                                 + jnp.where(first_head, carry_scale[0], carry_scale[1]) * y_carried[:, ps]
                                 ).astype(y_ref.dtype)
            state_ref[g, :, ps] = (state_ref[g, :, ps] * jnp.where(first_head[:1], decay[0], decay[1])
                                   + _dot(jnp.concatenate(to_state, axis=1), x_bd))


def _ssd_scan(xbc, dt_raw, dt_bias, a_log, n_batch, seq, lc):
    t = xbc.shape[0]
    q = SSM_CHUNK
    n_lat, n_ctx = seq // q, lc // q
    steps = n_lat + n_ctx
    ctx_blk0 = n_batch * n_lat

    def blk(b, d, s):
        ctx_chunk = jnp.where(d == 0, s, n_ctx - 1 - s)
        lat_chunk = jnp.where(d == 0, s - n_ctx, steps - 1 - s)
        return jnp.where(s < n_ctx, ctx_blk0 + b * n_ctx + ctx_chunk, b * n_lat + lat_chunk)

    idx = np.arange(q)
    tri = np.stack([idx[None, :] <= idx[:, None], idx[None, :] >= idx[:, None]]).astype(np.float32)
    first = np.broadcast_to(np.arange(2 * SSM_HEAD_DIM)[None, :] < SSM_HEAD_DIM, (q, 2 * SSM_HEAD_DIM))
    half = np.stack([first, ~first]).astype(np.float32)
    return pl.pallas_call(
        _ssd_kernel,
        grid=(n_batch, 2, steps),
        in_specs=[
            pl.BlockSpec((q, SSM_CONV_DIM), lambda b, d, s: (blk(b, d, s), 0)),
            pl.BlockSpec((q, 2 * SSM_HEADS), lambda b, d, s: (blk(b, d, s), 0)),
            pl.BlockSpec((1, 2 * SSM_HEADS), lambda b, d, s: (0, 0)),
            pl.BlockSpec((1, 2 * SSM_HEADS), lambda b, d, s: (0, 0)),
            pl.BlockSpec((1, q, q), lambda b, d, s: (d, 0, 0)),
            pl.BlockSpec((2, q, 2 * SSM_HEAD_DIM), lambda b, d, s: (0, 0, 0)),
        ],
        out_specs=pl.BlockSpec((1, q, SSM_D_INNER), lambda b, d, s: (d, blk(b, d, s), 0)),
        out_shape=jax.ShapeDtypeStruct((2, t, SSM_D_INNER), BF16),
        scratch_shapes=[pltpu.VMEM((SSM_GROUPS, SSM_STATE, SSM_GROUP_WIDTH), F32)],
        compiler_params=_params(3),
        name="ssd_scan",
    )(xbc, dt_raw, dt_bias.reshape(1, -1), a_log.reshape(1, -1), jnp.asarray(tri, BF16), jnp.asarray(half, BF16))


def _ssm_finish_kernel(yf_ref, yb_ref, x_ref, z_ref, dsk_ref, nw_ref, o_ref):
    gw = SSM_GROUP_WIDTH
    z = z_ref[...].astype(F32)
    y = yf_ref[0].astype(F32) + yb_ref[0].astype(F32) + x_ref[...].astype(F32) * dsk_ref[...]
    y = y * (z * _sigmoid(z))
    for g in range(SSM_GROUPS):
        cs = slice(g * gw, (g + 1) * gw)
        yg = y[:, cs]
        o_ref[:, cs] = (yg * lax.rsqrt(jnp.mean(yg * yg, axis=-1, keepdims=True) + EPS) * nw_ref[:, cs]).astype(o_ref.dtype)


def _ssm_finish(y2, xbc, proj, d_skip, norm_w):
    t = xbc.shape[0]
    di = SSM_D_INNER
    rb = 256
    dsk = jnp.repeat(d_skip.astype(F32), SSM_HEAD_DIM).reshape(1, di)
    return pl.pallas_call(
        _ssm_finish_kernel,
        grid=(t // rb,),
        in_specs=[
            pl.BlockSpec((1, rb, di), lambda i: (0, i, 0)),
            pl.BlockSpec((1, rb, di), lambda i: (1, i, 0)),
            pl.BlockSpec((rb, di), lambda i: (i, 0)),
            pl.BlockSpec((rb, di), lambda i: (i, 0)),
            pl.BlockSpec((1, di), lambda i: (0, 0)),
            pl.BlockSpec((1, di), lambda i: (0, 0)),
        ],
        out_specs=pl.BlockSpec((rb, di), lambda i: (i, 0)),
        out_shape=jax.ShapeDtypeStruct((t, di), BF16),
        compiler_params=_params(1),
        name="ssm_finish",
    )(y2, y2, xbc, proj, dsk, norm_w.reshape(1, di))


def _mamba2(xs, ssq, mod_l, geom, w_in_stack, layer, conv_w, conv_b, dt_bias, a_log, d_skip, norm_w, n_batch, seq, lc):
    t = xs.shape[0]
    n_main = SSM_D_INNER + SSM_CONV_DIM
    proj = _mm_pre(xs, ssq, mod_l, 0, w_in_stack, layer, t, geom, n_cols=n_main)
    dt_raw = _mm_pre(xs, ssq, mod_l, 0, w_in_stack, layer, t, geom, col_start=n_main, n_cols=2 * SSM_HEADS,
                     out_dtype=F32)
    xbc = _ssm_conv(proj, conv_w, conv_b, n_batch, seq, lc)
    y2 = _ssd_scan(xbc, dt_raw, dt_bias, a_log, n_batch, seq, lc)
    return _ssm_finish(y2, xbc, proj, d_skip, norm_w)


def kernel(x, c, ctx, c_ctx, ada_w, ada_b, norm_mix_w, norm_mlp_w, mlp_w1, mlp_w2, na_w_qkv, na_w_o, na_rpb,
           ssm_w_in, ssm_conv_w, ssm_conv_b, ssm_dt_bias, ssm_a_log, ssm_d, ssm_norm_w, ssm_w_out,
           da_w_qkv, da_w_o, da_lambda, da_subln_w, final_norm_w):
    n_batch, seq, d = x.shape
    lc = ctx.shape[1]
    t_lat, t_ctx = n_batch * seq, n_batch * lc
    t = t_lat + t_ctx
    geom = dict(t_lat=t_lat, seq=seq, n_batch=n_batch)
    assert d == D_MODEL and seq % GRID_W == 0 and n_batch < 8

    cond = jnp.concatenate([c, c_ctx[None, :], jnp.zeros((8 - n_batch - 1, d), F32)], axis=0)
    mod = _adaln_all(cond, ada_w, ada_b)
    bufs = _prep(x.reshape(t_lat, d), ctx.reshape(t_ctx, d), norm_mix_w[0], mod[0], geom)
    mlp_w2_bf16 = mlp_w2.astype(BF16)

    for i in range(DEPTH):
        last = i == DEPTH - 1
        mixer, j = i % N_MIXERS, i // N_MIXERS
        y_ctx = None
        mod_l = mod[i]
        rows_out = t_lat if last else t
        _, xs, ssq = bufs
        if mixer == 0:
            qkv = _mm_pre(xs, ssq, mod_l, 0, na_w_qkv, j, t, geom, lead=(NA_HEADS * NA_HEAD_DIM, ATTN_EXP2_SCALE))
            y, y_ctx = _na_attention(qkv, na_rpb[j], n_batch, seq, lc, not last)
            w_o = na_w_o
        elif mixer == 1:
            y = _mamba2(xs, ssq, mod_l, geom, ssm_w_in, j, ssm_conv_w[j], ssm_conv_b[j], ssm_dt_bias[j],
                        ssm_a_log[j], ssm_d[j], ssm_norm_w[j], n_batch, seq, lc)
            w_o = ssm_w_out
        else:
            lambda_init = 0.8 - 0.6 * math.exp(-0.3 * i)
            qk_w = 2 * DA_HEADS * DA_HEAD_DIM
            cos_t, sin_t = _rope_tables(seq, DA_HEAD_DIM)
            qkv = _mm_pre(xs, ssq, mod_l, 0, da_w_qkv, j, t, geom, rope=(cos_t, sin_t, t_lat, 2 * qk_w),
                          lead=(qk_w, ATTN_EXP2_SCALE))
            lam = da_lambda[j].astype(F32)
            lam_full = jnp.exp(jnp.sum(lam[0] * lam[1])) - jnp.exp(jnp.sum(lam[2] * lam[3])) + lambda_init
            y, y_ctx = _diff_attention(qkv, lam_full, da_subln_w[j], lambda_init, n_batch, seq, lc, not last)
            w_o = da_w_o
        w_o = w_o.astype(BF16)
        mlp_norm = (norm_mlp_w[i], mod_l, 4)
        if y_ctx is None:
            bufs = _mm_res(y, w_o, j, bufs, mod_l, 2, rows_out, geom, next_norm=mlp_norm)
        else:
            bufs = _mm_res(y, w_o, j, bufs, mod_l, 2, t_lat, geom, next_norm=mlp_norm)
            bufs = _mm_res(y_ctx, w_o, j, bufs, mod_l, 2, t_ctx, geom, row_off=t_lat, next_norm=mlp_norm)
        hidden = _mm_pre(bufs[1], bufs[2], mod_l, 3, mlp_w1, i, rows_out, geom, act="relu2")
        mixer_norm = None if last else (norm_mix_w[i + 1], mod[i + 1], 1)
        bufs = _mm_res(hidden, mlp_w2_bf16, i, bufs, mod_l, 5, rows_out, geom, next_norm=mixer_norm)

    return _final_norm(bufs[0], final_norm_w, t_lat).reshape(n_batch, seq, d)
```

```python
import functools
import math

import jax
import jax.numpy as jnp
import numpy as np
from jax import lax
from jax.experimental import pallas as pl
from jax.experimental.pallas import tpu as pltpu

F32 = jnp.float32
BF16 = jnp.bfloat16

D_MODEL = 2048
DEPTH = 4
GRID_W = 64
N_MIXERS = 3
EPS = 1e-6

NA_HEADS = 16
NA_HEAD_DIM = 128
NA_WIN_ROWS = 8
NA_WIN_COLS = 16
NA_GROUP_ROWS = 4
NA_KEY_ROWS = 3 * NA_GROUP_ROWS
NA_HEADS_PER_STEP = 8

SSM_D_INNER = 4096
SSM_HEAD_DIM = 64
SSM_HEADS = 64
SSM_GROUPS = 8
SSM_HEADS_PER_GROUP = 8
SSM_STATE = 128
SSM_CONV = 5
SSM_CHUNK = 128
SSM_CONV_DIM = SSM_D_INNER + 2 * SSM_GROUPS * SSM_STATE
SSM_GROUP_WIDTH = SSM_HEADS_PER_GROUP * SSM_HEAD_DIM

DA_HEADS = 8
DA_HEAD_DIM = 128
DA_SUBLN_EPS = 1e-5
DA_Q_BLOCK = 2048
DA_Q_SUBBLOCKS = 8
ROPE_BASE = 10000.0

MASK_NEG = -1e30
assert NA_HEAD_DIM == DA_HEAD_DIM
ATTN_EXP2_SCALE = NA_HEAD_DIM ** -0.5 * math.log2(math.e)

V7X_VMEM_BYTES = 64 * 1024 * 1024
VMEM_LIMIT = (V7X_VMEM_BYTES * 3) // 4
MM_INNER_COLS = 1024
MM_RES_INNER_COLS = 512
VMEM_LIMIT_F32_WEIGHTS = (V7X_VMEM_BYTES * 29) // 32
SUBLANES_BF16 = 16
LANES = 128


def _params(n_grid_dims, vmem_limit=VMEM_LIMIT):
    return pltpu.CompilerParams(
        dimension_semantics=("arbitrary",) * n_grid_dims, vmem_limit_bytes=vmem_limit)


def _sigmoid(x):
    return 1.0 / (1.0 + jnp.exp(-x))


def _dot(a, b):
    return jnp.dot(a, b, preferred_element_type=F32)


def _dot_nt(a, b):
    return lax.dot_general(a, b, (((1,), (1,)), ((), ())), preferred_element_type=F32)


def _split3(v):
    hi = v.astype(BF16)
    r1 = v - hi.astype(F32)
    mid = r1.astype(BF16)
    lo = (r1 - mid.astype(F32)).astype(BF16)
    return hi, mid, lo


def _mod_row(row0, t_lat, seq, n_batch):
    return jnp.where(row0 < t_lat, row0 // seq, n_batch)


def _adaln_kernel(c_ref, w_ref, b_ref, o_ref):
    x = c_ref[...]
    x = (x * _sigmoid(x)).astype(BF16)
    o_ref[0] = _dot(x, w_ref[0].astype(BF16)) + b_ref[0]


def _adaln_all(cond, ada_w, ada_b):
    depth, d, n = ada_w.shape
    bn = 1024
    return pl.pallas_call(
        _adaln_kernel,
        grid=(depth, n // bn),
        in_specs=[
            pl.BlockSpec((8, d), lambda l, j: (0, 0)),
            pl.BlockSpec((1, d, bn), lambda l, j: (l, 0, j)),
            pl.BlockSpec((1, 1, bn), lambda l, j: (l, 0, j)),
        ],
        out_specs=pl.BlockSpec((1, 8, bn), lambda l, j: (l, 0, j)),
        out_shape=jax.ShapeDtypeStruct((depth, 8, n), F32),
        compiler_params=_params(2),
        name="adaln",
    )(cond, ada_w, ada_b.reshape(depth, 1, n))


def _final_norm_kernel(h_ref, nw_ref, o_ref):
    x = h_ref[...]
    o_ref[...] = x * lax.rsqrt(jnp.mean(x * x, axis=-1, keepdims=True) + EPS) * nw_ref[...]


def _final_norm(h, norm_w, m_rows):
    d = h.shape[1]
    bm = 256
    return pl.pallas_call(
        _final_norm_kernel,
        grid=(m_rows // bm,),
        in_specs=[pl.BlockSpec((bm, d), lambda i: (i, 0)), pl.BlockSpec((1, d), lambda i: (0, 0))],
        out_specs=pl.BlockSpec((bm, d), lambda i: (i, 0)),
        out_shape=jax.ShapeDtypeStruct((m_rows, d), F32),
        compiler_params=_params(1),
        name="final_norm",
    )(h, norm_w.reshape(1, d))


def _mm_blocks(m_rows, n):
    bm = 1024 if m_rows % 1024 == 0 else 256
    bn = 1024 if n % 1024 == 0 else (512 if n % 512 == 0 else LANES)
    return bm, bn


def _accumulate_over_columns(ssq_ref, part, j):
    @pl.when(j == 0)
    def _():
        ssq_ref[...] = part

    @pl.when(j > 0)
    def _():
        ssq_ref[...] = ssq_ref[...] + part


def _next_norm_outputs(hb, j, midx, nw_ref, nscale_ref, xs_ref, ssq_ref):
    g = nw_ref[...] * (1.0 + nscale_ref[pl.ds(midx, 1), :])
    xs_ref[...] = (hb * g).astype(xs_ref.dtype)
    sq = hb * hb
    part = functools.reduce(jnp.add, [sq[:, c:c + LANES] for c in range(0, sq.shape[1], LANES)])
    _accumulate_over_columns(ssq_ref, part, j)


def _prep_kernel(x_ref, c_ref, nw_ref, nscale_ref, h_ref, xs_ref, ssq_ref, *, bm, lat_blocks, t_lat, seq, n_batch):
    i, j = pl.program_id(0), pl.program_id(1)
    midx = _mod_row(i * bm, t_lat, seq, n_batch)

    def emit(src_ref):
        hb = src_ref[...]
        h_ref[...] = hb
        _next_norm_outputs(hb, j, midx, nw_ref, nscale_ref, xs_ref, ssq_ref)

    pl.when(i < lat_blocks)(lambda: emit(x_ref))
    pl.when(i >= lat_blocks)(lambda: emit(c_ref))


def _prep(x2d, c2d, norm_w, mod_next, geom):
    t_lat, d = x2d.shape
    t_ctx = c2d.shape[0]
    t = t_lat + t_ctx
    bm = 512 if t_ctx % 512 == 0 else 256
    bn = 1024
    lat_blocks = t_lat // bm
    sc0 = d // bn
    return pl.pallas_call(
        functools.partial(_prep_kernel, bm=bm, lat_blocks=lat_blocks, **geom),
        grid=(t // bm, d // bn),
        in_specs=[
            pl.BlockSpec((bm, bn), lambda i, j: (jnp.minimum(i, lat_blocks - 1), j)),
            pl.BlockSpec((bm, bn), lambda i, j: (jnp.maximum(i - lat_blocks, 0), j)),
            pl.BlockSpec((1, bn), lambda i, j: (0, j)),
            pl.BlockSpec((8, bn), lambda i, j: (0, sc0 + j)),
        ],
        out_specs=[
            pl.BlockSpec((bm, bn), lambda i, j: (i, j)),
            pl.BlockSpec((bm, bn), lambda i, j: (i, j)),
            pl.BlockSpec((bm, LANES), lambda i, j: (i, 0)),
        ],
        out_shape=[
            jax.ShapeDtypeStruct((t, d), F32),
            jax.ShapeDtypeStruct((t, d), BF16),
            jax.ShapeDtypeStruct((t, LANES), F32),
        ],
        compiler_params=_params(2),
        name="prep",
    )(x2d, c2d, norm_w.reshape(1, d), mod_next)


def _mm_pre_kernel(*refs, bm, d_norm, shift_k, act, rope_row_blocks, rope_col_blocks, lead_blocks, lead_scale,
                   t_lat, seq, n_batch):
    xs_ref, ssq_ref, mod_ref, w_ref = refs[:4]
    pos = 4
    if rope_col_blocks:
        cos_ref, sin_ref = refs[pos:pos + 2]
        pos += 2
    o_ref, sw_ref = refs[pos], refs[pos + 1]
    i, j = pl.program_id(0), pl.program_id(1)
    bn = o_ref.shape[1]
    hn = min(bn, MM_INNER_COLS)

    @pl.when(i == 0)
    def _():
        shift = mod_ref[:, shift_k * d_norm:(shift_k + 1) * d_norm].astype(BF16)
        for c0 in range(0, bn, hn):
            sw_ref[j, :, c0:c0 + hn] = _dot(shift, w_ref[:, c0:c0 + hn].astype(BF16))

    midx = _mod_row(i * bm, t_lat, seq, n_batch)
    r = lax.rsqrt(jnp.sum(ssq_ref[...], axis=1, keepdims=True) * (1.0 / d_norm) + EPS)
    lead = jnp.where(j < lead_blocks, lead_scale, 1.0) if lead_blocks else 1.0
    r = r * lead
    do_rope = (i < rope_row_blocks) & (j < rope_col_blocks)
    for c0 in range(0, bn, hn):
        acc = _dot(xs_ref[...], w_ref[:, c0:c0 + hn].astype(BF16))
        acc = acc * r + sw_ref[j, pl.ds(midx, 1), c0:c0 + hn] * lead
        if act == "relu2":
            acc = jnp.maximum(acc, 0.0)
            acc = acc * acc
        if not rope_col_blocks:
            o_ref[:, c0:c0 + hn] = acc.astype(o_ref.dtype)
            continue

        @pl.when(do_rope)
        def _(acc=acc, c0=c0):
            hd = cos_ref.shape[1]
            lane = lax.broadcasted_iota(jnp.int32, (bm, hd), 1)
            first = (lane % (hd // 2)) < hd // 4
            for c in range(hn // hd):
                xh = acc[:, c * hd:(c + 1) * hd]
                partner = jnp.where(first, pltpu.roll(xh, hd - hd // 4, 1), pltpu.roll(xh, hd // 4, 1))
                o_ref[:, c0 + c * hd:c0 + (c + 1) * hd] = (xh * cos_ref[...] + partner * sin_ref[...]).astype(o_ref.dtype)

        @pl.when(jnp.logical_not(do_rope))
        def _(acc=acc, c0=c0):
            o_ref[:, c0:c0 + hn] = acc.astype(o_ref.dtype)


def _mm_pre(xs, ssq, mod_l, shift_k, w_stack, layer, m_rows, geom, act=None, rope=None, col_start=0, n_cols=None,
            out_dtype=BF16, lead=None):
    d = xs.shape[1]
    k = w_stack.shape[1]
    n = n_cols or w_stack.shape[2]
    bm, bn = _mm_blocks(m_rows, n)
    assert col_start % bn == 0
    cb0 = col_start // bn
    kern = functools.partial(
        _mm_pre_kernel, bm=bm, d_norm=d, shift_k=shift_k, act=act,
        lead_blocks=(lead[0] // bn if lead else 0), lead_scale=(lead[1] if lead else 1.0),
        rope_row_blocks=(rope[2] // bm if rope else 0), rope_col_blocks=(rope[3] // bn if rope else 0), **geom)
    in_specs = [
        pl.BlockSpec((bm, d), lambda i, j: (i, 0)),
        pl.BlockSpec((bm, LANES), lambda i, j: (i, 0)),
        pl.BlockSpec(mod_l.shape, lambda i, j: (0, 0)),
        pl.BlockSpec((None, k, bn), lambda i, j: (layer, 0, cb0 + j)),
    ]
    args = [xs, ssq, mod_l, w_stack]
    if rope:
        seq_blocks = rope[0].shape[0] // bm
        hd = rope[0].shape[1]
        in_specs += [pl.BlockSpec((bm, hd), lambda i, j: (i % seq_blocks, 0))] * 2
        args += [rope[0], rope[1]]
    return pl.pallas_call(
        kern,
        grid=(m_rows // bm, n // bn),
        in_specs=in_specs,
        out_specs=pl.BlockSpec((bm, bn), lambda i, j: (i, j)),
        out_shape=jax.ShapeDtypeStruct((m_rows, n), out_dtype),
        scratch_shapes=[pltpu.VMEM((n // bn, 8, bn), F32)],
        compiler_params=_params(2, VMEM_LIMIT_F32_WEIGHTS),
        name="mm_pre" + ("_" + act if act else "") + ("_rope" if rope else ""),
    )(*args)


def _mm_res_kernel(*refs, bm, n_k, row_off, emit_next, t_lat, seq, n_batch):
    x_ref, w_ref, res_ref, gate_ref = refs[:4]
    pos = 4
    if emit_next:
        nw_ref, nscale_ref = refs[pos:pos + 2]
        pos += 4
    o_ref = refs[pos]
    pos += 1
    if emit_next:
        xs_ref, ssq_ref = refs[pos:pos + 2]
        pos += 2
    acc_ref = refs[pos].at[pl.program_id(2)] if n_k > 1 else None
    i, kk, j = pl.program_id(0), pl.program_id(1), pl.program_id(2)
    midx = _mod_row(row_off + i * bm, t_lat, seq, n_batch)
    bn = o_ref.shape[1]
    slices = [slice(c0, c0 + MM_RES_INNER_COLS) for c0 in range(0, bn, MM_RES_INNER_COLS)]
    parts = [_dot(x_ref[...], w_ref[:, cs]) for cs in slices]

    def finish(accs):
        sums = None
        for cs, acc in zip(slices, accs):
            hb = res_ref[:, cs] + gate_ref[pl.ds(midx, 1), cs] * acc
            o_ref[:, cs] = hb
            if emit_next:
                g = nw_ref[:, cs] * (1.0 + nscale_ref[pl.ds(midx, 1), cs])
                xs_ref[:, cs] = (hb * g).astype(xs_ref.dtype)
                sq = hb * hb
                for c in range(0, sq.shape[1], LANES):
                    sums = sq[:, c:c + LANES] if sums is None else sums + sq[:, c:c + LANES]
        if emit_next:
            _accumulate_over_columns(ssq_ref, sums, j)

    if n_k == 1:
        finish(parts)
        return

    @pl.when(kk == 0)
    def _():
        for cs, part in zip(slices, parts):
            acc_ref[:, cs] = part

    @pl.when((kk > 0) & (kk < n_k - 1))
    def _():
        for cs, part in zip(slices, parts):
            acc_ref[:, cs] = acc_ref[:, cs] + part

    @pl.when(kk == n_k - 1)
    def _():
        finish([acc_ref[:, cs] + part for cs, part in zip(slices, parts)])


def _mm_res(x, w_stack, layer, bufs, mod_l, gate_k, m_rows, geom, row_off=0, next_norm=None):
    h, xs, ssq = bufs
    k, n = w_stack.shape[1:]
    n_k = 2 if k >= 8192 else 1
    kb = k // n_k
    bm = 1024 if m_rows % 1024 == 0 else 256
    bn = 512 if k >= 4096 else 1024
    assert row_off % bm == 0 and n % bn == 0
    blk_off = row_off // bm
    nb = n // bn
    emit_next = next_norm is not None

    def out_col(kk, j):
        return jnp.where(kk == n_k - 1, j, 0)

    in_specs = [
        pl.BlockSpec((bm, kb), lambda i, kk, j: (i, kk)),
        pl.BlockSpec((None, kb, bn), lambda i, kk, j: (layer, kk, j)),
        pl.BlockSpec((bm, bn), lambda i, kk, j: (blk_off + i, out_col(kk, j))),
        pl.BlockSpec((8, bn), lambda i, kk, j: (0, gate_k * nb + j)),
    ]
    args = [x, w_stack, h, mod_l]
    out_specs = [pl.BlockSpec((bm, bn), lambda i, kk, j: (blk_off + i, out_col(kk, j)))]
    out_shape = [jax.ShapeDtypeStruct(h.shape, F32)]
    aliases = {2: 0}
    if emit_next:
        nw, mod_next, scale_slot = next_norm
        in_specs += [
            pl.BlockSpec((1, bn), lambda i, kk, j: (0, j)),
            pl.BlockSpec((8, bn), lambda i, kk, j: (0, scale_slot * nb + j)),
            pl.BlockSpec(memory_space=pl.ANY),
            pl.BlockSpec(memory_space=pl.ANY),
        ]
        args += [nw.reshape(1, n), mod_next, xs, ssq]
        out_specs += [
            pl.BlockSpec((bm, bn), lambda i, kk, j: (blk_off + i, out_col(kk, j))),
            pl.BlockSpec((bm, LANES), lambda i, kk, j: (blk_off + i, 0)),
        ]
        out_shape += [jax.ShapeDtypeStruct(xs.shape, BF16), jax.ShapeDtypeStruct(ssq.shape, F32)]
        aliases.update({6: 1, 7: 2})
    outs = pl.pallas_call(
        functools.partial(_mm_res_kernel, bm=bm, n_k=n_k, row_off=row_off, emit_next=emit_next, **geom),
        grid=(m_rows // bm, n_k, nb),
        in_specs=in_specs,
        out_specs=out_specs,
        out_shape=out_shape,
        scratch_shapes=[pltpu.VMEM((nb, bm, bn), F32)] if n_k > 1 else [],
        input_output_aliases=aliases,
        compiler_params=_params(3),
        name="mm_res",
    )(*args)
    return (outs[0], outs[1], outs[2]) if emit_next else (outs[0], xs, ssq)


def _na_pair_layout(rows):
    gq, gk = NA_GROUP_ROWS, NA_KEY_ROWS
    pairs = {}
    idx = np.zeros((3, gq, gk // 2), np.int32)
    for typ, (r0, ks) in enumerate(((0, 0), (gq, 0), (rows - gq, rows - gk))):
        for i in range(gq):
            r = r0 + i
            rs = min(max(r - NA_WIN_ROWS // 2, 0), rows - NA_WIN_ROWS)
            dr = [kr - r + NA_WIN_ROWS - 1 if rs <= kr < rs + NA_WIN_ROWS else -1 for kr in range(ks, ks + gk)]
            for p in range(gk // 2):
                idx[typ, i, p] = pairs.setdefault((dr[2 * p], dr[2 * p + 1]), len(pairs))
    return list(pairs), idx


def _na_pair_table(rpb, pairs, out_scale):
    heads = rpb.shape[0]
    col = np.arange(GRID_W)
    col_start = np.clip(col - NA_WIN_COLS // 2, 0, GRID_W - NA_WIN_COLS)
    col_mask = (col[None, :] >= col_start[:, None]) & (col[None, :] < col_start[:, None] + NA_WIN_COLS)
    dc_idx = np.clip(col[None, :] - col[:, None], -(NA_WIN_COLS - 1), NA_WIN_COLS - 1) + NA_WIN_COLS - 1
    onehot = (dc_idx[None] == np.arange(2 * NA_WIN_COLS - 1)[:, None, None]).astype(np.float32)
    per_dr = jnp.einsum("hrd,dqk->hrqk", rpb.astype(F32), onehot, precision=lax.Precision.HIGHEST)
    per_dr = jnp.where(col_mask[None, None], per_dr * out_scale, MASK_NEG)
    masked = jnp.full((heads, GRID_W, GRID_W), MASK_NEG, F32)
    block = lambda dr: masked if dr < 0 else per_dr[:, dr]
    return jnp.stack([jnp.concatenate([block(a), block(b)], axis=-1) for a, b in pairs], axis=1)


def _softmax_pv(scores, values, out_dtype):
    m = functools.reduce(jnp.maximum, [jnp.max(t, axis=1, keepdims=True) for t in scores])
    dh = values[0].shape[1]
    o = None
    for t, v in zip(scores, values):
        p = jnp.exp2(t - m).astype(BF16)
        part = _dot(p, jnp.concatenate([v, jnp.ones_like(v)], axis=1))
        o = part if o is None else o + part
    return (o[:, :dh] * (1.0 / o[:, dh:])).astype(out_dtype)


def _na_kernel(idx_ref, q_ref, k0_ref, k1_ref, k2_ref, v0_ref, v1_ref, v2_ref, kc_ref, vc_ref, pair_ref, o_ref,
               *, n_groups):
    dh = NA_HEAD_DIM
    gq, n_pairs = NA_GROUP_ROWS, NA_KEY_ROWS // 2
    g = pl.program_id(1)
    typ = jnp.where(g == 0, 0, jnp.where(g == n_groups - 1, 2, 1))
    for hh in range(NA_HEADS_PER_STEP):
        sl = slice(hh * dh, (hh + 1) * dh)
        q = q_ref[:, sl]
        bias = jnp.concatenate(
            [jnp.concatenate([pair_ref[hh, idx_ref[(typ * gq + i) * n_pairs + p]] for p in range(n_pairs)], axis=1)
             for i in range(gq)], axis=0)
        s_loc = jnp.concatenate([_dot_nt(q, kr[:, sl]) for kr in (k0_ref, k1_ref, k2_ref)], axis=1)
        t_loc = s_loc + bias
        t_ctx = _dot_nt(q, kc_ref[:, sl])
        kb = k0_ref.shape[0]
        scores = [t_loc[:, j * kb:(j + 1) * kb] for j in range(3)] + [t_ctx]
        values = [vr[:, sl] for vr in (v0_ref, v1_ref, v2_ref, vc_ref)]
        o_ref[:, sl] = _softmax_pv(scores, values, o_ref.dtype)


def _na_ctx_kernel(q_ref, kc_ref, vc_ref, o_ref):
    dh = NA_HEAD_DIM
    for hh in range(NA_HEADS_PER_STEP):
        sl = slice(hh * dh, (hh + 1) * dh)
        t_ctx = _dot_nt(q_ref[:, sl], kc_ref[:, sl])
        o_ref[:, sl] = _softmax_pv([t_ctx], [vc_ref[:, sl]], o_ref.dtype)


def _na_attention(qkv, rpb, n_batch, seq, lc, ctx_out):
    hd = NA_HEADS * NA_HEAD_DIM
    gtok = NA_GROUP_ROWS * GRID_W
    ng = seq // gtok
    assert lc == gtok and ng >= 4
    hw = NA_HEADS_PER_STEP * NA_HEAD_DIM
    nhq = NA_HEADS // NA_HEADS_PER_STEP
    t_lat = n_batch * seq
    ctx_blk0 = n_batch * ng
    pairs, idx = _na_pair_layout(seq // GRID_W)
    pair_tab = _na_pair_table(rpb, pairs, math.log2(math.e))

    def k_row(g, b, j):
        return b * ng + jnp.clip(g - 1, 0, ng - 3) + j

    blk = (gtok, hw)
    in_specs = [pl.BlockSpec(memory_space=pltpu.SMEM), pl.BlockSpec(blk, lambda h, g, b: (b * ng + g, h))]
    for col0 in (nhq, 2 * nhq):
        for j in range(3):
            in_specs.append(pl.BlockSpec(blk, lambda h, g, b, j=j, col0=col0: (k_row(g, b, j), col0 + h)))
    in_specs.append(pl.BlockSpec(blk, lambda h, g, b: (ctx_blk0 + b, nhq + h)))
    in_specs.append(pl.BlockSpec(blk, lambda h, g, b: (ctx_blk0 + b, 2 * nhq + h)))
    in_specs.append(pl.BlockSpec((NA_HEADS_PER_STEP, len(pairs), GRID_W, 2 * GRID_W), lambda h, g, b: (h, 0, 0, 0)))
    y_lat = pl.pallas_call(
        functools.partial(_na_kernel, n_groups=ng),
        grid=(nhq, ng, n_batch),
        in_specs=in_specs,
        out_specs=pl.BlockSpec(blk, lambda h, g, b: (b * ng + g, h)),
        out_shape=jax.ShapeDtypeStruct((t_lat, hd), BF16),
        compiler_params=_params(3),
        name="na_attention",
    )(jnp.asarray(idx.reshape(-1)), *([qkv] * 9), pair_tab)
    if not ctx_out:
        return y_lat, None
    y_ctx = pl.pallas_call(
        _na_ctx_kernel,
        grid=(nhq, n_batch),
        in_specs=[
            pl.BlockSpec(blk, lambda h, b: (ctx_blk0 + b, h)),
            pl.BlockSpec(blk, lambda h, b: (ctx_blk0 + b, nhq + h)),
            pl.BlockSpec(blk, lambda h, b: (ctx_blk0 + b, 2 * nhq + h)),
        ],
        out_specs=pl.BlockSpec(blk, lambda h, b: (b, h)),
        out_shape=jax.ShapeDtypeStruct((n_batch * lc, hd), BF16),
        compiler_params=_params(2),
        name="na_attention_ctx",
    )(qkv, qkv, qkv)
    return y_lat, y_ctx


def _rope_tables(seq, width):
    t = jnp.arange(seq)
    pos = jnp.stack([t // GRID_W, t % GRID_W], axis=-1).astype(F32)
    n_freq = DA_HEAD_DIM // 4
    inv_freq = ROPE_BASE ** (-jnp.arange(n_freq, dtype=F32) / n_freq)
    ang = pos[:, :, None] * inv_freq
    cos = jnp.broadcast_to(jnp.cos(ang)[:, :, None, :], (seq, 2, 2, n_freq)).reshape(seq, DA_HEAD_DIM)
    sign = jnp.array([-1.0, 1.0], F32)[None, None, :, None]
    sin = (jnp.broadcast_to(jnp.sin(ang)[:, :, None, :], (seq, 2, 2, n_freq)) * sign).reshape(seq, DA_HEAD_DIM)
    reps = width // DA_HEAD_DIM
    return jnp.tile(cos, (1, reps)), jnp.tile(sin, (1, reps))


def _da_kernel(lam_ref, q_ref, *refs, out_scale, n_sub, with_latent):
    n_kv = 2 if with_latent else 1
    k_refs, v_refs = refs[:n_kv], refs[n_kv:2 * n_kv]
    sw_ref, o_ref = refs[2 * n_kv], refs[2 * n_kv + 1]
    dh = DA_HEAD_DIM
    lam = lam_ref[0]
    rows = q_ref.shape[0] // n_sub
    for sub in range(n_sub):
        rs = slice(sub * rows, (sub + 1) * rows)
        probs, sums = [], []
        for comp in range(2):
            sl = slice(comp * dh, (comp + 1) * dh)
            q = q_ref[rs, sl]
            s = [_dot_nt(q, k_ref[:, sl]) for k_ref in k_refs]
            m = functools.reduce(jnp.maximum, [jnp.max(x, axis=1, keepdims=True) for x in s])
            p = [jnp.exp2(x - m) for x in s]
            probs.append(p)
            sums.append(functools.reduce(jnp.add, [jnp.sum(x, axis=1, keepdims=True) for x in p]))
        ratio = lam * sums[0] / sums[1]
        o = None
        for p1, p2, v_ref in zip(probs[0], probs[1], v_refs):
            part = _dot((p1 - ratio * p2).astype(BF16), v_ref[...])
            o = part if o is None else o + part
        o = o * (1.0 / sums[0])
        y = o * lax.rsqrt(jnp.mean(o * o, axis=-1, keepdims=True) + DA_SUBLN_EPS) * sw_ref[...]
        o_ref[rs, :] = (y * out_scale).astype(o_ref.dtype)


def _diff_attention(qkv, lam_full, subln_w, lambda_init, n_batch, seq, lc, ctx_out):
    hw = 2 * DA_HEAD_DIM
    bq, n_sub = DA_Q_BLOCK, DA_Q_SUBBLOCKS
    nq = seq // bq
    t_lat = n_batch * seq
    ctx_blk0 = t_lat // lc
    lam1 = lam_full.reshape(1).astype(F32)
    sw = subln_w.reshape(1, hw)
    common = dict(out_scale=1.0 - lambda_init)
    smem = pl.BlockSpec(memory_space=pltpu.SMEM)
    y_lat = pl.pallas_call(
        functools.partial(_da_kernel, n_sub=n_sub, with_latent=True, **common),
        grid=(n_batch, DA_HEADS, nq),
        in_specs=[
            smem,
            pl.BlockSpec((bq, hw), lambda b, h, i: (b * nq + i, h)),
            pl.BlockSpec((seq, hw), lambda b, h, i: (b, DA_HEADS + h)),
            pl.BlockSpec((lc, hw), lambda b, h, i: (ctx_blk0 + b, DA_HEADS + h)),
            pl.BlockSpec((seq, hw), lambda b, h, i: (b, 2 * DA_HEADS + h)),
            pl.BlockSpec((lc, hw), lambda b, h, i: (ctx_blk0 + b, 2 * DA_HEADS + h)),
            pl.BlockSpec((1, hw), lambda b, h, i: (0, 0)),
        ],
        out_specs=pl.BlockSpec((bq, hw), lambda b, h, i: (b * nq + i, h)),
        out_shape=jax.ShapeDtypeStruct((t_lat, DA_HEADS * hw), BF16),
        compiler_params=_params(3, VMEM_LIMIT_F32_WEIGHTS),
        name="diff_attention",
    )(lam1, qkv, qkv, qkv, qkv, qkv, sw)
    if not ctx_out:
        return y_lat, None
    y_ctx = pl.pallas_call(
        functools.partial(_da_kernel, n_sub=1, with_latent=False, **common),
        grid=(n_batch, DA_HEADS),
        in_specs=[
            smem,
            pl.BlockSpec((lc, hw), lambda b, h: (ctx_blk0 + b, h)),
            pl.BlockSpec((lc, hw), lambda b, h: (ctx_blk0 + b, DA_HEADS + h)),
            pl.BlockSpec((lc, hw), lambda b, h: (ctx_blk0 + b, 2 * DA_HEADS + h)),
            pl.BlockSpec((1, hw), lambda b, h: (0, 0)),
        ],
        out_specs=pl.BlockSpec((lc, hw), lambda b, h: (b, h)),
        out_shape=jax.ShapeDtypeStruct((n_batch * lc, DA_HEADS * hw), BF16),
        compiler_params=_params(2),
        name="diff_attention_ctx",
    )(lam1, qkv, qkv, qkv, sw)
    return y_lat, y_ctx


def _conv_kernel(prev_ref, cur_ref, next_ref, w_ref, b_ref, o_ref, *, rb, lat_blocks, lat_blocks_per_seq, ctx_blocks_per_seq):
    i = pl.program_id(0)
    is_lat = i < lat_blocks
    pos = jnp.where(is_lat, i % lat_blocks_per_seq, (i - lat_blocks) % ctx_blocks_per_seq)
    per_seq = jnp.where(is_lat, lat_blocks_per_seq, ctx_blocks_per_seq)
    halo = prev_ref.shape[0]
    prev = jnp.where(pos == 0, 0.0, prev_ref[...].astype(F32))
    nxt = jnp.where(pos == per_seq - 1, 0.0, next_ref[...].astype(F32))
    cat = jnp.concatenate([prev, cur_ref[...].astype(F32), nxt], axis=0)
    n = rb + 2 * halo
    acc = b_ref[...]
    for k in range(SSM_CONV):
        shifted = cat if k == SSM_CONV // 2 else pltpu.roll(cat, (SSM_CONV // 2 - k) % n, 0)
        acc = acc + w_ref[k:k + 1, :] * shifted[halo:halo + rb]
    o_ref[...] = (acc * _sigmoid(acc)).astype(o_ref.dtype)


def _ssm_conv(proj, conv_w, conv_b, n_batch, seq, lc):
    t = proj.shape[0]
    rb, ct, halo = 256, 2048, SUBLANES_BF16
    assert seq % rb == 0 and lc % rb == 0 and SSM_D_INNER % ct == 0 and SSM_CONV_DIM % ct == 0
    col0 = SSM_D_INNER // ct
    hb = rb // halo
    n_halo_blocks = t // halo
    kern = functools.partial(_conv_kernel, rb=rb, lat_blocks=n_batch * seq // rb,
                             lat_blocks_per_seq=seq // rb, ctx_blocks_per_seq=lc // rb)
    return pl.pallas_call(
        kern,
        grid=(t // rb, SSM_CONV_DIM // ct),
        in_specs=[
            pl.BlockSpec((halo, ct), lambda i, j: (jnp.maximum(i * hb - 1, 0), col0 + j)),
            pl.BlockSpec((rb, ct), lambda i, j: (i, col0 + j)),
            pl.BlockSpec((halo, ct), lambda i, j: (jnp.minimum((i + 1) * hb, n_halo_blocks - 1), col0 + j)),
            pl.BlockSpec((SSM_CONV, ct), lambda i, j: (0, j)),
            pl.BlockSpec((1, ct), lambda i, j: (0, j)),
        ],
        out_specs=pl.BlockSpec((rb, ct), lambda i, j: (i, j)),
        out_shape=jax.ShapeDtypeStruct((t, SSM_CONV_DIM), BF16),
        compiler_params=_params(2),
        name="ssm_conv",
    )(proj, proj, proj, conv_w, conv_b.reshape(1, SSM_CONV_DIM))


def _ssd_kernel(xbc_ref, dt_ref, dtb_ref, alog_ref, tri_ref, half_ref, y_ref, state_ref):
    q = SSM_CHUNK
    nh = SSM_HEADS
    p2 = 2 * SSM_HEAD_DIM
    gw = SSM_GROUP_WIDTH
    log2e = math.log2(math.e)
    direction = pl.program_id(1)
    fwd = direction == 0

    @pl.when(pl.program_id(2) == 0)
    def _():
        state_ref[...] = jnp.zeros_like(state_ref)

    pre = dt_ref[...] + dtb_ref[...]
    dt_all = jnp.maximum(pre, 0.0) + jnp.log1p(jnp.exp(-jnp.abs(pre)))
    a_all = dt_all * (jnp.exp(alog_ref[...]) * -log2e)
    tri = tri_ref[0]
    acum_all = sum(_dot(tri, piece) for piece in _split3(a_all))
    acum = jnp.where(fwd, acum_all, pltpu.roll(acum_all, nh, 1))
    acum_t = acum.T
    dt_t = jnp.where(fwd, dt_all, pltpu.roll(dt_all, nh, 1)).T
    src_t = acum_t - jnp.log(dt_t) * log2e
    lane = lax.broadcasted_iota(jnp.int32, (q, q), 1)
    last = jnp.where(fwd, q - 1, 0)
    total = jnp.sum(jnp.where(lane == last, acum_t, 0.0), axis=1, keepdims=True)
    to_end_t = dt_t * jnp.exp2(total - acum_t)
    chunk_decay = jnp.broadcast_to(jnp.exp2(total), (q, q))
    seen = tri.astype(F32) > 0.5
    first_head = lane < SSM_HEAD_DIM
    c_off = SSM_D_INNER + SSM_GROUPS * SSM_STATE

    for g in range(SSM_GROUPS):
        b_g = xbc_ref[:, SSM_D_INNER + g * SSM_STATE:SSM_D_INNER + (g + 1) * SSM_STATE]
        c_g = xbc_ref[:, c_off + g * SSM_STATE:c_off + (g + 1) * SSM_STATE]
        cb = _dot_nt(c_g, b_g)
        b_t = b_g.astype(F32).T
        y_carried = _dot(c_g, state_ref[g].astype(BF16))
        for pr in range(SSM_HEADS_PER_GROUP // 2):
            ps = slice(pr * p2, (pr + 1) * p2)
            cols = slice(g * gw + pr * p2, g * gw + (pr + 1) * p2)
            x2 = xbc_ref[:, cols]
            x_bd = jnp.concatenate([x2 * half_ref[0], x2 * half_ref[1]], axis=0)
            intra, to_state, carry_scale, decay = [], [], [], []
            for r in (2 * pr, 2 * pr + 1):
                hh = g * SSM_HEADS_PER_GROUP + r
                col = jnp.sum(jnp.where(lane == hh, acum, 0.0), axis=1, keepdims=True)
                intra.append((cb * jnp.where(seen, jnp.exp2(col - src_t[hh:hh + 1, :]), 0.0)).astype(BF16))
                to_state.append((b_t * to_end_t[hh:hh + 1, :]).astype(BF16))
                carry_scale.append(jnp.exp2(col))
                decay.append(chunk_decay[hh:hh + 1, :])
            y_ref[0, :, cols] = (_dot(jnp.concatenate(intra, axis=1), x_bd)
                                 + jnp.where(first_head, carry_scale[0], carry_scale[1]) * y_carried[:, ps]
                                 ).astype(y_ref.dtype)
            state_ref[g, :, ps] = (state_ref[g, :, ps] * jnp.where(first_head[:1], decay[0], decay[1])
                                   + _dot(jnp.concatenate(to_state, axis=1), x_bd))


def _ssd_scan(xbc, dt_raw, dt_bias, a_log, n_batch, seq, lc):
    t = xbc.shape[0]
    q = SSM_CHUNK
    n_lat, n_ctx = seq // q, lc // q
    steps = n_lat + n_ctx
    ctx_blk0 = n_batch * n_lat

    def blk(b, d, s):
        ctx_chunk = jnp.where(d == 0, s, n_ctx - 1 - s)
        lat_chunk = jnp.where(d == 0, s - n_ctx, steps - 1 - s)
        return jnp.where(s < n_ctx, ctx_blk0 + b * n_ctx + ctx_chunk, b * n_lat + lat_chunk)

    idx = np.arange(q)
    tri = np.stack([idx[None, :] <= idx[:, None], idx[None, :] >= idx[:, None]]).astype(np.float32)
    first = np.broadcast_to(np.arange(2 * SSM_HEAD_DIM)[None, :] < SSM_HEAD_DIM, (q, 2 * SSM_HEAD_DIM))
    half = np.stack([first, ~first]).astype(np.float32)
    return pl.pallas_call(
        _ssd_kernel,
        grid=(n_batch, 2, steps),
        in_specs=[
            pl.BlockSpec((q, SSM_CONV_DIM), lambda b, d, s: (blk(b, d, s), 0)),
            pl.BlockSpec((q, 2 * SSM_HEADS), lambda b, d, s: (blk(b, d, s), 0)),
            pl.BlockSpec((1, 2 * SSM_HEADS), lambda b, d, s: (0, 0)),
            pl.BlockSpec((1, 2 * SSM_HEADS), lambda b, d, s: (0, 0)),
            pl.BlockSpec((1, q, q), lambda b, d, s: (d, 0, 0)),
            pl.BlockSpec((2, q, 2 * SSM_HEAD_DIM), lambda b, d, s: (0, 0, 0)),
        ],
        out_specs=pl.BlockSpec((1, q, SSM_D_INNER), lambda b, d, s: (d, blk(b, d, s), 0)),
        out_shape=jax.ShapeDtypeStruct((2, t, SSM_D_INNER), BF16),
        scratch_shapes=[pltpu.VMEM((SSM_GROUPS, SSM_STATE, SSM_GROUP_WIDTH), F32)],
        compiler_params=_params(3),
        name="ssd_scan",
    )(xbc, dt_raw, dt_bias.reshape(1, -1), a_log.reshape(1, -1), jnp.asarray(tri, BF16), jnp.asarray(half, BF16))


def _ssm_finish_kernel(yf_ref, yb_ref, x_ref, z_ref, dsk_ref, nw_ref, o_ref):
    gw = SSM_GROUP_WIDTH
    z = z_ref[...].astype(F32)
    y = yf_ref[0].astype(F32) + yb_ref[0].astype(F32) + x_ref[...].astype(F32) * dsk_ref[...]
    y = y * (z * _sigmoid(z))
    for g in range(SSM_GROUPS):
        cs = slice(g * gw, (g + 1) * gw)
        yg = y[:, cs]
        o_ref[:, cs] = (yg * lax.rsqrt(jnp.mean(yg * yg, axis=-1, keepdims=True) + EPS) * nw_ref[:, cs]).astype(o_ref.dtype)


def _ssm_finish(y2, xbc, proj, d_skip, norm_w):
    t = xbc.shape[0]
    di = SSM_D_INNER
    rb = 256
    dsk = jnp.repeat(d_skip.astype(F32), SSM_HEAD_DIM).reshape(1, di)
    return pl.pallas_call(
        _ssm_finish_kernel,
        grid=(t // rb,),
        in_specs=[
            pl.BlockSpec((1, rb, di), lambda i: (0, i, 0)),
            pl.BlockSpec((1, rb, di), lambda i: (1, i, 0)),
            pl.BlockSpec((rb, di), lambda i: (i, 0)),
            pl.BlockSpec((rb, di), lambda i: (i, 0)),
            pl.BlockSpec((1, di), lambda i: (0, 0)),
            pl.BlockSpec((1, di), lambda i: (0, 0)),
        ],
        out_specs=pl.BlockSpec((rb, di), lambda i: (i, 0)),
        out_shape=jax.ShapeDtypeStruct((t, di), BF16),
        compiler_params=_params(1),
        name="ssm_finish",
    )(y2, y2, xbc, proj, dsk, norm_w.reshape(1, di))


def _mamba2(xs, ssq, mod_l, geom, w_in_stack, layer, conv_w, conv_b, dt_bias, a_log, d_skip, norm_w, n_batch, seq, lc):
    t = xs.shape[0]
    n_main = SSM_D_INNER + SSM_CONV_DIM
    proj = _mm_pre(xs, ssq, mod_l, 0, w_in_stack, layer, t, geom, n_cols=n_main)
    dt_raw = _mm_pre(xs, ssq, mod_l, 0, w_in_stack, layer, t, geom, col_start=n_main, n_cols=2 * SSM_HEADS,
                     out_dtype=F32)
    xbc = _ssm_conv(proj, conv_w, conv_b, n_batch, seq, lc)
    y2 = _ssd_scan(xbc, dt_raw, dt_bias, a_log, n_batch, seq, lc)
    return _ssm_finish(y2, xbc, proj, d_skip, norm_w)


def kernel(x, c, ctx, c_ctx, ada_w, ada_b, norm_mix_w, norm_mlp_w, mlp_w1, mlp_w2, na_w_qkv, na_w_o, na_rpb,
           ssm_w_in, ssm_conv_w, ssm_conv_b, ssm_dt_bias, ssm_a_log, ssm_d, ssm_norm_w, ssm_w_out,
           da_w_qkv, da_w_o, da_lambda, da_subln_w, final_norm_w):
    n_batch, seq, d = x.shape
    lc = ctx.shape[1]
    t_lat, t_ctx = n_batch * seq, n_batch * lc
    t = t_lat + t_ctx
    geom = dict(t_lat=t_lat, seq=seq, n_batch=n_batch)
    assert d == D_MODEL and seq % GRID_W == 0 and n_batch < 8

    cond = jnp.concatenate([c, c_ctx[None, :], jnp.zeros((8 - n_batch - 1, d), F32)], axis=0)
    mod = _adaln_all(cond, ada_w, ada_b)
    bufs = _prep(x.reshape(t_lat, d), ctx.reshape(t_ctx, d), norm_mix_w[0], mod[0], geom)
    mlp_w2_bf16 = mlp_w2.astype(BF16)

    for i in range(DEPTH):
        last = i == DEPTH - 1
        mixer, j = i % N_MIXERS, i // N_MIXERS
        y_ctx = None
        mod_l = mod[i]
        rows_out = t_lat if last else t
        _, xs, ssq = bufs
        if mixer == 0:
            qkv = _mm_pre(xs, ssq, mod_l, 0, na_w_qkv, j, t, geom, lead=(NA_HEADS * NA_HEAD_DIM, ATTN_EXP2_SCALE))
            y, y_ctx = _na_attention(qkv, na_rpb[j], n_batch, seq, lc, not last)
            w_o = na_w_o
        elif mixer == 1:
            y = _mamba2(xs, ssq, mod_l, geom, ssm_w_in, j, ssm_conv_w[j], ssm_conv_b[j], ssm_dt_bias[j],
                        ssm_a_log[j], ssm_d[j], ssm_norm_w[j], n_batch, seq, lc)
            w_o = ssm_w_out
        else:
            lambda_init = 0.8 - 0.6 * math.exp(-0.3 * i)
            qk_w = 2 * DA_HEADS * DA_HEAD_DIM
            cos_t, sin_t = _rope_tables(seq, DA_HEAD_DIM)
            qkv = _mm_pre(xs, ssq, mod_l, 0, da_w_qkv, j, t, geom, rope=(cos_t, sin_t, t_lat, 2 * qk_w),
                          lead=(qk_w, ATTN_EXP2_SCALE))
            lam = da_lambda[j].astype(F32)
            lam_full = jnp.exp(jnp.sum(lam[0] * lam[1])) - jnp.exp(jnp.sum(lam[2] * lam[3])) + lambda_init
            y, y_ctx = _diff_attention(qkv, lam_full, da_subln_w[j], lambda_init, n_batch, seq, lc, not last)
            w_o = da_w_o
        w_o = w_o.astype(BF16)
        mlp_norm = (norm_mlp_w[i], mod_l, 4)
        if y_ctx is None:
            bufs = _mm_res(y, w_o, j, bufs, mod_l, 2, rows_out, geom, next_norm=mlp_norm)
        else:
            bufs = _mm_res(y, w_o, j, bufs, mod_l, 2, t_lat, geom, next_norm=mlp_norm)
            bufs = _mm_res(y_ctx, w_o, j, bufs, mod_l, 2, t_ctx, geom, row_off=t_lat, next_norm=mlp_norm)
        hidden = _mm_pre(bufs[1], bufs[2], mod_l, 3, mlp_w1, i, rows_out, geom, act="relu2")
        mixer_norm = None if last else (norm_mix_w[i + 1], mod[i + 1], 1)
        bufs = _mm_res(hidden, mlp_w2_bf16, i, bufs, mod_l, 5, rows_out, geom, next_norm=mixer_norm)

    return _final_norm(bufs[0], final_norm_w, t_lat).reshape(n_batch, seq, d)
```
